```python
import jax
import jax.numpy as jnp
from jax import lax
import numpy as np

D_MODEL = 1024
BATCH = 8
SEQ = 2048
DEPTH = 1
DEC_BATCH = 128
DEC_SEQ = 8
PAST_LEN = 16384
PAGE_SIZE = 128

N_META = 16
D_MIX = D_MODEL
SSD_WIDTH = D_MIX // 2
SSD_HEAD_DIM = 64
SSD_HEADS = SSD_WIDTH // SSD_HEAD_DIM
SSD_GROUPS = 2
SSD_STATE = 128
CONV_W = 4
SSD_CHUNK = 128
CONV_DIM = SSD_WIDTH + 2 * SSD_GROUPS * SSD_STATE
SSD_IN = SSD_WIDTH + CONV_DIM + SSD_HEADS
RWKV_WIDTH = D_MIX - SSD_WIDTH
RWKV_HEAD_DIM = 64
RWKV_HEADS = RWKV_WIDTH // RWKV_HEAD_DIM
DECAY_LORA = 64
AAA_LORA = 64
GATE_LORA = 128
RWKV_IN = 3 * RWKV_WIDTH + DECAY_LORA + AAA_LORA + GATE_LORA
D_IN = SSD_IN + RWKV_IN
PEER_HEADS = 8
N_KEYS = 128
N_EXPERTS = N_KEYS * N_KEYS
PEER_TOPK = 16
D_KEY = 256
PEER_BLOCK = 128
RMS_EPS = 1e-6
GN_EPS = 64e-5
F32 = jnp.float32

kernel_name = 'hymba_ssd_rwkv7_peer_step'


def rmsnorm(x, g):
    x32 = x.astype(F32)
    y = x32 * lax.rsqrt(jnp.mean(x32 * x32, axis=-1, keepdims=True) + RMS_EPS)
    return (y * g.astype(F32)).astype(x.dtype)


def causal_conv(u, buf, w, b):
    L = u.shape[1]
    full = jnp.concatenate([buf.astype(u.dtype), u], axis=1)
    y = b + full[:, 0:L] * w[0]
    for k in range(1, CONV_W):
        y = y + full[:, k:k + L] * w[k]
    return y, full[:, L:]


def chunk_len(L):
    return SSD_CHUNK if L % SSD_CHUNK == 0 else L


def ssd_chunked(xs, dt, A, Bm, Cm, h0, chunk):
    b, L, H, P = xs.shape
    N = Bm.shape[-1]
    nc = L // chunk
    xdt = (xs * dt[..., None]).reshape(b, nc, chunk, H, P)
    Bc = Bm.reshape(b, nc, chunk, H, N)
    Cc = Cm.reshape(b, nc, chunk, H, N)
    acum = jnp.cumsum((dt * A).reshape(b, nc, chunk, H), axis=2)
    causal = jnp.tril(jnp.ones((chunk, chunk), dtype=bool))[None, None, :, :, None]
    seg = acum[:, :, :, None, :] - acum[:, :, None, :, :]
    decay_ls = jnp.exp(jnp.where(causal, seg, -jnp.inf))
    scores = jnp.einsum('bclhn,bcshn->bclsh', Cc, Bc) * decay_ls
    y_diag = jnp.einsum('bclsh,bcshp->bclhp', scores, xdt)
    to_end = jnp.exp(acum[:, :, -1:, :] - acum)
    chunk_states = jnp.einsum('bclhn,bclh,bclhp->bchpn', Bc, to_end, xdt)
    chunk_decay = jnp.exp(acum[:, :, -1, :])

    def step(h, inp):
        s_c, d_c = inp
        return h * d_c[:, :, None, None] + s_c, h

    h_last, h_prev = lax.scan(step, h0, (jnp.moveaxis(chunk_states, 1, 0), jnp.moveaxis(chunk_decay, 1, 0)))
    h_prev = jnp.moveaxis(h_prev, 0, 1)
    y_off = jnp.einsum('bclhn,bchpn,bclh->bclhp', Cc, h_prev, jnp.exp(acum))
    return (y_diag + y_off).reshape(b, L, H, P), h_last


def ssd_group(p, conv_buf, h0, segments, conv_w, conv_b, dt_bias, a_log, d_skip, ssm_norm):
    bsz, L, _ = p.shape
    z = p[..., :SSD_WIDTH]
    xbc_raw = p[..., SSD_WIDTH:SSD_WIDTH + CONV_DIM]
    dt_raw = p[..., SSD_WIDTH + CONV_DIM:SSD_IN]
    xbc, new_buf = causal_conv(xbc_raw, conv_buf, conv_w, conv_b)
    xbc = jax.nn.silu(xbc.astype(F32))
    gn = SSD_GROUPS * SSD_STATE
    xs = xbc[..., :SSD_WIDTH].reshape(bsz, L, SSD_HEADS, SSD_HEAD_DIM)
    rep = SSD_HEADS // SSD_GROUPS
    Bm = jnp.repeat(xbc[..., SSD_WIDTH:SSD_WIDTH + gn].reshape(bsz, L, SSD_GROUPS, SSD_STATE), rep, axis=2)
    Cm = jnp.repeat(xbc[..., SSD_WIDTH + gn:].reshape(bsz, L, SSD_GROUPS, SSD_STATE), rep, axis=2)
    dt = jax.nn.softplus(dt_raw.astype(F32) + dt_bias.astype(F32))
    A = -jnp.exp(a_log.astype(F32))
    h = h0.astype(F32)
    ys = []
    start = 0
    for seg in segments:
        y_seg, h = ssd_chunked(xs[:, start:start + seg], dt[:, start:start + seg], A,
                               Bm[:, start:start + seg], Cm[:, start:start + seg], h, chunk_len(seg))
        ys.append(y_seg)
        start += seg
    y = jnp.concatenate(ys, axis=1)
    y = (y + d_skip.astype(F32)[:, None] * xs).reshape(bsz, L, SSD_WIDTH)
    y = rmsnorm(y * jax.nn.silu(z.astype(F32)), ssm_norm)
    return y, new_buf, h


def rwkv_group(p, shift_buf, s0, shift_mu, decay_w0, decay_w2, iclr_a0, iclr_a2, gate_g2, k_k, k_a, r_k, lnx_w, lnx_b):
    bsz, L, _ = p.shape
    prev = jnp.concatenate([shift_buf[:, None].astype(p.dtype), p[:, :-1]], axis=1)
    pm = p + (prev - p) * shift_mu
    new_shift = p[:, -1]
    rw = RWKV_WIDTH
    r, k, v, wl, al, gl = jnp.split(pm, [rw, 2 * rw, 3 * rw, 3 * rw + DECAY_LORA, 3 * rw + DECAY_LORA + AAA_LORA], axis=-1)
    w = -jax.nn.softplus(-(decay_w0.astype(F32) + (jnp.tanh(wl) @ decay_w2).astype(F32))) - 0.5
    a = jax.nn.sigmoid(iclr_a0.astype(F32) + (al @ iclr_a2).astype(F32))
    g = (jax.nn.sigmoid(gl) @ gate_g2).astype(F32)

    def heads(t):
        return t.astype(F32).reshape(bsz, L, RWKV_HEADS, RWKV_HEAD_DIM)

    r, k, v, w, a = heads(r), heads(k), heads(v), heads(w), heads(a)
    kk = k * k_k.astype(F32).reshape(RWKV_HEADS, RWKV_HEAD_DIM)
    kk = kk / jnp.maximum(jnp.sqrt(jnp.sum(kk * kk, axis=-1, keepdims=True)), 1e-12)
    k = k * (1.0 + (a - 1.0) * k_a.astype(F32).reshape(RWKV_HEADS, RWKV_HEAD_DIM))
    decay = jnp.exp(-jnp.exp(w))

    def step(S, inp):
        r_t, d_t, k_t, v_t, kk_t, a_t = inp
        sa = jnp.einsum('bhvk,bhk->bhv', S, kk_t)
        S = S * d_t[:, :, None, :] - sa[..., None] * (kk_t * a_t)[:, :, None, :] + v_t[..., None] * k_t[:, :, None, :]
        return S, jnp.einsum('bhvk,bhk->bhv', S, r_t)

    tm = lambda t: jnp.moveaxis(t, 1, 0)
    S, y = lax.scan(step, s0.astype(F32), (tm(r), tm(decay), tm(k), tm(v), tm(kk), tm(a)))
    y = jnp.moveaxis(y, 0, 1)
    mu = jnp.mean(y, axis=-1, keepdims=True)
    var = jnp.mean(jnp.square(y - mu), axis=-1, keepdims=True)
    yn = ((y - mu) * lax.rsqrt(var + GN_EPS)).reshape(bsz, L, rw) * lnx_w.astype(F32) + lnx_b.astype(F32)
    bonus = (jnp.sum(r * k * r_k.astype(F32), axis=-1, keepdims=True) * v).reshape(bsz, L, rw)
    return (yn + bonus) * g, new_shift, S


def mixer_sublayer(h, conv_buf, ssd_h, shift_buf, wkv_s, segments, mp):
    (norm_mix, w_in, conv_w, conv_b, dt_bias, a_log, d_skip, ssm_norm, shift_mu, decay_w0, decay_w2,
     iclr_a0, iclr_a2, gate_g2, k_k, k_a, r_k, lnx_w, lnx_b, w_out) = mp
    u = rmsnorm(h, norm_mix)
    p = u @ w_in
    y_a, conv_new, ssd_new = ssd_group(p[..., :SSD_IN], conv_buf, ssd_h, segments, conv_w, conv_b, dt_bias, a_log, d_skip, ssm_norm)
    y_b, shift_new, wkv_new = rwkv_group(p[..., SSD_IN:], shift_buf, wkv_s, shift_mu, decay_w0, decay_w2,
                                         iclr_a0, iclr_a2, gate_g2, k_k, k_a, r_k, lnx_w, lnx_b)
    y = jnp.concatenate([y_a.astype(h.dtype), y_b.astype(h.dtype)], axis=-1) @ w_out
    return h + y, conv_new, ssd_new, shift_new, wkv_new


def peer(u, w_query, sub_keys, expert_u, expert_v):
    lead = u.shape[:-1]
    t = u.reshape(-1, D_MODEL)
    n = t.shape[0]
    nb = -(-n // PEER_BLOCK)
    t = jnp.pad(t, ((0, nb * PEER_BLOCK - n), (0, 0))).reshape(nb, PEER_BLOCK, D_MODEL)
    half = D_KEY // 2

    def block(tb):
        q = (tb @ w_query).astype(F32).reshape(PEER_BLOCK, PEER_HEADS, 2, half)
        s1 = jnp.einsum('thd,hkd->thk', q[:, :, 0], sub_keys[0].astype(F32))
        s2 = jnp.einsum('thd,hkd->thk', q[:, :, 1], sub_keys[1].astype(F32))
        v1, i1 = lax.top_k(s1, PEER_TOPK)
        v2, i2 = lax.top_k(s2, PEER_TOPK)
        cand = (v1[..., :, None] + v2[..., None, :]).reshape(PEER_BLOCK, PEER_HEADS, PEER_TOPK * PEER_TOPK)
        top_s, pos = lax.top_k(cand, PEER_TOPK)
        e1 = jnp.take_along_axis(i1, pos // PEER_TOPK, axis=-1)
        e2 = jnp.take_along_axis(i2, pos % PEER_TOPK, axis=-1)
        experts = e1 * N_KEYS + e2
        gate = jax.nn.softmax(top_s, axis=-1)
        ue = expert_u[experts]
        act = jax.nn.gelu(jnp.einsum('thkd,td->thk', ue, tb).astype(F32), approximate=False) * gate
        return jnp.einsum('thk,thkd->td', act.astype(tb.dtype), expert_v[experts])

    out = lax.map(block, t).reshape(-1, D_MODEL)[:n]
    return out.reshape(*lead, D_MODEL)


def setup_inputs(seed: int = 0) -> dict:
    key = jax.random.key(seed)
    ks = jax.random.split(key, 40)

    def nrm(k, shape, scale):
        return jax.random.normal(k, shape, F32) * scale

    dt0 = jnp.exp(jax.random.uniform(ks[9], (DEPTH, SSD_HEADS), F32, np.log(1e-3), np.log(1e-1)))
    return {
        'x_prompt': nrm(ks[0], (BATCH, SEQ, D_MODEL), 1.0),
        'x_sample': nrm(ks[1], (DEC_BATCH, DEC_SEQ, D_MODEL), 1.0),
        'state_conv': nrm(ks[2], (DEPTH, DEC_BATCH, CONV_W - 1, CONV_DIM), 1.0),
        'state_ssd': nrm(ks[3], (DEPTH, DEC_BATCH, SSD_HEADS, SSD_HEAD_DIM, SSD_STATE), 0.1),
        'state_shift': nrm(ks[4], (DEPTH, DEC_BATCH, RWKV_IN), 1.0),
        'state_wkv': nrm(ks[5], (DEPTH, DEC_BATCH, RWKV_HEADS, RWKV_HEAD_DIM, RWKV_HEAD_DIM), 0.1),
        'meta_tokens': nrm(ks[6], (N_META, D_MODEL), 1.0),
        'norm_mix': 1.0 + nrm(ks[7], (DEPTH, D_MODEL), 0.02),
        'w_in': nrm(ks[8], (DEPTH, D_MODEL, D_IN), D_MODEL ** -0.5),
        'conv_w': nrm(ks[10], (DEPTH, CONV_W, CONV_DIM), 0.5),
        'conv_b': nrm(ks[11], (DEPTH, CONV_DIM), 0.02),
        'dt_bias': dt0 + jnp.log(-jnp.expm1(-dt0)),
        'a_log': jnp.log(jax.random.uniform(ks[12], (DEPTH, SSD_HEADS), F32, 1.0, 16.0)),
        'd_skip': 1.0 + nrm(ks[13], (DEPTH, SSD_HEADS), 0.1),
        'ssm_norm': 1.0 + nrm(ks[14], (DEPTH, SSD_WIDTH), 0.02),
        'shift_mu': jax.random.uniform(ks[15], (DEPTH, RWKV_IN), F32, 0.0, 1.0),
        'decay_w0': jax.random.uniform(ks[16], (DEPTH, RWKV_WIDTH), F32, -6.0, 1.0),
        'decay_w2': nrm(ks[17], (DEPTH, DECAY_LORA, RWKV_WIDTH), 0.5 * DECAY_LORA ** -0.5),
        'iclr_a0': nrm(ks[18], (DEPTH, RWKV_WIDTH), 0.1),
        'iclr_a2': nrm(ks[19], (DEPTH, AAA_LORA, RWKV_WIDTH), AAA_LORA ** -0.5),
        'gate_g2': nrm(ks[20], (DEPTH, GATE_LORA, RWKV_WIDTH), GATE_LORA ** -0.5),
        'k_k': 0.85 + nrm(ks[21], (DEPTH, RWKV_WIDTH), 0.05),
        'k_a': 1.0 + nrm(ks[22], (DEPTH, RWKV_WIDTH), 0.05),
        'r_k': nrm(ks[23], (DEPTH, RWKV_HEADS, RWKV_HEAD_DIM), 0.1),
        'lnx_w': 1.0 + nrm(ks[24], (DEPTH, RWKV_WIDTH), 0.02),
        'lnx_b': nrm(ks[25], (DEPTH, RWKV_WIDTH), 0.02),
        'w_out': nrm(ks[26], (DEPTH, D_MIX, D_MODEL), D_MIX ** -0.5),
        'norm_ffn': 1.0 + nrm(ks[27], (DEPTH, D_MODEL), 0.02),
        'w_query': nrm(ks[28], (DEPTH, D_MODEL, PEER_HEADS * D_KEY), D_MODEL ** -0.5),
        'sub_keys': nrm(ks[29], (DEPTH, 2, PEER_HEADS, N_KEYS, D_KEY // 2), (D_KEY // 2) ** -0.5),
        'expert_u': nrm(ks[30], (DEPTH, N_EXPERTS, D_MODEL), D_MODEL ** -0.5),
        'expert_v': nrm(ks[31], (DEPTH, N_EXPERTS, D_MODEL), 0.5),
        'norm_final': 1.0 + nrm(ks[32], (D_MODEL,), 0.02),
    }


def reference(x_prompt, x_sample, state_conv, state_ssd, state_shift, state_wkv, meta_tokens, norm_mix, w_in,
              conv_w, conv_b, dt_bias, a_log, d_skip, ssm_norm, shift_mu, decay_w0, decay_w2, iclr_a0, iclr_a2,
              gate_g2, k_k, k_a, r_k, lnx_w, lnx_b, w_out, norm_ffn, w_query, sub_keys, expert_u, expert_v,
              norm_final):
    bp, seq_p, _ = x_prompt.shape
    seq_s = x_sample.shape[1]
    h_p = jnp.concatenate([jnp.broadcast_to(meta_tokens[None].astype(x_prompt.dtype), (bp, N_META, D_MODEL)), x_prompt], axis=1)
    h_s = x_sample
    zc = jnp.zeros((bp, CONV_W - 1, CONV_DIM), x_prompt.dtype)
    zs = jnp.zeros((bp, SSD_HEADS, SSD_HEAD_DIM, SSD_STATE), F32)
    zsh = jnp.zeros((bp, RWKV_IN), x_prompt.dtype)
    zw = jnp.zeros((bp, RWKV_HEADS, RWKV_HEAD_DIM, RWKV_HEAD_DIM), F32)
    cp, sp, shp, wp, cs, ss, shs, ws = [], [], [], [], [], [], [], []
    for i in range(DEPTH):
        mp = (norm_mix[i], w_in[i], conv_w[i], conv_b[i], dt_bias[i], a_log[i], d_skip[i], ssm_norm[i], shift_mu[i],
              decay_w0[i], decay_w2[i], iclr_a0[i], iclr_a2[i], gate_g2[i], k_k[i], k_a[i], r_k[i], lnx_w[i], lnx_b[i], w_out[i])
        h_p, c1, s1, sh1, w1 = mixer_sublayer(h_p, zc, zs, zsh, zw, (N_META, seq_p), mp)
        h_s, c2, s2, sh2, w2 = mixer_sublayer(h_s, state_conv[i], state_ssd[i], state_shift[i], state_wkv[i], (seq_s,), mp)
        cp.append(c1); sp.append(s1); shp.append(sh1); wp.append(w1)
        cs.append(c2); ss.append(s2); shs.append(sh2); ws.append(w2)
        if i == DEPTH - 1:
            h_p = h_p[:, N_META:]
        h_p = h_p + peer(rmsnorm(h_p, norm_ffn[i]), w_query[i], sub_keys[i], expert_u[i], expert_v[i])
        h_s = h_s + peer(rmsnorm(h_s, norm_ffn[i]), w_query[i], sub_keys[i], expert_u[i], expert_v[i])
    y_prompt = rmsnorm(h_p, norm_final)
    y_sample = rmsnorm(h_s, norm_final)
    new_conv_prompt = jnp.stack(cp).astype(state_conv.dtype)
    new_ssd_prompt = jnp.stack(sp).astype(state_ssd.dtype)
    new_shift_prompt = jnp.stack(shp).astype(state_shift.dtype)
    new_wkv_prompt = jnp.stack(wp).astype(state_wkv.dtype)
    new_conv_sample = jnp.stack(cs).astype(state_conv.dtype)
    new_ssd_sample = jnp.stack(ss).astype(state_ssd.dtype)
    new_shift_sample = jnp.stack(shs).astype(state_shift.dtype)
    new_wkv_sample = jnp.stack(ws).astype(state_wkv.dtype)
    return (y_prompt, y_sample, new_conv_prompt, new_ssd_prompt, new_shift_prompt, new_wkv_prompt,
            new_conv_sample, new_ssd_sample, new_shift_sample, new_wkv_sample)
```

```python
import functools

import jax
import jax.numpy as jnp
import numpy as np
from jax import lax
from jax.experimental import pallas as pl
from jax.experimental.pallas import tpu as pltpu

F32 = jnp.float32
BF16 = jnp.bfloat16
I32 = jnp.int32

N_META = 16
SSD_HEAD_DIM = 64
SSD_GROUPS = 2
SSD_STATE = 128
CONV_W = 4
RWKV_HEAD_DIM = 64
DECAY_LORA = 64
AAA_LORA = 64
GATE_LORA = 128
PEER_HEADS = 8
N_KEYS = 128
PEER_TOPK = 16
RMS_EPS = 1e-6
GN_EPS = 64e-5

LANES = 128
SUBLANES = 8
CHUNK = 128
VMEM_LIMIT_BYTES = 56 * 1024 * 1024
PACK = 2


def _full(shape):
    zeros = (0,) * len(shape)
    return pl.BlockSpec(shape, lambda *_: zeros)


def _params(semantics, vmem=VMEM_LIMIT_BYTES):
    return pltpu.CompilerParams(dimension_semantics=semantics, vmem_limit_bytes=vmem)


def _split2(x):
    hi = x.astype(BF16)
    lo = (x - hi.astype(F32)).astype(BF16)
    return hi, lo


def _split3(x):
    hi = x.astype(BF16)
    r = x - hi.astype(F32)
    mid = r.astype(BF16)
    lo = (r - mid.astype(F32)).astype(BF16)
    return hi, mid, lo


def _dot(a, b):
    return jnp.dot(a, b, preferred_element_type=F32)


def _dot_nt(a, b):
    return lax.dot_general(a, b, (((1,), (1,)), ((), ())), preferred_element_type=F32)


def _dot_sel(x, sel):
    hi, mid, lo = _split3(x)
    return _dot(hi, sel) + _dot(mid, sel) + _dot(lo, sel)


def _dot_hp(x, w_hi, w_lo):
    hi, lo = _split2(x)
    return _dot(hi, w_hi) + (_dot(lo, w_hi) + _dot(hi, w_lo))


def _dot_hp_nt(x, w_hi, w_lo):
    hi, lo = _split2(x)
    return _dot_nt(hi, w_hi) + (_dot_nt(lo, w_hi) + _dot_nt(hi, w_lo))


def _silu(x):
    return x * jax.nn.sigmoid(x)


def _softplus(x):
    return jnp.maximum(x, 0.0) + jnp.log1p(jnp.exp(-jnp.abs(x)))


def _rmsnorm(x, g):
    ms = jnp.mean(x * x, axis=-1, keepdims=True)
    return x * lax.rsqrt(ms + RMS_EPS) * g


def _proj_in_kernel(x_ref, lead_ref, g_ref, w_ref, xbc_ref, z_ref, dt_ref, rw_ref, *, widths, has_lead):
    x = x_ref[...]
    if has_lead:
        x = jnp.where(pl.program_id(1) == 0, lead_ref[...], x)
    u = _rmsnorm(x, g_ref[...]).astype(BF16)
    p = _dot(u, w_ref[...])
    off = 0
    for ref, w in zip((xbc_ref, z_ref, dt_ref, rw_ref), widths):
        ref[...] = p[:, off:off + w]
        off += w


def _proj_in(rows, lead, g, w_cat, widths, *, nseq, nblk, has_lead):
    d = rows.shape[1]
    nblk_in = nblk - 1 if has_lead else nblk
    shift = 1 if has_lead else 0
    x_map = lambda s, j: (s * nblk_in + jnp.maximum(j - shift, 0), 0)
    o_map = lambda s, j: (s * nblk + j, 0)
    return pl.pallas_call(
        functools.partial(_proj_in_kernel, widths=widths, has_lead=has_lead),
        grid=(nseq, nblk),
        in_specs=[pl.BlockSpec((CHUNK, d), x_map), _full(lead.shape), _full(g.shape), _full(w_cat.shape)],
        out_specs=[pl.BlockSpec((CHUNK, w), o_map) for w in widths],
        out_shape=[jax.ShapeDtypeStruct((nseq * nblk * CHUNK, w), F32) for w in widths],
        compiler_params=_params(("parallel", "parallel")),
        name="proj_in",
    )(rows, lead, g, w_cat)


def _cumsum_rows(x, tri_ref):
    hi, mid, lo = _split3(x)
    tri = tri_ref[...]
    return _dot(tri, hi) + _dot(tri, mid) + _dot(tri, lo)


def _ssd_kernel(xbc_ref, z_ref, dt_ref, cs_ref, h0_ref, cw_ref, cb_ref, dtb_ref, alog_ref,
                dskip_ref, norm_ref, tri_ref, ehead_ref, ecol_ref,
                y_ref, cs_out_ref, h_out_ref, xfull, dtile, *, lb, npad, heads, width):
    c = pl.program_id(1)
    nc = pl.num_programs(1)
    L = CHUNK
    prev = CONV_W - 1
    base = SUBLANES
    state = SSD_STATE
    hd = SSD_HEAD_DIM
    per_group = heads // SSD_GROUPS

    @pl.when(c == 0)
    def _():
        h_out_ref[...] = h0_ref[...]
        if lb == L:
            xfull[base - prev:base, :] = cs_ref[0]

    if lb == L:
        xfull[base:base + L, :] = xbc_ref[...]
        dt_raw = dt_ref[...]
    else:
        xfull[...] = jnp.zeros_like(xfull)
        xfull[base + L - lb - prev:base + L - lb, :] = cs_ref[0]
        xfull[base + L - lb:base + L, :] = xbc_ref[...]
        dtile[...] = jnp.zeros_like(dtile)
        dtile[L - lb:L, :] = dt_ref[...]
        dt_raw = dtile[...]

    conv = cb_ref[...] + xfull[base - prev:base - prev + L, :] * cw_ref[0:1, :]
    for k in range(1, CONV_W):
        conv = conv + xfull[base - prev + k:base - prev + k + L, :] * cw_ref[k:k + 1, :]
    cs_new = xfull[base + L - prev:base + L, :]

    @pl.when(c == nc - 1)
    def _():
        cs_out_ref[0] = cs_new

    xfull[base - prev:base, :] = cs_new

    xbc = _silu(conv)
    xs = xbc[:, :width]
    gn = SSD_GROUPS * state
    row = lax.broadcasted_iota(I32, (L, LANES), 0) + c * L
    dt = jnp.where(row >= npad, _softplus(dt_raw + dtb_ref[...]), 0.0)
    da = dt * (-jnp.exp(alog_ref[...]))
    acum = _cumsum_rows(da, tri_ref)
    acum_t = acum.T
    ehead = ehead_ref[...]
    acum_x = _dot_sel(acum, ehead)
    dt_x = _dot_sel(dt, ehead)
    last_x = acum_x[L - 1:L, :]
    xdt = xs * dt_x
    wend = xdt * jnp.exp(last_x - acum_x)
    colb = _dot_sel(acum, ecol_ref[...])
    causal = (lax.broadcasted_iota(I32, (L, L), 0) >= lax.broadcasted_iota(I32, (L, L), 1))
    lane = lax.broadcasted_iota(I32, (L, LANES), 1)

    y_diag = []
    y_off = []
    for g in range(SSD_GROUPS):
        bm = xbc[:, width + g * state:width + (g + 1) * state].astype(BF16)
        cm = xbc[:, width + gn + g * state:width + gn + (g + 1) * state].astype(BF16)
        cb = _dot_nt(cm, bm)
        h_prev = h_out_ref[0, g * per_group:(g + 1) * per_group].reshape(per_group * hd, state)
        y_off.append(_dot_nt(cm, h_prev.astype(BF16)))
        for pair in range(per_group // 2):
            h_a = g * per_group + 2 * pair
            xp = xdt[:, h_a * hd:(h_a + 2) * hd].astype(BF16)
            outs = []
            for h in (h_a, h_a + 1):
                seg = colb[:, h * L:(h + 1) * L] - acum_t[h:h + 1, :]
                dec = jnp.exp(jnp.where(causal, seg, -jnp.inf))
                outs.append(_dot((cb * dec).astype(BF16), xp))
            y_diag.append(jnp.where(lane < hd, outs[0], outs[1]))
        wg_t = wend[:, g * per_group * hd:(g + 1) * per_group * hd].T.astype(BF16)
        upd = _dot(wg_t, bm)
        for i in range(per_group):
            h = g * per_group + i
            cd = jnp.exp(acum_t[h:h + 1, L - 1:L])
            h_out_ref[0, h] = h_out_ref[0, h] * cd + upd[i * hd:(i + 1) * hd, :]
    y = (jnp.concatenate(y_diag, axis=1) + jnp.concatenate(y_off, axis=1) * jnp.exp(acum_x)
         + xs * dskip_ref[...])
    y = y[L - lb:, :] * _silu(z_ref[...])
    y_ref[...] = _rmsnorm(y, norm_ref[...])


def _ssd(xbc, z, dtp, cs, h0, consts, *, nseq, lb, nchunk, npad):
    heads = h0.shape[1]
    width = z.shape[1]
    conv_dim = xbc.shape[1]
    row_map = lambda s, c: (s * nchunk + c, 0)
    kernel = functools.partial(_ssd_kernel, lb=lb, npad=npad, heads=heads, width=width)
    return pl.pallas_call(
        kernel,
        grid=(nseq, nchunk),
        in_specs=[pl.BlockSpec((lb, conv_dim), row_map),
                  pl.BlockSpec((lb, width), row_map),
                  pl.BlockSpec((lb, LANES), row_map),
                  pl.BlockSpec((1,) + cs.shape[1:], lambda s, c: (s, 0, 0)),
                  pl.BlockSpec((1,) + h0.shape[1:], lambda s, c: (s, 0, 0, 0))]
                 + [_full(a.shape) for a in consts],
        out_specs=[pl.BlockSpec((lb, width), row_map),
                   pl.BlockSpec((1,) + cs.shape[1:], lambda s, c: (s, 0, 0)),
                   pl.BlockSpec((1,) + h0.shape[1:], lambda s, c: (s, 0, 0, 0))],
        out_shape=[jax.ShapeDtypeStruct((nseq * nchunk * lb, width), F32),
                   jax.ShapeDtypeStruct(cs.shape, F32),
                   jax.ShapeDtypeStruct(h0.shape, F32)],
        scratch_shapes=[pltpu.VMEM((SUBLANES + CHUNK, conv_dim), F32), pltpu.VMEM((CHUNK, LANES), F32)],
        compiler_params=_params(("parallel", "arbitrary")),
        name="ssd",
    )(xbc, z, dtp, cs, h0, *consts)


def _rwkv_pre_kernel(rw_ref, sh_ref, mu_ref, w0_ref, a0_ref, wa_hi_ref, wa_lo_ref, g2_hi_ref, g2_lo_ref,
                     kk_ref, ka_ref, rk_ref, ones_ref,
                     r_o, d_o, k_o, kn_o, b_o, v_o, g_o, bonus_o, sh_o, pfull, *, lb, npad, width):
    c = pl.program_id(1)
    nc = pl.num_programs(1)
    base = SUBLANES
    p = rw_ref[...]

    @pl.when(c == 0)
    def _():
        pfull[base - 1:base, :] = sh_ref[0]

    pfull[base:base + lb, :] = p
    prev = pfull[base - 1:base - 1 + lb, :]
    last = p[lb - 1:lb, :]
    pfull[base - 1:base, :] = last

    @pl.when(c == nc - 1)
    def _():
        sh_o[0] = last

    pm = p + (prev - p) * mu_ref[...]
    r = pm[:, :width]
    k = pm[:, width:2 * width]
    v = pm[:, 2 * width:3 * width]
    lora_in = pm[:, 3 * width:3 * width + DECAY_LORA + AAA_LORA]
    lane = lax.broadcasted_iota(I32, lora_in.shape, 1)
    lora_in = jnp.where(lane < DECAY_LORA, jnp.tanh(lora_in), lora_in)
    lora = _dot_hp(lora_in, wa_hi_ref[...], wa_lo_ref[...])
    gate_in = jax.nn.sigmoid(pm[:, 3 * width + DECAY_LORA + AAA_LORA:])
    g_o[...] = _dot_hp(gate_in, g2_hi_ref[...], g2_lo_ref[...])
    w = -_softplus(-(w0_ref[...] + lora[:, :width])) - 0.5
    row = lax.broadcasted_iota(I32, w.shape, 0) + c * lb
    d_o[...] = jnp.where(row >= npad, jnp.exp(-jnp.exp(w)), 1.0)
    a = jax.nn.sigmoid(a0_ref[...] + lora[:, width:])
    ones = ones_ref[...]
    kn = k * kk_ref[...]
    kn = kn / jnp.maximum(jnp.sqrt(_dot_sel(kn * kn, ones)), 1e-12)
    kp = k * (1.0 + (a - 1.0) * ka_ref[...])
    r_o[...] = r
    k_o[...] = kp
    kn_o[...] = kn
    b_o[...] = kn * a
    v_o[...] = v
    bonus_o[...] = _dot_sel(r * kp * rk_ref[...], ones) * v


def _rwkv_pre(rw, sh, consts, *, nseq, lb, nchunk, npad, width):
    rw_in = rw.shape[1]
    row_map = lambda s, c: (s * nchunk + c, 0)
    n = nseq * nchunk * lb
    kernel = functools.partial(_rwkv_pre_kernel, lb=lb, npad=npad, width=width)
    return pl.pallas_call(
        kernel,
        grid=(nseq, nchunk),
        in_specs=[pl.BlockSpec((lb, rw_in), row_map), pl.BlockSpec((1, 1, rw_in), lambda s, c: (s, 0, 0))]
                 + [_full(a.shape) for a in consts],
        out_specs=[pl.BlockSpec((lb, width), row_map)] * 8 + [pl.BlockSpec((1, 1, rw_in), lambda s, c: (s, 0, 0))],
        out_shape=[jax.ShapeDtypeStruct((n, width), F32)] * 8 + [jax.ShapeDtypeStruct(sh.shape, F32)],
        scratch_shapes=[pltpu.VMEM((SUBLANES + lb, rw_in), F32)],
        compiler_params=_params(("parallel", "arbitrary")),
        name="rwkv_pre",
    )(rw, sh, *consts)


def _rwkv_scan_kernel(kn_ref, d_ref, b_ref, k_ref, r_ref, v_ref, s0_ref, y_ref, s_ref, *, tb, hd):
    @pl.when(pl.program_id(1) == 0)
    def _():
        s_ref[...] = s0_ref[...]

    def step(t, carry):
        v_t = v_ref[0, t]
        sa = jnp.zeros_like(v_t)
        for k in range(hd):
            sa = sa + s_ref[0, k] * kn_ref[0, t, pl.ds(k, 1), :]
        y = jnp.zeros_like(v_t)
        for k in range(hd):
            sk = (s_ref[0, k] * d_ref[0, t, pl.ds(k, 1), :] - sa * b_ref[0, t, pl.ds(k, 1), :]
                  + v_t * k_ref[0, t, pl.ds(k, 1), :])
            s_ref[0, k] = sk
            y = y + sk * r_ref[0, t, pl.ds(k, 1), :]
        y_ref[0, t] = y
        return carry

    lax.fori_loop(0, tb, step, 0)


def _rwkv_scan(kn, d, b, k, r, v, s0, *, tb):
    ngroup, t_total, hd, _ = kn.shape
    vr = v.shape[2]
    kspec = pl.BlockSpec((1, tb, hd, LANES), lambda g, i: (g, i, 0, 0))
    vspec = pl.BlockSpec((1, tb, vr, LANES), lambda g, i: (g, i, 0, 0))
    sspec = pl.BlockSpec((1, hd, vr, LANES), lambda g, i: (g, 0, 0, 0))
    return pl.pallas_call(
        functools.partial(_rwkv_scan_kernel, tb=tb, hd=hd),
        grid=(ngroup, t_total // tb),
        in_specs=[kspec] * 5 + [vspec, sspec],
        out_specs=[vspec, sspec],
        out_shape=[jax.ShapeDtypeStruct(v.shape, F32), jax.ShapeDtypeStruct(s0.shape, F32)],
        compiler_params=_params(("parallel", "arbitrary")),
        name="rwkv_scan",
    )(kn, d, b, k, r, v, s0)


def _mix_out_kernel(ys_ref, bonus_ref, g_ref, ya_ref, h_ref, lnw_ref, lnb_ref, ones_ref, wa_ref, wb_ref, o_ref):
    ones = ones_ref[...]
    inv = 1.0 / RWKV_HEAD_DIM
    y = ys_ref[...]
    yc = y - _dot_sel(y, ones) * inv
    var = _dot_sel(yc * yc, ones) * inv
    yb = (yc * lax.rsqrt(var + GN_EPS) * lnw_ref[...] + lnb_ref[...] + bonus_ref[...]) * g_ref[...]
    o_ref[...] = (h_ref[...] + _dot(ya_ref[...].astype(BF16), wa_ref[...])
                  + _dot(yb.astype(BF16), wb_ref[...]))


def _mix_out(ys, bonus, g, ya, h, consts, *, nseq, nblk, skip):
    width = ys.shape[1]
    d = h.shape[1]
    cmap = lambda i: (i, 0)
    pmap = lambda i: ((i // nblk) * (nblk + skip) + skip + i % nblk, 0)
    wide = pl.BlockSpec((CHUNK, width), pmap)
    return pl.pallas_call(
        _mix_out_kernel,
        grid=(nseq * nblk,),
        in_specs=[pl.BlockSpec((CHUNK, width), cmap), wide, wide, wide, pl.BlockSpec((CHUNK, d), cmap)]
                 + [_full(a.shape) for a in consts],
        out_specs=pl.BlockSpec((CHUNK, d), cmap),
        out_shape=jax.ShapeDtypeStruct(h.shape, F32),
        compiler_params=_params(("parallel",)),
        name="mix_out",
    )(ys, bonus, g, ya, h, *consts)


def _top_lanes(s, ncol, out_lane0):
    tm = s.shape[0]
    lane_s = lax.broadcasted_iota(I32, s.shape, 1)
    lane_o = lax.broadcasted_iota(I32, (tm, LANES), 1)

    def body(i, carry):
        s, vals, vinf, pos = carry
        m = jnp.max(s, axis=-1, keepdims=True)
        p = jnp.min(jnp.where(s == m, lane_s, ncol), axis=-1, keepdims=True)
        hit = lane_o == out_lane0 + i
        vals = jnp.where(hit, m, vals)
        vinf = jnp.where(hit, m, vinf)
        pos = jnp.where(hit, p.astype(F32), pos)
        s = jnp.where(lane_s == p, -jnp.inf, s)
        return s, vals, vinf, pos

    zeros = jnp.zeros((tm, LANES), F32)
    _, vals, vinf, pos = lax.fori_loop(0, PEER_TOPK, body, (s, zeros, jnp.full((tm, LANES), -jnp.inf, F32), zeros))
    return vals, vinf, pos


def _peer_route_kernel(h_ref, g_ref, wq_hi_ref, wq_lo_ref, k1_hi_ref, k1_lo_ref, k2_hi_ref, k2_lo_ref,
                       e1_ref, e2_ref, u_o, idx_o, gate_o):
    u = _rmsnorm(h_ref[...], g_ref[...])
    u_o[...] = u
    q = _dot_hp(u, wq_hi_ref[...], wq_lo_ref[...])
    tm = u.shape[0]
    half = N_KEYS
    ncand = PEER_TOPK * PEER_TOPK
    e1 = e1_ref[...]
    e2 = e2_ref[...]
    lane_c = lax.broadcasted_iota(I32, (tm, ncand), 1)
    lane_o = lax.broadcasted_iota(I32, (tm, LANES), 1)
    idx_all = jnp.zeros((tm, LANES), F32)
    gate_all = jnp.zeros((tm, LANES), F32)
    for h in range(PEER_HEADS):
        q1 = q[:, 2 * h * half:(2 * h + 1) * half]
        q2 = q[:, (2 * h + 1) * half:(2 * h + 2) * half]
        s1 = _dot_hp_nt(q1, k1_hi_ref[h], k1_lo_ref[h])
        s2 = _dot_hp_nt(q2, k2_hi_ref[h], k2_lo_ref[h])
        v1, _, i1 = _top_lanes(s1, N_KEYS, 0)
        v2, _, i2 = _top_lanes(s2, N_KEYS, 0)
        cand = _dot_sel(v1, e1) + _dot_sel(v2, e2)
        expert = _dot(i1.astype(BF16), e1) * float(N_KEYS) + _dot(i2.astype(BF16), e2)

        def body(i, carry, expert=expert, h=h):
            cand, ts, ex = carry
            m = jnp.max(cand, axis=-1, keepdims=True)
            p = jnp.min(jnp.where(cand == m, lane_c, ncand), axis=-1, keepdims=True)
            sel = lane_c == p
            e = jnp.sum(jnp.where(sel, expert, 0.0), axis=-1, keepdims=True)
            hit = lane_o == h * PEER_TOPK + i
            ts = jnp.where(hit, m, ts)
            ex = jnp.where(hit, e, ex)
            cand = jnp.where(sel, -jnp.inf, cand)
            return cand, ts, ex

        _, ts, idx_all = lax.fori_loop(
            0, PEER_TOPK, body, (cand, jnp.full((tm, LANES), -jnp.inf, F32), idx_all))
        ex = jnp.exp(ts - jnp.max(ts, axis=-1, keepdims=True))
        gate_all = gate_all + ex / jnp.sum(ex, axis=-1, keepdims=True)
    idx_o[...] = idx_all.astype(I32)
    gate_o[...] = gate_all


def _peer_route(h, consts):
    n, d = h.shape
    rows = pl.BlockSpec((CHUNK, d), lambda i: (i, 0))
    sel = pl.BlockSpec((CHUNK, LANES), lambda i: (i, 0))
    return pl.pallas_call(
        _peer_route_kernel,
        grid=(n // CHUNK,),
        in_specs=[rows] + [_full(a.shape) for a in consts],
        out_specs=[rows, sel, sel],
        out_shape=[jax.ShapeDtypeStruct((n, d), F32), jax.ShapeDtypeStruct((n, LANES), I32),
                   jax.ShapeDtypeStruct((n, LANES), F32)],
        compiler_params=_params(("parallel",)),
        name="peer_route",
    )(h, *consts)


def _gather_rows(idx_ref, tab_ref, gbuf, i, nsel, rows_per):
    for j in range(nsel):
        e = idx_ref[i, j]
        gbuf[j * rows_per:(j + 1) * rows_per, :] = tab_ref[pl.ds(pl.multiple_of(e * rows_per, rows_per), rows_per), :]
    return pltpu.bitcast(gbuf[...], BF16)


def _diag_mask(nsel):
    sub = lax.broadcasted_iota(I32, (SUBLANES, nsel * SUBLANES), 0)
    lane = lax.broadcasted_iota(I32, (SUBLANES, nsel * SUBLANES), 1)
    return sub == lane % SUBLANES


def _gelu(x):
    return 0.5 * x * (1.0 + lax.erf(x * np.float32(1.0 / np.sqrt(2.0))))


def _peer_u_kernel(idx_ref, tok_ref, gate_ref, tab_ref, esum_ref, act_o, gbuf, rsum, *, tm, nsel, rows_per):
    diag = _diag_mask(nsel)

    def body(i, carry):
        rows = _gather_rows(idx_ref, tab_ref, gbuf, i, nsel, rows_per)
        tok = tok_ref[i].astype(BF16)
        prod = _dot_nt(tok, rows)
        rsum[pl.ds(i, 1), :] = jnp.sum(jnp.where(diag, prod, 0.0), axis=0, keepdims=True)
        return carry

    lax.fori_loop(0, tm, body, 0)
    pre = _dot_sel(rsum[...], esum_ref[...])
    act_o[...] = _gelu(pre) * gate_ref[...]


def _peer_v_kernel(idx_ref, act_ref, tab_ref, eexp_ref, out_o, gbuf, arep, *, tm, nsel, rows_per):
    diag = _diag_mask(nsel)
    arep[...] = _dot(act_ref[...].astype(BF16), eexp_ref[...])

    def body(i, carry):
        rows = _gather_rows(idx_ref, tab_ref, gbuf, i, nsel, rows_per)
        a = jnp.broadcast_to(arep[pl.ds(i, 1), :], diag.shape)
        out_o[i] = _dot(jnp.where(diag, a, 0.0).astype(BF16), rows)
        return carry

    lax.fori_loop(0, tm, body, 0)


def _peer_gather_specs(tm, nsel, tab):
    idx = pl.BlockSpec((tm, nsel), lambda i: (i, 0), memory_space=pltpu.SMEM)
    table = pl.BlockSpec(tab.shape, lambda i: (0, 0), pipeline_mode=pl.Buffered(1))
    return idx, table


def _peer_u(idx, tok3, gate, tab, esum, *, tm):
    n, nsel = idx.shape
    rows_per = SUBLANES // PACK
    idx_spec, tab_spec = _peer_gather_specs(tm, nsel, tab)
    return pl.pallas_call(
        functools.partial(_peer_u_kernel, tm=tm, nsel=nsel, rows_per=rows_per),
        grid=(n // tm,),
        in_specs=[idx_spec, pl.BlockSpec((tm, SUBLANES, LANES), lambda i: (i, 0, 0)),
                  pl.BlockSpec((tm, nsel), lambda i: (i, 0)), tab_spec, _full(esum.shape)],
        out_specs=pl.BlockSpec((tm, nsel), lambda i: (i, 0)),
        out_shape=jax.ShapeDtypeStruct((n, nsel), F32),
        scratch_shapes=[pltpu.VMEM((nsel * rows_per, LANES), I32), pltpu.VMEM((tm, nsel * SUBLANES), F32)],
        compiler_params=_params(("arbitrary",)),
        name="peer_u",
    )(idx, tok3, gate, tab, esum)


def _peer_v(idx, act, tab, eexp, *, tm):
    n, nsel = idx.shape
    rows_per = SUBLANES // PACK
    idx_spec, tab_spec = _peer_gather_specs(tm, nsel, tab)
    return pl.pallas_call(
        functools.partial(_peer_v_kernel, tm=tm, nsel=nsel, rows_per=rows_per),
        grid=(n // tm,),
        in_specs=[idx_spec, pl.BlockSpec((tm, nsel), lambda i: (i, 0)), tab_spec, _full(eexp.shape)],
        out_specs=pl.BlockSpec((tm, SUBLANES, LANES), lambda i: (i, 0, 0)),
        out_shape=jax.ShapeDtypeStruct((n, SUBLANES, LANES), F32),
        scratch_shapes=[pltpu.VMEM((nsel * rows_per, LANES), I32), pltpu.VMEM((tm, nsel * SUBLANES), F32)],
        compiler_params=_params(("arbitrary",)),
        name="peer_v",
    )(idx, act, tab, eexp)


def _final_norm_kernel(h_ref, f_ref, g_ref, o_ref):
    o_ref[...] = _rmsnorm(h_ref[...] + f_ref[...], g_ref[...])


def _final_norm(h, f, g):
    n, d = h.shape
    rows = pl.BlockSpec((CHUNK, d), lambda i: (i, 0))
    return pl.pallas_call(
        _final_norm_kernel,
        grid=(n // CHUNK,),
        in_specs=[rows, rows, _full(g.shape)],
        out_specs=rows,
        out_shape=jax.ShapeDtypeStruct((n, d), F32),
        compiler_params=_params(("parallel",)),
        name="final_norm",
    )(h, f, g)


def _hi_lo(w):
    hi = w.astype(BF16)
    return hi, (w - hi.astype(F32)).astype(BF16)


def _block_ones(n, blk):
    i = np.arange(n)
    return jnp.asarray(i[:, None] // blk == i[None, :] // blk, dtype=BF16)


def _pack_table(t):
    n, d = t.shape
    b = lax.bitcast_convert_type(t.astype(BF16), jnp.uint16).astype(jnp.uint32)
    b = b.reshape(n, d // (PACK * LANES), PACK, LANES)
    word = b[:, :, 0, :] | (b[:, :, 1, :] << 16)
    return lax.bitcast_convert_type(word, I32).reshape(n * d // (PACK * LANES), LANES)


def _stream(x_rows, lead, conv0, ssd0, shift0, wkv0, w, *, nseq, nblk, lb, has_lead, npad, tb):
    d = x_rows.shape[1]
    nblk_all = nblk + (1 if has_lead else 0)
    width_a = w["ssm_norm"].shape[1]
    width_b = w["ones_b"].shape[0]
    heads_b = width_b // RWKV_HEAD_DIM
    if lb == CHUNK:
        xbc, z, dtp, rw = _proj_in(x_rows, lead, w["norm_mix"], w["w_in"], w["widths"],
                                   nseq=nseq, nblk=nblk_all, has_lead=has_lead)
        nchunk = nblk_all
    else:
        xbc, z, dtp, rw = _proj_in(x_rows, lead, w["norm_mix"], w["w_in"], w["widths"],
                                   nseq=1, nblk=x_rows.shape[0] // CHUNK, has_lead=False)
        nchunk = 1
    ya, conv_new, ssd_new = _ssd(xbc, z, dtp, conv0, ssd0, w["ssd"], nseq=nseq, lb=lb, nchunk=nchunk, npad=npad)
    r, dcy, kp, kn, b, v, g, bonus, shift_new = _rwkv_pre(
        rw, shift0[:, None, :], w["rwkv_pre"], nseq=nseq, lb=lb, nchunk=nchunk,
        npad=npad if lb == CHUNK else 0, width=width_b)

    t_all = nchunk * lb
    t0 = npad if lb == CHUNK else 0
    t_real = t_all - t0
    pairs = nseq * heads_b
    hd = RWKV_HEAD_DIM
    if pairs >= LANES:
        ngroup, dup = pairs // LANES, 1
    else:
        ngroup, dup = 1, LANES // pairs
    seq_per = nseq // ngroup
    vr = hd // dup

    def to_scan_k(a):
        a = a.reshape(ngroup, seq_per, t_all, heads_b, hd)[:, :, t0:]
        a = a.transpose(0, 2, 4, 1, 3).reshape(ngroup, t_real, hd, seq_per * heads_b)
        return jnp.tile(a, (1, 1, 1, dup))

    def to_scan_v(a):
        a = a.reshape(ngroup, seq_per, t_all, heads_b, dup, vr)[:, :, t0:]
        return a.transpose(0, 2, 5, 4, 1, 3).reshape(ngroup, t_real, vr, LANES)

    s0 = wkv0.reshape(ngroup, seq_per, heads_b, dup, vr, hd).transpose(0, 5, 4, 3, 1, 2)
    s0 = s0.reshape(ngroup, hd, vr, LANES)
    ysc, s_new = _rwkv_scan(to_scan_k(kn), to_scan_k(dcy), to_scan_k(b), to_scan_k(kp), to_scan_k(r),
                            to_scan_v(v), s0, tb=tb)
    wkv_new = s_new.reshape(ngroup, hd, vr, dup, seq_per, heads_b).transpose(0, 4, 5, 3, 2, 1)
    wkv_new = wkv_new.reshape(nseq, heads_b, hd, hd)
    skip_t = t_real - nblk * lb if lb == CHUNK else 0
    ys = ysc[:, skip_t:].reshape(ngroup, t_real - skip_t, vr, dup, seq_per, heads_b)
    ys = ys.transpose(0, 4, 1, 5, 3, 2).reshape(nseq * (t_real - skip_t), width_b)

    if lb == CHUNK:
        h1 = _mix_out(ys, bonus, g, ya, x_rows, w["mix_out"], nseq=nseq, nblk=nblk, skip=nchunk - nblk)
    else:
        h1 = _mix_out(ys, bonus, g, ya, x_rows, w["mix_out"], nseq=1, nblk=x_rows.shape[0] // CHUNK, skip=0)

    u, idx, gate = _peer_route(h1, w["route"])
    n = u.shape[0]
    act = _peer_u(idx, u.reshape(n, SUBLANES, LANES), gate, w["tab_u"], w["esum"], tm=w["peer_tm"])
    ffn = _peer_v(idx, act, w["tab_v"], w["eexp"], tm=w["peer_tm"])
    y = _final_norm(h1, ffn.reshape(n, d), w["norm_final"])
    return y, conv_new, ssd_new, shift_new[:, 0, :], wkv_new


def kernel(x_prompt, x_sample, state_conv, state_ssd, state_shift, state_wkv, meta_tokens, norm_mix, w_in, conv_w, conv_b, dt_bias, a_log, d_skip, ssm_norm, shift_mu, decay_w0, decay_w2, iclr_a0, iclr_a2, gate_g2, k_k, k_a, r_k, lnx_w, lnx_b, w_out, norm_ffn, w_query, sub_keys, expert_u, expert_v, norm_final):
    bp, seq_p, d = x_prompt.shape
    bs, seq_s, _ = x_sample.shape
    depth = w_in.shape[0]
    assert depth == 1 and seq_p % CHUNK == 0 and (bs * seq_s) % CHUNK == 0 and seq_s % SUBLANES == 0
    heads_a = state_ssd.shape[2]
    width_a = heads_a * SSD_HEAD_DIM
    conv_dim = state_conv.shape[3]
    rw_in = state_shift.shape[2]
    heads_b = state_wkv.shape[2]
    width_b = heads_b * RWKV_HEAD_DIM
    assert heads_a <= LANES and rw_in == 3 * width_b + DECAY_LORA + AAA_LORA + GATE_LORA

    wi = w_in[0]
    ssd_in = width_a + conv_dim + heads_a
    w_cat = jnp.concatenate([
        wi[:, width_a:width_a + conv_dim], wi[:, :width_a],
        jnp.pad(wi[:, width_a + conv_dim:ssd_in], ((0, 0), (0, LANES - heads_a))),
        wi[:, ssd_in:]], axis=1).astype(BF16)
    widths = (conv_dim, width_a, LANES, rw_in)
    pad_h = lambda a: jnp.pad(a.reshape(1, -1), ((0, 0), (0, LANES - heads_a)))
    i_l = np.arange(CHUNK)
    tri = jnp.asarray(i_l[:, None] >= i_l[None, :], dtype=BF16)
    ehead = jnp.asarray(np.arange(LANES)[:, None] == np.arange(width_a)[None, :] // SSD_HEAD_DIM, dtype=BF16)
    ecol = jnp.asarray(np.arange(LANES)[:, None] == np.arange(heads_a * CHUNK)[None, :] // CHUNK, dtype=BF16)
    ssd_consts = (conv_w[0], conv_b[0].reshape(1, -1), pad_h(dt_bias[0]), pad_h(a_log[0]),
                  jnp.repeat(d_skip[0], SSD_HEAD_DIM).reshape(1, -1), ssm_norm[0].reshape(1, -1), tri, ehead, ecol)
    ones_b = _block_ones(width_b, RWKV_HEAD_DIM)
    zero = jnp.zeros((DECAY_LORA, width_b), F32)
    w_wa = jnp.concatenate([jnp.concatenate([decay_w2[0], zero], axis=1),
                            jnp.concatenate([jnp.zeros((AAA_LORA, width_b), F32), iclr_a2[0]], axis=1)], axis=0)
    row = lambda a: a.reshape(1, -1)
    pre_consts = (row(shift_mu[0]), row(decay_w0[0]), row(iclr_a0[0]), *_hi_lo(w_wa), *_hi_lo(gate_g2[0]),
                  row(k_k[0]), row(k_a[0]), row(r_k[0]), ones_b)
    wo = w_out[0].astype(BF16)
    mix_consts = (row(lnx_w[0]), row(lnx_b[0]), ones_b, wo[:width_a], wo[width_a:])
    c = np.arange(PEER_TOPK * PEER_TOPK)
    e1 = jnp.asarray(np.arange(LANES)[:, None] == c[None, :] // PEER_TOPK, dtype=BF16)
    e2 = jnp.asarray(np.arange(LANES)[:, None] == c[None, :] % PEER_TOPK, dtype=BF16)
    route_consts = (row(norm_ffn[0]), *_hi_lo(w_query[0]), *_hi_lo(sub_keys[0, 0]), *_hi_lo(sub_keys[0, 1]), e1, e2)
    nsel = PEER_HEADS * PEER_TOPK
    lane8 = np.arange(nsel * SUBLANES)
    eexp = jnp.asarray(np.arange(nsel)[:, None] == lane8[None, :] // SUBLANES, dtype=BF16)
    w = dict(norm_mix=row(norm_mix[0]), w_in=w_cat, widths=widths, ssm_norm=row(ssm_norm[0]), ones_b=ones_b,
             ssd=ssd_consts, rwkv_pre=pre_consts, mix_out=mix_consts, route=route_consts,
             tab_u=_pack_table(expert_u[0]), tab_v=_pack_table(expert_v[0]), esum=eexp.T, eexp=eexp,
             norm_final=row(norm_final), peer_tm=64)

    npad = CHUNK - N_META
    lead = jnp.concatenate([jnp.zeros((npad, d), F32), meta_tokens.astype(F32)], axis=0)
    zeros = lambda *s: jnp.zeros(s, F32)
    yp, cp, sp, shp, wp = _stream(
        x_prompt.reshape(bp * seq_p, d), lead,
        zeros(bp, CONV_W - 1, conv_dim), zeros(bp, heads_a, SSD_HEAD_DIM, SSD_STATE), zeros(bp, rw_in),
        zeros(bp, heads_b, RWKV_HEAD_DIM, RWKV_HEAD_DIM), w,
        nseq=bp, nblk=seq_p // CHUNK, lb=CHUNK, has_lead=True, npad=npad, tb=N_META)
    ys, cs, ss, shs, ws = _stream(
        x_sample.reshape(bs * seq_s, d), lead, state_conv[0], state_ssd[0], state_shift[0], state_wkv[0], w,
        nseq=bs, nblk=1, lb=seq_s, has_lead=False, npad=CHUNK - seq_s, tb=seq_s)
    return (yp.reshape(bp, seq_p, d), ys.reshape(bs, seq_s, d), cp[None], sp[None], shp[None], wp[None],
            cs[None], ss[None], shs[None], ws[None])
```

```python
import functools

import jax
import jax.numpy as jnp
import numpy as np
from jax import lax
from jax.experimental import pallas as pl
from jax.experimental.pallas import tpu as pltpu

F32 = jnp.float32
BF16 = jnp.bfloat16
I32 = jnp.int32

N_META = 16
SSD_HEAD_DIM = 64
SSD_GROUPS = 2
SSD_STATE = 128
CONV_W = 4
RWKV_HEAD_DIM = 64
DECAY_LORA = 64
AAA_LORA = 64
GATE_LORA = 128
PEER_HEADS = 8
N_KEYS = 128
PEER_TOPK = 16
RMS_EPS = 1e-6
GN_EPS = 64e-5

LANES = 128
SUBLANES = 8
CHUNK = 128
VMEM_LIMIT_BYTES = 56 * 1024 * 1024


def _full(shape):
    zeros = (0,) * len(shape)
    return pl.BlockSpec(shape, lambda *_: zeros)


def _params(semantics, vmem=VMEM_LIMIT_BYTES):
    return pltpu.CompilerParams(dimension_semantics=semantics, vmem_limit_bytes=vmem)


def _split2(x):
    hi = x.astype(BF16)
    lo = (x - hi.astype(F32)).astype(BF16)
    return hi, lo


def _split3(x):
    hi = x.astype(BF16)
    r = x - hi.astype(F32)
    mid = r.astype(BF16)
    lo = (r - mid.astype(F32)).astype(BF16)
    return hi, mid, lo


def _dot(a, b):
    return jnp.dot(a, b, preferred_element_type=F32)


def _dot_nt(a, b):
    return lax.dot_general(a, b, (((1,), (1,)), ((), ())), preferred_element_type=F32)


def _dot_sel(x, sel):
    hi, mid, lo = _split3(x)
    return _dot(hi, sel) + _dot(mid, sel) + _dot(lo, sel)


def _dot_hp(x, w_hi, w_lo):
    hi, lo = _split2(x)
    return _dot(hi, w_hi) + (_dot(lo, w_hi) + _dot(hi, w_lo))


def _dot_hp_nt(x, w_hi, w_lo):
    hi, lo = _split2(x)
    return _dot_nt(hi, w_hi) + (_dot_nt(lo, w_hi) + _dot_nt(hi, w_lo))


def _silu(x):
    return x * jax.nn.sigmoid(x)


def _softplus(x):
    return jnp.maximum(x, 0.0) + jnp.log1p(jnp.exp(-jnp.abs(x)))


def _rmsnorm(x, g):
    ms = jnp.mean(x * x, axis=-1, keepdims=True)
    return x * lax.rsqrt(ms + RMS_EPS) * g


def _proj_in_kernel(x_ref, lead_ref, g_ref, w_ref, xbc_ref, z_ref, dt_ref, rw_ref, *, widths, has_lead):
    x = x_ref[...]
    if has_lead:
        x = jnp.where(pl.program_id(1) == 0, lead_ref[...], x)
    u = _rmsnorm(x, g_ref[...]).astype(BF16)
    p = _dot(u, w_ref[...])
    off = 0
    for ref, w in zip((xbc_ref, z_ref, dt_ref, rw_ref), widths):
        ref[...] = p[:, off:off + w]
        off += w


def _proj_in(rows, lead, g, w_cat, widths, *, nseq, nblk, has_lead):
    d = rows.shape[1]
    nblk_in = nblk - 1 if has_lead else nblk
    shift = 1 if has_lead else 0
    x_map = lambda s, j: (s * nblk_in + jnp.maximum(j - shift, 0), 0)
    o_map = lambda s, j: (s * nblk + j, 0)
    return pl.pallas_call(
        functools.partial(_proj_in_kernel, widths=widths, has_lead=has_lead),
        grid=(nseq, nblk),
        in_specs=[pl.BlockSpec((CHUNK, d), x_map), _full(lead.shape), _full(g.shape), _full(w_cat.shape)],
        out_specs=[pl.BlockSpec((CHUNK, w), o_map) for w in widths],
        out_shape=[jax.ShapeDtypeStruct((nseq * nblk * CHUNK, w), F32) for w in widths],
        compiler_params=_params(("parallel", "parallel")),
        name="proj_in",
    )(rows, lead, g, w_cat)


def _cumsum_rows(x, tri_ref):
    hi, mid, lo = _split3(x)
    tri = tri_ref[...]
    return _dot(tri, hi) + _dot(tri, mid) + _dot(tri, lo)


def _ssd_kernel(xbc_ref, z_ref, dt_ref, cs_ref, h0_ref, cw_ref, cb_ref, dtb_ref, alog_ref,
                dskip_ref, norm_ref, tri_ref, ehead_ref, ecol_ref,
                y_ref, cs_out_ref, h_out_ref, xfull, dtile, *, lb, npad, heads, width):
    c = pl.program_id(1)
    nc = pl.num_programs(1)
    L = CHUNK
    prev = CONV_W - 1
    base = SUBLANES
    state = SSD_STATE
    hd = SSD_HEAD_DIM
    per_group = heads // SSD_GROUPS

    @pl.when(c == 0)
    def _():
        h_out_ref[...] = h0_ref[...]
        if lb == L:
            xfull[base - prev:base, :] = cs_ref[0]

    if lb == L:
        xfull[base:base + L, :] = xbc_ref[...]
        dt_raw = dt_ref[...]
    else:
        xfull[...] = jnp.zeros_like(xfull)
        xfull[base + L - lb - prev:base + L - lb, :] = cs_ref[0]
        xfull[base + L - lb:base + L, :] = xbc_ref[...]
        dtile[...] = jnp.zeros_like(dtile)
        dtile[L - lb:L, :] = dt_ref[...]
        dt_raw = dtile[...]

    conv = cb_ref[...] + xfull[base - prev:base - prev + L, :] * cw_ref[0:1, :]
    for k in range(1, CONV_W):
        conv = conv + xfull[base - prev + k:base - prev + k + L, :] * cw_ref[k:k + 1, :]
    cs_new = xfull[base + L - prev:base + L, :]

    @pl.when(c == nc - 1)
    def _():
        cs_out_ref[0] = cs_new

    xfull[base - prev:base, :] = cs_new

    xbc = _silu(conv)
    xs = xbc[:, :width]
    gn = SSD_GROUPS * state
    row = lax.broadcasted_iota(I32, (L, LANES), 0) + c * L
    dt = jnp.where(row >= npad, _softplus(dt_raw + dtb_ref[...]), 0.0)
    da = dt * (-jnp.exp(alog_ref[...]))
    acum = _cumsum_rows(da, tri_ref)
    acum_t = acum.T
    ehead = ehead_ref[...]
    acum_x = _dot_sel(acum, ehead)
    dt_x = _dot_sel(dt, ehead)
    last_x = acum_x[L - 1:L, :]
    xdt = xs * dt_x
    wend = xdt * jnp.exp(last_x - acum_x)
    colb = _dot_sel(acum, ecol_ref[...])
    causal = (lax.broadcasted_iota(I32, (L, L), 0) >= lax.broadcasted_iota(I32, (L, L), 1))
    lane = lax.broadcasted_iota(I32, (L, LANES), 1)

    y_diag = []
    y_off = []
    for g in range(SSD_GROUPS):
        bm = xbc[:, width + g * state:width + (g + 1) * state].astype(BF16)
        cm = xbc[:, width + gn + g * state:width + gn + (g + 1) * state].astype(BF16)
        cb = _dot_nt(cm, bm)
        h_prev = h_out_ref[0, g * per_group:(g + 1) * per_group].reshape(per_group * hd, state)
        y_off.append(_dot_nt(cm, h_prev.astype(BF16)))
        for pair in range(per_group // 2):
            h_a = g * per_group + 2 * pair
            xp = xdt[:, h_a * hd:(h_a + 2) * hd].astype(BF16)
            outs = []
            for h in (h_a, h_a + 1):
                seg = colb[:, h * L:(h + 1) * L] - acum_t[h:h + 1, :]
                dec = jnp.exp(jnp.where(causal, seg, -jnp.inf))
                outs.append(_dot((cb * dec).astype(BF16), xp))
            y_diag.append(jnp.where(lane < hd, outs[0], outs[1]))
        wg_t = wend[:, g * per_group * hd:(g + 1) * per_group * hd].T.astype(BF16)
        upd = _dot(wg_t, bm)
        for i in range(per_group):
            h = g * per_group + i
            cd = jnp.exp(acum_t[h:h + 1, L - 1:L])
            h_out_ref[0, h] = h_out_ref[0, h] * cd + upd[i * hd:(i + 1) * hd, :]
    y = (jnp.concatenate(y_diag, axis=1) + jnp.concatenate(y_off, axis=1) * jnp.exp(acum_x)
         + xs * dskip_ref[...])
    y = y[L - lb:, :] * _silu(z_ref[...])
    y_ref[...] = _rmsnorm(y, norm_ref[...])


def _ssd(xbc, z, dtp, cs, h0, consts, *, nseq, lb, nchunk, npad):
    heads = h0.shape[1]
    width = z.shape[1]
    conv_dim = xbc.shape[1]
    row_map = lambda s, c: (s * nchunk + c, 0)
    kernel = functools.partial(_ssd_kernel, lb=lb, npad=npad, heads=heads, width=width)
    return pl.pallas_call(
        kernel,
        grid=(nseq, nchunk),
        in_specs=[pl.BlockSpec((lb, conv_dim), row_map),
                  pl.BlockSpec((lb, width), row_map),
                  pl.BlockSpec((lb, LANES), row_map),
                  pl.BlockSpec((1,) + cs.shape[1:], lambda s, c: (s, 0, 0)),
                  pl.BlockSpec((1,) + h0.shape[1:], lambda s, c: (s, 0, 0, 0))]
                 + [_full(a.shape) for a in consts],
        out_specs=[pl.BlockSpec((lb, width), row_map),
                   pl.BlockSpec((1,) + cs.shape[1:], lambda s, c: (s, 0, 0)),
                   pl.BlockSpec((1,) + h0.shape[1:], lambda s, c: (s, 0, 0, 0))],
        out_shape=[jax.ShapeDtypeStruct((nseq * nchunk * lb, width), F32),
                   jax.ShapeDtypeStruct(cs.shape, F32),
                   jax.ShapeDtypeStruct(h0.shape, F32)],
        scratch_shapes=[pltpu.VMEM((SUBLANES + CHUNK, conv_dim), F32), pltpu.VMEM((CHUNK, LANES), F32)],
        compiler_params=_params(("parallel", "arbitrary")),
        name="ssd",
    )(xbc, z, dtp, cs, h0, *consts)


def _rwkv_pre_kernel(rw_ref, sh_ref, mu_ref, w0_ref, a0_ref, wa_hi_ref, wa_lo_ref, g2_hi_ref, g2_lo_ref,
                     kk_ref, ka_ref, rk_ref, ones_ref,
                     r_o, d_o, k_o, kn_o, b_o, v_o, g_o, bonus_o, sh_o, pfull, *, lb, npad, width):
    c = pl.program_id(1)
    nc = pl.num_programs(1)
    base = SUBLANES
    p = rw_ref[...]

    @pl.when(c == 0)
    def _():
        pfull[base - 1:base, :] = sh_ref[0]

    pfull[base:base + lb, :] = p
    prev = pfull[base - 1:base - 1 + lb, :]
    last = p[lb - 1:lb, :]
    pfull[base - 1:base, :] = last

    @pl.when(c == nc - 1)
    def _():
        sh_o[0] = last

    pm = p + (prev - p) * mu_ref[...]
    r = pm[:, :width]
    k = pm[:, width:2 * width]
    v = pm[:, 2 * width:3 * width]
    lora_in = pm[:, 3 * width:3 * width + DECAY_LORA + AAA_LORA]
    lane = lax.broadcasted_iota(I32, lora_in.shape, 1)
    lora_in = jnp.where(lane < DECAY_LORA, jnp.tanh(lora_in), lora_in)
    lora = _dot_hp(lora_in, wa_hi_ref[...], wa_lo_ref[...])
    gate_in = jax.nn.sigmoid(pm[:, 3 * width + DECAY_LORA + AAA_LORA:])
    g_o[...] = _dot_hp(gate_in, g2_hi_ref[...], g2_lo_ref[...])
    w = -_softplus(-(w0_ref[...] + lora[:, :width])) - 0.5
    row = lax.broadcasted_iota(I32, w.shape, 0) + c * lb
    d_o[...] = jnp.where(row >= npad, jnp.exp(-jnp.exp(w)), 1.0)
    a = jax.nn.sigmoid(a0_ref[...] + lora[:, width:])
    ones = ones_ref[...]
    kn = k * kk_ref[...]
    kn = kn / jnp.maximum(jnp.sqrt(_dot_sel(kn * kn, ones)), 1e-12)
    kp = k * (1.0 + (a - 1.0) * ka_ref[...])
    r_o[...] = r
    k_o[...] = kp
    kn_o[...] = kn
    b_o[...] = kn * a
    v_o[...] = v
    bonus_o[...] = _dot_sel(r * kp * rk_ref[...], ones) * v


def _rwkv_pre(rw, sh, consts, *, nseq, lb, nchunk, npad, width):
    rw_in = rw.shape[1]
    row_map = lambda s, c: (s * nchunk + c, 0)
    n = nseq * nchunk * lb
    kernel = functools.partial(_rwkv_pre_kernel, lb=lb, npad=npad, width=width)
    return pl.pallas_call(
        kernel,
        grid=(nseq, nchunk),
        in_specs=[pl.BlockSpec((lb, rw_in), row_map), pl.BlockSpec((1, 1, rw_in), lambda s, c: (s, 0, 0))]
                 + [_full(a.shape) for a in consts],
        out_specs=[pl.BlockSpec((lb, width), row_map)] * 8 + [pl.BlockSpec((1, 1, rw_in), lambda s, c: (s, 0, 0))],
        out_shape=[jax.ShapeDtypeStruct((n, width), F32)] * 8 + [jax.ShapeDtypeStruct(sh.shape, F32)],
        scratch_shapes=[pltpu.VMEM((SUBLANES + lb, rw_in), F32)],
        compiler_params=_params(("parallel", "arbitrary")),
        name="rwkv_pre",
    )(rw, sh, *consts)


def _rwkv_scan_kernel(kn_ref, d_ref, b_ref, k_ref, r_ref, v_ref, s0_ref, y_ref, s_ref, *, tb, hd):
    @pl.when(pl.program_id(1) == 0)
    def _():
        s_ref[...] = s0_ref[...]

    def step(t, carry):
        v_t = v_ref[0, t]
        sa = jnp.zeros_like(v_t)
        for k in range(hd):
            sa = sa + s_ref[0, k] * kn_ref[0, t, pl.ds(k, 1), :]
        y = jnp.zeros_like(v_t)
        for k in range(hd):
            sk = (s_ref[0, k] * d_ref[0, t, pl.ds(k, 1), :] - sa * b_ref[0, t, pl.ds(k, 1), :]
                  + v_t * k_ref[0, t, pl.ds(k, 1), :])
            s_ref[0, k] = sk
            y = y + sk * r_ref[0, t, pl.ds(k, 1), :]
        y_ref[0, t] = y
        return carry

    lax.fori_loop(0, tb, step, 0)


def _rwkv_scan(kn, d, b, k, r, v, s0, *, tb):
    ngroup, t_total, hd, _ = kn.shape
    vr = v.shape[2]
    kspec = pl.BlockSpec((1, tb, hd, LANES), lambda g, i: (g, i, 0, 0))
    vspec = pl.BlockSpec((1, tb, vr, LANES), lambda g, i: (g, i, 0, 0))
    sspec = pl.BlockSpec((1, hd, vr, LANES), lambda g, i: (g, 0, 0, 0))
    return pl.pallas_call(
        functools.partial(_rwkv_scan_kernel, tb=tb, hd=hd),
        grid=(ngroup, t_total // tb),
        in_specs=[kspec] * 5 + [vspec, sspec],
        out_specs=[vspec, sspec],
        out_shape=[jax.ShapeDtypeStruct(v.shape, F32), jax.ShapeDtypeStruct(s0.shape, F32)],
        compiler_params=_params(("parallel", "arbitrary")),
        name="rwkv_scan",
    )(kn, d, b, k, r, v, s0)


def _mix_out_kernel(ys_ref, bonus_ref, g_ref, ya_ref, h_ref, lnw_ref, lnb_ref, ones_ref, wa_ref, wb_ref, o_ref):
    ones = ones_ref[...]
    inv = 1.0 / RWKV_HEAD_DIM
    y = ys_ref[...]
    yc = y - _dot_sel(y, ones) * inv
    var = _dot_sel(yc * yc, ones) * inv
    yb = (yc * lax.rsqrt(var + GN_EPS) * lnw_ref[...] + lnb_ref[...] + bonus_ref[...]) * g_ref[...]
    o_ref[...] = (h_ref[...] + _dot(ya_ref[...].astype(BF16), wa_ref[...])
                  + _dot(yb.astype(BF16), wb_ref[...]))


def _mix_out(ys, bonus, g, ya, h, consts, *, nseq, nblk, skip):
    width = ys.shape[1]
    d = h.shape[1]
    cmap = lambda i: (i, 0)
    pmap = lambda i: ((i // nblk) * (nblk + skip) + skip + i % nblk, 0)
    wide = pl.BlockSpec((CHUNK, width), pmap)
    return pl.pallas_call(
        _mix_out_kernel,
        grid=(nseq * nblk,),
        in_specs=[pl.BlockSpec((CHUNK, width), cmap), wide, wide, wide, pl.BlockSpec((CHUNK, d), cmap)]
                 + [_full(a.shape) for a in consts],
        out_specs=pl.BlockSpec((CHUNK, d), cmap),
        out_shape=jax.ShapeDtypeStruct(h.shape, F32),
        compiler_params=_params(("parallel",)),
        name="mix_out",
    )(ys, bonus, g, ya, h, *consts)


def _cand_blocks():
    k = PEER_TOPK
    blocks = []
    a = 0
    while k // (a + 1) > 1:
        blocks.append((a, 1, -(-(k // (a + 1)) // SUBLANES) * SUBLANES))
        a += 1
    assert (k - a) % SUBLANES == 0
    blocks.append((a, k - a, 1))
    ids = np.concatenate([a0 * k + np.arange(nb) if na == 1 else (a0 + np.arange(na)) * k for a0, na, nb in blocks])
    return tuple(blocks), ids.astype(np.int32)


def _peer_route_kernel(h_ref, g_ref, wq_hi_ref, wq_lo_ref, k_hi_ref, k_lo_ref, cid_ref, u_o, idx_o, gate_o,
                       s_scr, v_scr, p_scr, c_scr, e_scr, ts_scr, ex_scr, *, blocks):
    u = _rmsnorm(h_ref[...], g_ref[...])
    u_o[...] = u
    q_hi, q_lo = _split2(_dot_hp(u, wq_hi_ref[...], wq_lo_ref[...]))
    tm = u.shape[0]
    k = PEER_TOPK
    nchain = 2 * PEER_HEADS
    for c in range(nchain):
        qh = q_hi[:, c * N_KEYS:(c + 1) * N_KEYS]
        ql = q_lo[:, c * N_KEYS:(c + 1) * N_KEYS]
        kh = k_hi_ref[c]
        s_scr[c] = _dot_nt(kh, qh) + (_dot_nt(k_lo_ref[c], qh) + _dot_nt(kh, ql))
    rid = lax.broadcasted_iota(I32, (N_KEYS, tm), 0)

    def top_keys(i, carry):
        for c in range(nchain):
            s = s_scr[c]
            m = jnp.max(s, axis=0, keepdims=True)
            p = jnp.min(jnp.where(s == m, rid, N_KEYS), axis=0, keepdims=True)
            v_scr[c, pl.ds(i, 1), :] = m
            p_scr[c, pl.ds(i, 1), :] = p
            s_scr[c] = jnp.where(rid == p, -jnp.inf, s)
        return carry

    lax.fori_loop(0, k, top_keys, 0)

    for h in range(PEER_HEADS):
        v1, v2 = v_scr[2 * h], v_scr[2 * h + 1]
        i1, i2 = p_scr[2 * h] * N_KEYS, p_scr[2 * h + 1]
        cand, eid = [], []
        for a0, na, nb in blocks:
            cand.append(v1[a0:a0 + na] + v2[0:nb])
            eid.append(i1[a0:a0 + na] + i2[0:nb])
        c_scr[h] = jnp.concatenate(cand, axis=0)
        e_scr[h] = jnp.concatenate(eid, axis=0)
    cid = cid_ref[...]
    ncand = cid.shape[0]

    def top_cands(i, carry):
        for h in range(PEER_HEADS):
            cand = c_scr[h]
            m = jnp.max(cand, axis=0, keepdims=True)
            c = jnp.min(jnp.where(cand == m, cid, k * k), axis=0, keepdims=True)
            hit = cid == c
            e = jnp.max(jnp.where(hit, e_scr[h], -1), axis=0, keepdims=True)
            c_scr[h] = jnp.where(hit, -jnp.inf, cand)
            ts_scr[pl.ds(h * k + i, 1), :] = m
            ex_scr[pl.ds(h * k + i, 1), :] = e
        return carry

    lax.fori_loop(0, k, top_cands, 0)
    idx_o[...] = ex_scr[...].T
    for h in range(PEER_HEADS):
        ts = ts_scr[h * k:(h + 1) * k, :]
        ex = jnp.exp(ts - jnp.max(ts, axis=0, keepdims=True))
        gate_o[h * k:(h + 1) * k, :] = ex / jnp.sum(ex, axis=0, keepdims=True)


def _peer_route(h, consts, blocks):
    n, d = h.shape
    tm = CHUNK
    nsel = PEER_HEADS * PEER_TOPK
    ncand = consts[-1].shape[0]
    rows = pl.BlockSpec((tm, d), lambda i: (i, 0))
    sel = pl.BlockSpec((nsel, tm), lambda i: (0, i))
    return pl.pallas_call(
        functools.partial(_peer_route_kernel, blocks=blocks),
        grid=(n // tm,),
        in_specs=[rows] + [_full(a.shape) for a in consts],
        out_specs=[rows, pl.BlockSpec((tm, nsel), lambda i: (i, 0)), sel],
        out_shape=[jax.ShapeDtypeStruct((n, d), F32), jax.ShapeDtypeStruct((n, nsel), I32),
                   jax.ShapeDtypeStruct((nsel, n), F32)],
        scratch_shapes=[pltpu.VMEM((2 * PEER_HEADS, N_KEYS, tm), F32),
                        pltpu.VMEM((2 * PEER_HEADS, PEER_TOPK, tm), F32),
                        pltpu.VMEM((2 * PEER_HEADS, PEER_TOPK, tm), I32),
                        pltpu.VMEM((PEER_HEADS, ncand, tm), F32),
                        pltpu.VMEM((PEER_HEADS, ncand, tm), I32),
                        pltpu.VMEM((nsel, tm), F32),
                        pltpu.VMEM((nsel, tm), I32)],
        compiler_params=_params(("parallel",)),
        name="peer_route",
    )(h, *consts)


PEER_UNROLL = 4


def _gather_rows(idx_ref, tab_ref, gbuf, i, slot, nsel):
    for j in range(nsel):
        gbuf[slot, j] = tab_ref[idx_ref[i, j]]
    return gbuf[slot].reshape(nsel * SUBLANES, LANES)


def _diag_mask(nsel):
    sub = lax.broadcasted_iota(I32, (SUBLANES, nsel * SUBLANES), 0)
    lane = lax.broadcasted_iota(I32, (SUBLANES, nsel * SUBLANES), 1)
    return sub == lane % SUBLANES


def _gelu(x):
    return 0.5 * x * (1.0 + lax.erf(x * np.float32(1.0 / np.sqrt(2.0))))


def _peer_u_kernel(idx_ref, tok_ref, gate_ref, tab_ref, eexp_ref, act_o, gbuf, rsum, *, tm, nsel):
    diag = _diag_mask(nsel)

    def body(p, carry):
        for s in range(PEER_UNROLL):
            i = p * PEER_UNROLL + s
            rows = _gather_rows(idx_ref, tab_ref, gbuf, i, s, nsel)
            tok = tok_ref[i].astype(BF16)
            prod = _dot_nt(tok, rows)
            rsum[pl.ds(i, 1), :] = jnp.sum(jnp.where(diag, prod, 0.0), axis=0, keepdims=True)
        return carry

    lax.fori_loop(0, tm // PEER_UNROLL, body, 0)
    eexp = eexp_ref[...]
    hi, mid, lo = _split3(rsum[...])
    pre_t = _dot_nt(eexp, hi) + _dot_nt(eexp, mid) + _dot_nt(eexp, lo)
    act_o[...] = _gelu(pre_t) * gate_ref[...]


def _peer_v_kernel(idx_ref, act_ref, tab_ref, eexp_ref, out_o, gbuf, arep, *, tm, nsel):
    diag = _diag_mask(nsel)
    arep[...] = _dot(act_ref[...].T.astype(BF16), eexp_ref[...])

    def body(p, carry):
        for s in range(PEER_UNROLL):
            i = p * PEER_UNROLL + s
            rows = _gather_rows(idx_ref, tab_ref, gbuf, i, s, nsel)
            a = jnp.broadcast_to(arep[pl.ds(i, 1), :], diag.shape)
            out_o[i] = _dot(jnp.where(diag, a, 0.0).astype(BF16), rows)
        return carry

    lax.fori_loop(0, tm // PEER_UNROLL, body, 0)


def _peer_gather_specs(tm, nsel, tab):
    idx = pl.BlockSpec((tm, nsel), lambda i: (i, 0), memory_space=pltpu.SMEM)
    table = pl.BlockSpec(tab.shape, lambda i: (0, 0, 0), pipeline_mode=pl.Buffered(1))
    gbuf = pltpu.VMEM((PEER_UNROLL, nsel, SUBLANES, LANES), BF16)
    return idx, table, gbuf


def _peer_u(idx, tok3, gate, tab, eexp, *, tm):
    n, nsel = idx.shape
    idx_spec, tab_spec, gbuf = _peer_gather_specs(tm, nsel, tab)
    sel = pl.BlockSpec((nsel, tm), lambda i: (0, i))
    return pl.pallas_call(
        functools.partial(_peer_u_kernel, tm=tm, nsel=nsel),
        grid=(n // tm,),
        in_specs=[idx_spec, pl.BlockSpec((tm, SUBLANES, LANES), lambda i: (i, 0, 0)), sel, tab_spec,
                  _full(eexp.shape)],
        out_specs=sel,
        out_shape=jax.ShapeDtypeStruct((nsel, n), F32),
        scratch_shapes=[gbuf, pltpu.VMEM((tm, nsel * SUBLANES), F32)],
        compiler_params=_params(("arbitrary",)),
        name="peer_u",
    )(idx, tok3, gate, tab, eexp)


def _peer_v(idx, act, tab, eexp, *, tm):
    n, nsel = idx.shape
    idx_spec, tab_spec, gbuf = _peer_gather_specs(tm, nsel, tab)
    return pl.pallas_call(
        functools.partial(_peer_v_kernel, tm=tm, nsel=nsel),
        grid=(n // tm,),
        in_specs=[idx_spec, pl.BlockSpec((nsel, tm), lambda i: (0, i)), tab_spec, _full(eexp.shape)],
        out_specs=pl.BlockSpec((tm, SUBLANES, LANES), lambda i: (i, 0, 0)),
        out_shape=jax.ShapeDtypeStruct((n, SUBLANES, LANES), F32),
        scratch_shapes=[gbuf, pltpu.VMEM((tm, nsel * SUBLANES), F32)],
        compiler_params=_params(("arbitrary",)),
        name="peer_v",
    )(idx, act, tab, eexp)


def _final_norm_kernel(h_ref, f_ref, g_ref, o_ref):
    o_ref[...] = _rmsnorm(h_ref[...] + f_ref[...], g_ref[...])


def _final_norm(h, f, g):
    n, d = h.shape
    rows = pl.BlockSpec((CHUNK, d), lambda i: (i, 0))
    return pl.pallas_call(
        _final_norm_kernel,
        grid=(n // CHUNK,),
        in_specs=[rows, rows, _full(g.shape)],
        out_specs=rows,
        out_shape=jax.ShapeDtypeStruct((n, d), F32),
        compiler_params=_params(("parallel",)),
        name="final_norm",
    )(h, f, g)


def _hi_lo(w):
    hi = w.astype(BF16)
    return hi, (w - hi.astype(F32)).astype(BF16)


def _block_ones(n, blk):
    i = np.arange(n)
    return jnp.asarray(i[:, None] // blk == i[None, :] // blk, dtype=BF16)


def _expert_table(t):
    n, d = t.shape
    return t.astype(BF16).reshape(n, d // LANES, LANES)


def _stream(x_rows, lead, conv0, ssd0, shift0, wkv0, w, *, nseq, nblk, lb, has_lead, npad, tb):
    d = x_rows.shape[1]
    nblk_all = nblk + (1 if has_lead else 0)
    width_a = w["ssm_norm"].shape[1]
    width_b = w["ones_b"].shape[0]
    heads_b = width_b // RWKV_HEAD_DIM
    if lb == CHUNK:
        xbc, z, dtp, rw = _proj_in(x_rows, lead, w["norm_mix"], w["w_in"], w["widths"],
                                   nseq=nseq, nblk=nblk_all, has_lead=has_lead)
        nchunk = nblk_all
    else:
        xbc, z, dtp, rw = _proj_in(x_rows, lead, w["norm_mix"], w["w_in"], w["widths"],
                                   nseq=1, nblk=x_rows.shape[0] // CHUNK, has_lead=False)
        nchunk = 1
    ya, conv_new, ssd_new = _ssd(xbc, z, dtp, conv0, ssd0, w["ssd"], nseq=nseq, lb=lb, nchunk=nchunk, npad=npad)
    r, dcy, kp, kn, b, v, g, bonus, shift_new = _rwkv_pre(
        rw, shift0[:, None, :], w["rwkv_pre"], nseq=nseq, lb=lb, nchunk=nchunk,
        npad=npad if lb == CHUNK else 0, width=width_b)

    t_all = nchunk * lb
    t0 = npad if lb == CHUNK else 0
    t_real = t_all - t0
    pairs = nseq * heads_b
    hd = RWKV_HEAD_DIM
    if pairs >= LANES:
        ngroup, dup = pairs // LANES, 1
    else:
        ngroup, dup = 1, LANES // pairs
    seq_per = nseq // ngroup
    vr = hd // dup

    def to_scan_k(a):
        a = a.reshape(ngroup, seq_per, t_all, heads_b, hd)[:, :, t0:]
        a = a.transpose(0, 2, 4, 1, 3).reshape(ngroup, t_real, hd, seq_per * heads_b)
        return jnp.tile(a, (1, 1, 1, dup))

    def to_scan_v(a):
        a = a.reshape(ngroup, seq_per, t_all, heads_b, dup, vr)[:, :, t0:]
        return a.transpose(0, 2, 5, 4, 1, 3).reshape(ngroup, t_real, vr, LANES)

    s0 = wkv0.reshape(ngroup, seq_per, heads_b, dup, vr, hd).transpose(0, 5, 4, 3, 1, 2)
    s0 = s0.reshape(ngroup, hd, vr, LANES)
    ysc, s_new = _rwkv_scan(to_scan_k(kn), to_scan_k(dcy), to_scan_k(b), to_scan_k(kp), to_scan_k(r),
                            to_scan_v(v), s0, tb=tb)
    wkv_new = s_new.reshape(ngroup, hd, vr, dup, seq_per, heads_b).transpose(0, 4, 5, 3, 2, 1)
    wkv_new = wkv_new.reshape(nseq, heads_b, hd, hd)
    skip_t = t_real - nblk * lb if lb == CHUNK else 0
    ys = ysc[:, skip_t:].reshape(ngroup, t_real - skip_t, vr, dup, seq_per, heads_b)
    ys = ys.transpose(0, 4, 1, 5, 3, 2).reshape(nseq * (t_real - skip_t), width_b)

    if lb == CHUNK:
        h1 = _mix_out(ys, bonus, g, ya, x_rows, w["mix_out"], nseq=nseq, nblk=nblk, skip=nchunk - nblk)
    else:
        h1 = _mix_out(ys, bonus, g, ya, x_rows, w["mix_out"], nseq=1, nblk=x_rows.shape[0] // CHUNK, skip=0)

    u, idx, gate = _peer_route(h1, w["route"], w["cand_blocks"])
    n = u.shape[0]
    act = _peer_u(idx, u.reshape(n, SUBLANES, LANES), gate, w["tab_u"], w["eexp"], tm=w["peer_tm"])
    ffn = _peer_v(idx, act, w["tab_v"], w["eexp"], tm=w["peer_tm"])
    y = _final_norm(h1, ffn.reshape(n, d), w["norm_final"])
    return y, conv_new, ssd_new, shift_new[:, 0, :], wkv_new


def kernel(x_prompt, x_sample, state_conv, state_ssd, state_shift, state_wkv, meta_tokens, norm_mix, w_in, conv_w, conv_b, dt_bias, a_log, d_skip, ssm_norm, shift_mu, decay_w0, decay_w2, iclr_a0, iclr_a2, gate_g2, k_k, k_a, r_k, lnx_w, lnx_b, w_out, norm_ffn, w_query, sub_keys, expert_u, expert_v, norm_final):
    bp, seq_p, d = x_prompt.shape
    bs, seq_s, _ = x_sample.shape
    depth = w_in.shape[0]
    assert depth == 1 and seq_p % CHUNK == 0 and (bs * seq_s) % CHUNK == 0 and seq_s % SUBLANES == 0
    heads_a = state_ssd.shape[2]
    width_a = heads_a * SSD_HEAD_DIM
    conv_dim = state_conv.shape[3]
    rw_in = state_shift.shape[2]
    heads_b = state_wkv.shape[2]
    width_b = heads_b * RWKV_HEAD_DIM
    assert heads_a <= LANES and rw_in == 3 * width_b + DECAY_LORA + AAA_LORA + GATE_LORA

    wi = w_in[0]
    ssd_in = width_a + conv_dim + heads_a
    w_cat = jnp.concatenate([
        wi[:, width_a:width_a + conv_dim], wi[:, :width_a],
        jnp.pad(wi[:, width_a + conv_dim:ssd_in], ((0, 0), (0, LANES - heads_a))),
        wi[:, ssd_in:]], axis=1).astype(BF16)
    widths = (conv_dim, width_a, LANES, rw_in)
    pad_h = lambda a: jnp.pad(a.reshape(1, -1), ((0, 0), (0, LANES - heads_a)))
    i_l = np.arange(CHUNK)
    tri = jnp.asarray(i_l[:, None] >= i_l[None, :], dtype=BF16)
    ehead = jnp.asarray(np.arange(LANES)[:, None] == np.arange(width_a)[None, :] // SSD_HEAD_DIM, dtype=BF16)
    ecol = jnp.asarray(np.arange(LANES)[:, None] == np.arange(heads_a * CHUNK)[None, :] // CHUNK, dtype=BF16)
    ssd_consts = (conv_w[0], conv_b[0].reshape(1, -1), pad_h(dt_bias[0]), pad_h(a_log[0]),
                  jnp.repeat(d_skip[0], SSD_HEAD_DIM).reshape(1, -1), ssm_norm[0].reshape(1, -1), tri, ehead, ecol)
    ones_b = _block_ones(width_b, RWKV_HEAD_DIM)
    zero = jnp.zeros((DECAY_LORA, width_b), F32)
    w_wa = jnp.concatenate([jnp.concatenate([decay_w2[0], zero], axis=1),
                            jnp.concatenate([jnp.zeros((AAA_LORA, width_b), F32), iclr_a2[0]], axis=1)], axis=0)
    row = lambda a: a.reshape(1, -1)
    pre_consts = (row(shift_mu[0]), row(decay_w0[0]), row(iclr_a0[0]), *_hi_lo(w_wa), *_hi_lo(gate_g2[0]),
                  row(k_k[0]), row(k_a[0]), row(r_k[0]), ones_b)
    wo = w_out[0].astype(BF16)
    mix_consts = (row(lnx_w[0]), row(lnx_b[0]), ones_b, wo[:width_a], wo[width_a:])
    cand_blocks, cand_ids = _cand_blocks()
    keys = sub_keys[0].transpose(1, 0, 2, 3).reshape(2 * PEER_HEADS, N_KEYS, -1)
    route_consts = (row(norm_ffn[0]), *_hi_lo(w_query[0]), *_hi_lo(keys),
                    jnp.asarray(np.broadcast_to(cand_ids[:, None], (cand_ids.shape[0], CHUNK))))
    nsel = PEER_HEADS * PEER_TOPK
    lane8 = np.arange(nsel * SUBLANES)
    eexp = jnp.asarray(np.arange(nsel)[:, None] == lane8[None, :] // SUBLANES, dtype=BF16)
    w = dict(norm_mix=row(norm_mix[0]), w_in=w_cat, widths=widths, ssm_norm=row(ssm_norm[0]), ones_b=ones_b,
             ssd=ssd_consts, rwkv_pre=pre_consts, mix_out=mix_consts, route=route_consts,
             tab_u=_expert_table(expert_u[0]), tab_v=_expert_table(expert_v[0]), eexp=eexp, cand_blocks=cand_blocks,
             norm_final=row(norm_final), peer_tm=CHUNK)

    npad = CHUNK - N_META
    lead = jnp.concatenate([jnp.zeros((npad, d), F32), meta_tokens.astype(F32)], axis=0)
    zeros = lambda *s: jnp.zeros(s, F32)
    yp, cp, sp, shp, wp = _stream(
        x_prompt.reshape(bp * seq_p, d), lead,
        zeros(bp, CONV_W - 1, conv_dim), zeros(bp, heads_a, SSD_HEAD_DIM, SSD_STATE), zeros(bp, rw_in),
        zeros(bp, heads_b, RWKV_HEAD_DIM, RWKV_HEAD_DIM), w,
        nseq=bp, nblk=seq_p // CHUNK, lb=CHUNK, has_lead=True, npad=npad, tb=N_META)
    ys, cs, ss, shs, ws = _stream(
        x_sample.reshape(bs * seq_s, d), lead, state_conv[0], state_ssd[0], state_shift[0], state_wkv[0], w,
        nseq=bs, nblk=1, lb=seq_s, has_lead=False, npad=CHUNK - seq_s, tb=seq_s)
    return (yp.reshape(bp, seq_p, d), ys.reshape(bs, seq_s, d), cp[None], sp[None], shp[None], wp[None],
            cs[None], ss[None], shs[None], ws[None])
```

```python
import functools

import jax
import jax.numpy as jnp
import numpy as np
from jax import lax
from jax.experimental import pallas as pl
from jax.experimental.pallas import tpu as pltpu

F32 = jnp.float32
BF16 = jnp.bfloat16
I32 = jnp.int32

N_META = 16
SSD_HEAD_DIM = 64
SSD_GROUPS = 2
SSD_STATE = 128
CONV_W = 4
RWKV_HEAD_DIM = 64
DECAY_LORA = 64
AAA_LORA = 64
GATE_LORA = 128
PEER_HEADS = 8
N_KEYS = 128
PEER_TOPK = 16
RMS_EPS = 1e-6
GN_EPS = 64e-5

LANES = 128
SUBLANES = 8
CHUNK = 128
VMEM_LIMIT_BYTES = 56 * 1024 * 1024


def _full(shape):
    zeros = (0,) * len(shape)
    return pl.BlockSpec(shape, lambda *_: zeros)


def _params(semantics, vmem=VMEM_LIMIT_BYTES):
    return pltpu.CompilerParams(dimension_semantics=semantics, vmem_limit_bytes=vmem)


def _split2(x):
    hi = x.astype(BF16)
    lo = (x - hi.astype(F32)).astype(BF16)
    return hi, lo


def _split3(x):
    hi = x.astype(BF16)
    r = x - hi.astype(F32)
    mid = r.astype(BF16)
    lo = (r - mid.astype(F32)).astype(BF16)
    return hi, mid, lo


def _dot(a, b):
    return jnp.dot(a, b, preferred_element_type=F32)


def _dot_nt(a, b):
    return lax.dot_general(a, b, (((1,), (1,)), ((), ())), preferred_element_type=F32)


def _dot_sel(x, sel):
    hi, mid, lo = _split3(x)
    return _dot(hi, sel) + _dot(mid, sel) + _dot(lo, sel)


def _dot_hp(x, w_hi, w_lo):
    hi, lo = _split2(x)
    return _dot(hi, w_hi) + (_dot(lo, w_hi) + _dot(hi, w_lo))


def _dot_hp_nt(x, w_hi, w_lo):
    hi, lo = _split2(x)
    return _dot_nt(hi, w_hi) + (_dot_nt(lo, w_hi) + _dot_nt(hi, w_lo))


def _silu(x):
    return x * jax.nn.sigmoid(x)


def _softplus(x):
    return jnp.maximum(x, 0.0) + jnp.log1p(jnp.exp(-jnp.abs(x)))


def _rmsnorm(x, g):
    ms = jnp.mean(x * x, axis=-1, keepdims=True)
    return x * lax.rsqrt(ms + RMS_EPS) * g


def _proj_in_kernel(x_ref, lead_ref, g_ref, w_ref, xbc_ref, z_ref, dt_ref, rw_ref, *, widths, has_lead):
    x = x_ref[...]
    if has_lead:
        x = jnp.where(pl.program_id(1) == 0, lead_ref[...], x)
    u = _rmsnorm(x, g_ref[...]).astype(BF16)
    p = _dot(u, w_ref[...])
    off = 0
    for ref, w in zip((xbc_ref, z_ref, dt_ref, rw_ref), widths):
        ref[...] = p[:, off:off + w]
        off += w


def _proj_in(rows, lead, g, w_cat, widths, *, nseq, nblk, has_lead):
    d = rows.shape[1]
    nblk_in = nblk - 1 if has_lead else nblk
    shift = 1 if has_lead else 0
    x_map = lambda s, j: (s * nblk_in + jnp.maximum(j - shift, 0), 0)
    o_map = lambda s, j: (s * nblk + j, 0)
    return pl.pallas_call(
        functools.partial(_proj_in_kernel, widths=widths, has_lead=has_lead),
        grid=(nseq, nblk),
        in_specs=[pl.BlockSpec((CHUNK, d), x_map), _full(lead.shape), _full(g.shape), _full(w_cat.shape)],
        out_specs=[pl.BlockSpec((CHUNK, w), o_map) for w in widths],
        out_shape=[jax.ShapeDtypeStruct((nseq * nblk * CHUNK, w), F32) for w in widths],
        compiler_params=_params(("parallel", "parallel")),
        name="proj_in",
    )(rows, lead, g, w_cat)


def _cumsum_rows(x, tri_ref):
    hi, mid, lo = _split3(x)
    tri = tri_ref[...]
    return _dot(tri, hi) + _dot(tri, mid) + _dot(tri, lo)


def _ssd_kernel(xbc_ref, z_ref, dt_ref, cs_ref, h0_ref, cw_ref, cb_ref, dtb_ref, alog_ref,
                dskip_ref, norm_ref, tri_ref, ehead_ref, ecol_ref,
                y_ref, cs_out_ref, h_out_ref, xfull, dtile, *, lb, npad, heads, width):
    c = pl.program_id(1)
    nc = pl.num_programs(1)
    L = CHUNK
    prev = CONV_W - 1
    base = SUBLANES
    state = SSD_STATE
    hd = SSD_HEAD_DIM
    per_group = heads // SSD_GROUPS

    @pl.when(c == 0)
    def _():
        h_out_ref[...] = h0_ref[...]
        if lb == L:
            xfull[base - prev:base, :] = cs_ref[0]

    if lb == L:
        xfull[base:base + L, :] = xbc_ref[...]
        dt_raw = dt_ref[...]
    else:
        xfull[...] = jnp.zeros_like(xfull)
        xfull[base + L - lb - prev:base + L - lb, :] = cs_ref[0]
        xfull[base + L - lb:base + L, :] = xbc_ref[...]
        dtile[...] = jnp.zeros_like(dtile)
        dtile[L - lb:L, :] = dt_ref[...]
        dt_raw = dtile[...]

    conv = cb_ref[...] + xfull[base - prev:base - prev + L, :] * cw_ref[0:1, :]
    for k in range(1, CONV_W):
        conv = conv + xfull[base - prev + k:base - prev + k + L, :] * cw_ref[k:k + 1, :]
    cs_new = xfull[base + L - prev:base + L, :]

    @pl.when(c == nc - 1)
    def _():
        cs_out_ref[0] = cs_new

    xfull[base - prev:base, :] = cs_new

    xbc = _silu(conv)
    xs = xbc[:, :width]
    gn = SSD_GROUPS * state
    row = lax.broadcasted_iota(I32, (L, LANES), 0) + c * L
    dt = jnp.where(row >= npad, _softplus(dt_raw + dtb_ref[...]), 0.0)
    da = dt * (-jnp.exp(alog_ref[...]))
    acum = _cumsum_rows(da, tri_ref)
    acum_t = acum.T
    ehead = ehead_ref[...]
    acum_x = _dot_sel(acum, ehead)
    dt_x = _dot_sel(dt, ehead)
    last_x = acum_x[L - 1:L, :]
    xdt = xs * dt_x
    wend = xdt * jnp.exp(last_x - acum_x)
    colb = _dot_sel(acum, ecol_ref[...])
    causal = (lax.broadcasted_iota(I32, (L, L), 0) >= lax.broadcasted_iota(I32, (L, L), 1))
    lane = lax.broadcasted_iota(I32, (L, LANES), 1)

    y_diag = []
    y_off = []
    for g in range(SSD_GROUPS):
        bm = xbc[:, width + g * state:width + (g + 1) * state].astype(BF16)
        cm = xbc[:, width + gn + g * state:width + gn + (g + 1) * state].astype(BF16)
        cb = _dot_nt(cm, bm)
        h_prev = h_out_ref[0, g * per_group:(g + 1) * per_group].reshape(per_group * hd, state)
        y_off.append(_dot_nt(cm, h_prev.astype(BF16)))
        for pair in range(per_group // 2):
            h_a = g * per_group + 2 * pair
            xp = xdt[:, h_a * hd:(h_a + 2) * hd].astype(BF16)
            outs = []
            for h in (h_a, h_a + 1):
                seg = colb[:, h * L:(h + 1) * L] - acum_t[h:h + 1, :]
                dec = jnp.exp(jnp.where(causal, seg, -jnp.inf))
                outs.append(_dot((cb * dec).astype(BF16), xp))
            y_diag.append(jnp.where(lane < hd, outs[0], outs[1]))
        wg_t = wend[:, g * per_group * hd:(g + 1) * per_group * hd].T.astype(BF16)
        upd = _dot(wg_t, bm)
        for i in range(per_group):
            h = g * per_group + i
            cd = jnp.exp(acum_t[h:h + 1, L - 1:L])
            h_out_ref[0, h] = h_out_ref[0, h] * cd + upd[i * hd:(i + 1) * hd, :]
    y = (jnp.concatenate(y_diag, axis=1) + jnp.concatenate(y_off, axis=1) * jnp.exp(acum_x)
         + xs * dskip_ref[...])
    y = y[L - lb:, :] * _silu(z_ref[...])
    y_ref[...] = _rmsnorm(y, norm_ref[...])


def _ssd(xbc, z, dtp, cs, h0, consts, *, nseq, lb, nchunk, npad):
    heads = h0.shape[1]
    width = z.shape[1]
    conv_dim = xbc.shape[1]
    row_map = lambda s, c: (s * nchunk + c, 0)
    kernel = functools.partial(_ssd_kernel, lb=lb, npad=npad, heads=heads, width=width)
    return pl.pallas_call(
        kernel,
        grid=(nseq, nchunk),
        in_specs=[pl.BlockSpec((lb, conv_dim), row_map),
                  pl.BlockSpec((lb, width), row_map),
                  pl.BlockSpec((lb, LANES), row_map),
                  pl.BlockSpec((1,) + cs.shape[1:], lambda s, c: (s, 0, 0)),
                  pl.BlockSpec((1,) + h0.shape[1:], lambda s, c: (s, 0, 0, 0))]
                 + [_full(a.shape) for a in consts],
        out_specs=[pl.BlockSpec((lb, width), row_map),
                   pl.BlockSpec((1,) + cs.shape[1:], lambda s, c: (s, 0, 0)),
                   pl.BlockSpec((1,) + h0.shape[1:], lambda s, c: (s, 0, 0, 0))],
        out_shape=[jax.ShapeDtypeStruct((nseq * nchunk * lb, width), F32),
                   jax.ShapeDtypeStruct(cs.shape, F32),
                   jax.ShapeDtypeStruct(h0.shape, F32)],
        scratch_shapes=[pltpu.VMEM((SUBLANES + CHUNK, conv_dim), F32), pltpu.VMEM((CHUNK, LANES), F32)],
        compiler_params=_params(("parallel", "arbitrary")),
        name="ssd",
    )(xbc, z, dtp, cs, h0, *consts)


def _rwkv_pre_kernel(rw_ref, sh_ref, mu_ref, w0_ref, a0_ref, wa_hi_ref, wa_lo_ref, g2_hi_ref, g2_lo_ref,
                     kk_ref, ka_ref, rk_ref, ones_ref,
                     r_o, d_o, k_o, kn_o, b_o, v_o, g_o, bonus_o, sh_o, pfull, *, lb, npad, width):
    c = pl.program_id(1)
    nc = pl.num_programs(1)
    base = SUBLANES
    p = rw_ref[...]

    @pl.when(c == 0)
    def _():
        pfull[base - 1:base, :] = sh_ref[0]

    pfull[base:base + lb, :] = p
    prev = pfull[base - 1:base - 1 + lb, :]
    last = p[lb - 1:lb, :]
    pfull[base - 1:base, :] = last

    @pl.when(c == nc - 1)
    def _():
        sh_o[0] = last

    pm = p + (prev - p) * mu_ref[...]
    r = pm[:, :width]
    k = pm[:, width:2 * width]
    v = pm[:, 2 * width:3 * width]
    lora_in = pm[:, 3 * width:3 * width + DECAY_LORA + AAA_LORA]
    lane = lax.broadcasted_iota(I32, lora_in.shape, 1)
    lora_in = jnp.where(lane < DECAY_LORA, jnp.tanh(lora_in), lora_in)
    lora = _dot_hp(lora_in, wa_hi_ref[...], wa_lo_ref[...])
    gate_in = jax.nn.sigmoid(pm[:, 3 * width + DECAY_LORA + AAA_LORA:])
    g_o[...] = _dot_hp(gate_in, g2_hi_ref[...], g2_lo_ref[...])
    w = -_softplus(-(w0_ref[...] + lora[:, :width])) - 0.5
    row = lax.broadcasted_iota(I32, w.shape, 0) + c * lb
    d_o[...] = jnp.where(row >= npad, jnp.exp(-jnp.exp(w)), 1.0)
    a = jax.nn.sigmoid(a0_ref[...] + lora[:, width:])
    ones = ones_ref[...]
    kn = k * kk_ref[...]
    kn = kn / jnp.maximum(jnp.sqrt(_dot_sel(kn * kn, ones)), 1e-12)
    kp = k * (1.0 + (a - 1.0) * ka_ref[...])
    r_o[...] = r
    k_o[...] = kp
    kn_o[...] = kn
    b_o[...] = kn * a
    v_o[...] = v
    bonus_o[...] = _dot_sel(r * kp * rk_ref[...], ones) * v


def _rwkv_pre(rw, sh, consts, *, nseq, lb, nchunk, npad, width):
    rw_in = rw.shape[1]
    row_map = lambda s, c: (s * nchunk + c, 0)
    n = nseq * nchunk * lb
    kernel = functools.partial(_rwkv_pre_kernel, lb=lb, npad=npad, width=width)
    return pl.pallas_call(
        kernel,
        grid=(nseq, nchunk),
        in_specs=[pl.BlockSpec((lb, rw_in), row_map), pl.BlockSpec((1, 1, rw_in), lambda s, c: (s, 0, 0))]
                 + [_full(a.shape) for a in consts],
        out_specs=[pl.BlockSpec((lb, width), row_map)] * 8 + [pl.BlockSpec((1, 1, rw_in), lambda s, c: (s, 0, 0))],
        out_shape=[jax.ShapeDtypeStruct((n, width), F32)] * 8 + [jax.ShapeDtypeStruct(sh.shape, F32)],
        scratch_shapes=[pltpu.VMEM((SUBLANES + lb, rw_in), F32)],
        compiler_params=_params(("parallel", "arbitrary")),
        name="rwkv_pre",
    )(rw, sh, *consts)


def _rwkv_scan_kernel(kn_ref, d_ref, b_ref, k_ref, r_ref, v_ref, s0_ref, y_ref, s_ref, *, tb, hd):
    @pl.when(pl.program_id(1) == 0)
    def _():
        s_ref[...] = s0_ref[...]

    def step(t, carry):
        v_t = v_ref[0, t]
        sa = jnp.zeros_like(v_t)
        for k in range(hd):
            sa = sa + s_ref[0, k] * kn_ref[0, t, pl.ds(k, 1), :]
        y = jnp.zeros_like(v_t)
        for k in range(hd):
            sk = (s_ref[0, k] * d_ref[0, t, pl.ds(k, 1), :] - sa * b_ref[0, t, pl.ds(k, 1), :]
                  + v_t * k_ref[0, t, pl.ds(k, 1), :])
            s_ref[0, k] = sk
            y = y + sk * r_ref[0, t, pl.ds(k, 1), :]
        y_ref[0, t] = y
        return carry

    lax.fori_loop(0, tb, step, 0)


def _rwkv_scan(kn, d, b, k, r, v, s0, *, tb):
    ngroup, t_total, hd, _ = kn.shape
    vr = v.shape[2]
    kspec = pl.BlockSpec((1, tb, hd, LANES), lambda g, i: (g, i, 0, 0))
    vspec = pl.BlockSpec((1, tb, vr, LANES), lambda g, i: (g, i, 0, 0))
    sspec = pl.BlockSpec((1, hd, vr, LANES), lambda g, i: (g, 0, 0, 0))
    return pl.pallas_call(
        functools.partial(_rwkv_scan_kernel, tb=tb, hd=hd),
        grid=(ngroup, t_total // tb),
        in_specs=[kspec] * 5 + [vspec, sspec],
        out_specs=[vspec, sspec],
        out_shape=[jax.ShapeDtypeStruct(v.shape, F32), jax.ShapeDtypeStruct(s0.shape, F32)],
        compiler_params=_params(("parallel", "arbitrary")),
        name="rwkv_scan",
    )(kn, d, b, k, r, v, s0)


def _mix_out_kernel(ys_ref, bonus_ref, g_ref, ya_ref, h_ref, lnw_ref, lnb_ref, ones_ref, wa_ref, wb_ref, o_ref):
    ones = ones_ref[...]
    inv = 1.0 / RWKV_HEAD_DIM
    y = ys_ref[...]
    yc = y - _dot_sel(y, ones) * inv
    var = _dot_sel(yc * yc, ones) * inv
    yb = (yc * lax.rsqrt(var + GN_EPS) * lnw_ref[...] + lnb_ref[...] + bonus_ref[...]) * g_ref[...]
    o_ref[...] = (h_ref[...] + _dot(ya_ref[...].astype(BF16), wa_ref[...])
                  + _dot(yb.astype(BF16), wb_ref[...]))


def _mix_out(ys, bonus, g, ya, h, consts, *, nseq, nblk, skip):
    width = ys.shape[1]
    d = h.shape[1]
    cmap = lambda i: (i, 0)
    pmap = lambda i: ((i // nblk) * (nblk + skip) + skip + i % nblk, 0)
    wide = pl.BlockSpec((CHUNK, width), pmap)
    return pl.pallas_call(
        _mix_out_kernel,
        grid=(nseq * nblk,),
        in_specs=[pl.BlockSpec((CHUNK, width), cmap), wide, wide, wide, pl.BlockSpec((CHUNK, d), cmap)]
                 + [_full(a.shape) for a in consts],
        out_specs=pl.BlockSpec((CHUNK, d), cmap),
        out_shape=jax.ShapeDtypeStruct(h.shape, F32),
        compiler_params=_params(("parallel",)),
        name="mix_out",
    )(ys, bonus, g, ya, h, *consts)


def _cand_blocks():
    k = PEER_TOPK
    blocks = []
    a = 0
    while k // (a + 1) > 1:
        blocks.append((a, 1, -(-(k // (a + 1)) // SUBLANES) * SUBLANES))
        a += 1
    assert (k - a) % SUBLANES == 0
    blocks.append((a, k - a, 1))
    ids = np.concatenate([a0 * k + np.arange(nb) if na == 1 else (a0 + np.arange(na)) * k for a0, na, nb in blocks])
    return tuple(blocks), ids.astype(np.int32)


def _peer_route_kernel(h_ref, g_ref, wq_hi_ref, wq_lo_ref, k_hi_ref, k_lo_ref, cid_ref, u_o, idx_o, gate_o,
                       s_scr, v_scr, p_scr, c_scr, e_scr, ts_scr, ex_scr, *, blocks, idx_scale):
    u = _rmsnorm(h_ref[...], g_ref[...])
    u_o[...] = u
    q_hi, q_lo = _split2(_dot_hp(u, wq_hi_ref[...], wq_lo_ref[...]))
    tm = u.shape[0]
    k = PEER_TOPK
    nchain = 2 * PEER_HEADS
    for c in range(nchain):
        qh = q_hi[:, c * N_KEYS:(c + 1) * N_KEYS]
        ql = q_lo[:, c * N_KEYS:(c + 1) * N_KEYS]
        kh = k_hi_ref[c]
        s_scr[c] = _dot_nt(kh, qh) + (_dot_nt(k_lo_ref[c], qh) + _dot_nt(kh, ql))
    rid = lax.broadcasted_iota(I32, (N_KEYS, tm), 0)

    def top_keys(i, carry):
        for c in range(nchain):
            s = s_scr[c]
            m = jnp.max(s, axis=0, keepdims=True)
            p = jnp.min(jnp.where(s == m, rid, N_KEYS), axis=0, keepdims=True)
            v_scr[c, pl.ds(i, 1), :] = m
            p_scr[c, pl.ds(i, 1), :] = p
            s_scr[c] = jnp.where(rid == p, -jnp.inf, s)
        return carry

    lax.fori_loop(0, k, top_keys, 0)

    for h in range(PEER_HEADS):
        v1, v2 = v_scr[2 * h], v_scr[2 * h + 1]
        i1, i2 = p_scr[2 * h] * N_KEYS, p_scr[2 * h + 1]
        cand, eid = [], []
        for a0, na, nb in blocks:
            cand.append(v1[a0:a0 + na] + v2[0:nb])
            eid.append(i1[a0:a0 + na] + i2[0:nb])
        c_scr[h] = jnp.concatenate(cand, axis=0)
        e_scr[h] = jnp.concatenate(eid, axis=0)
    cid = cid_ref[...]
    ncand = cid.shape[0]

    def top_cands(i, carry):
        for h in range(PEER_HEADS):
            cand = c_scr[h]
            m = jnp.max(cand, axis=0, keepdims=True)
            c = jnp.min(jnp.where(cand == m, cid, k * k), axis=0, keepdims=True)
            hit = cid == c
            e = jnp.max(jnp.where(hit, e_scr[h], -1), axis=0, keepdims=True)
            c_scr[h] = jnp.where(hit, -jnp.inf, cand)
            ts_scr[pl.ds(h * k + i, 1), :] = m
            ex_scr[pl.ds(h * k + i, 1), :] = e * idx_scale
        return carry

    lax.fori_loop(0, k, top_cands, 0)
    idx_o[...] = ex_scr[...].T
    for h in range(PEER_HEADS):
        ts = ts_scr[h * k:(h + 1) * k, :]
        ex = jnp.exp(ts - jnp.max(ts, axis=0, keepdims=True))
        gate_o[h * k:(h + 1) * k, :] = ex / jnp.sum(ex, axis=0, keepdims=True)


def _peer_route(h, consts, blocks, idx_scale):
    n, d = h.shape
    tm = CHUNK
    nsel = PEER_HEADS * PEER_TOPK
    ncand = consts[-1].shape[0]
    rows = pl.BlockSpec((tm, d), lambda i: (i, 0))
    sel = pl.BlockSpec((nsel, tm), lambda i: (0, i))
    return pl.pallas_call(
        functools.partial(_peer_route_kernel, blocks=blocks, idx_scale=idx_scale),
        grid=(n // tm,),
        in_specs=[rows] + [_full(a.shape) for a in consts],
        out_specs=[rows, pl.BlockSpec((tm, nsel), lambda i: (i, 0)), sel],
        out_shape=[jax.ShapeDtypeStruct((n, d), F32), jax.ShapeDtypeStruct((n, nsel), I32),
                   jax.ShapeDtypeStruct((nsel, n), F32)],
        scratch_shapes=[pltpu.VMEM((2 * PEER_HEADS, N_KEYS, tm), F32),
                        pltpu.VMEM((2 * PEER_HEADS, PEER_TOPK, tm), F32),
                        pltpu.VMEM((2 * PEER_HEADS, PEER_TOPK, tm), I32),
                        pltpu.VMEM((PEER_HEADS, ncand, tm), F32),
                        pltpu.VMEM((PEER_HEADS, ncand, tm), I32),
                        pltpu.VMEM((nsel, tm), F32),
                        pltpu.VMEM((nsel, tm), I32)],
        compiler_params=_params(("parallel",)),
        name="peer_route",
    )(h, *consts)


PEER_GROUP = 4
PEER_SLOTS = 4
WORDS = 2


def _pack_kernel(t_ref, o_ref):
    r = t_ref.shape[0]
    o_ref[...] = pltpu.bitcast(t_ref[...].reshape(r * SUBLANES, LANES), I32)


def _pack_table(t):
    n, d = t.shape
    rows = d // LANES
    blk = 512
    t3 = t.astype(BF16).reshape(n, rows, LANES)
    return pl.pallas_call(
        _pack_kernel,
        grid=(n // blk,),
        in_specs=[pl.BlockSpec((blk, rows, LANES), lambda i: (i, 0, 0))],
        out_specs=pl.BlockSpec((blk * rows // WORDS, LANES), lambda i: (i, 0)),
        out_shape=jax.ShapeDtypeStruct((n * rows // WORDS, LANES), I32),
        compiler_params=_params(("parallel",)),
        name="pack_table",
    )(t3)


def _for_each_token(idx_ref, ids, sem, tm, token_fn):
    ngrp = tm // PEER_GROUP

    def copy(grp, slot):
        return pltpu.make_async_copy(idx_ref.at[pl.ds(grp * PEER_GROUP, PEER_GROUP)], ids.at[slot], sem.at[slot])

    for slot in range(PEER_SLOTS):
        copy(slot, slot).start()

    def body(q, carry):
        for slot in range(PEER_SLOTS):
            grp = q * PEER_SLOTS + slot
            copy(grp, slot).wait()
            for s in range(PEER_GROUP):
                token_fn(grp * PEER_GROUP + s, slot, s)

            @pl.when(grp + PEER_SLOTS < ngrp)
            def _():
                copy(grp + PEER_SLOTS, slot).start()
        return carry

    lax.fori_loop(0, ngrp // PEER_SLOTS, body, 0)


def _gather_rows(ids, tab_ref, gbuf, slot, s, nsel):
    rows_per = SUBLANES // WORDS
    for j in range(nsel):
        e = pl.multiple_of(ids[slot, s, j], rows_per)
        gbuf[s, j * rows_per:(j + 1) * rows_per, :] = tab_ref[pl.ds(e, rows_per), :]
    return pltpu.bitcast(gbuf[s], BF16)


def _diag_mask(nsel):
    sub = lax.broadcasted_iota(I32, (SUBLANES, nsel * SUBLANES), 0)
    lane = lax.broadcasted_iota(I32, (SUBLANES, nsel * SUBLANES), 1)
    return sub == lane % SUBLANES


def _gelu(x):
    return 0.5 * x * (1.0 + lax.erf(x * np.float32(1.0 / np.sqrt(2.0))))


def _peer_u_kernel(idx_ref, tok_ref, gate_ref, tab_ref, eexp_ref, act_o, gbuf, rsum, ids, sem, *, tm, nsel):
    diag = _diag_mask(nsel)

    def token(i, slot, s):
        rows = _gather_rows(ids, tab_ref, gbuf, slot, s, nsel)
        tok = tok_ref[i].astype(BF16)
        prod = _dot_nt(tok, rows)
        rsum[pl.ds(i, 1), :] = jnp.sum(jnp.where(diag, prod, 0.0), axis=0, keepdims=True)

    _for_each_token(idx_ref, ids, sem, tm, token)
    eexp = eexp_ref[...]
    hi, mid, lo = _split3(rsum[...])
    pre_t = _dot_nt(eexp, hi) + _dot_nt(eexp, mid) + _dot_nt(eexp, lo)
    act_o[...] = _gelu(pre_t) * gate_ref[...]


def _peer_v_kernel(idx_ref, act_ref, tab_ref, eexp_ref, out_o, gbuf, arep, ids, sem, *, tm, nsel):
    diag = _diag_mask(nsel)
    arep[...] = _dot(act_ref[...].T.astype(BF16), eexp_ref[...])

    def token(i, slot, s):
        rows = _gather_rows(ids, tab_ref, gbuf, slot, s, nsel)
        a = jnp.broadcast_to(arep[pl.ds(i, 1), :], diag.shape)
        out_o[i] = _dot(jnp.where(diag, a, 0.0).astype(BF16), rows)

    _for_each_token(idx_ref, ids, sem, tm, token)


def _peer_gather_specs(tm, nsel, tab):
    idx = pl.BlockSpec((tm, nsel), lambda i: (i, 0))
    table = pl.BlockSpec(tab.shape, lambda i: (0, 0), pipeline_mode=pl.Buffered(1))
    scratch = [pltpu.VMEM((PEER_GROUP, nsel * SUBLANES // WORDS, LANES), I32),
               pltpu.VMEM((tm, nsel * SUBLANES), F32),
               pltpu.SMEM((PEER_SLOTS, PEER_GROUP, nsel), I32),
               pltpu.SemaphoreType.DMA((PEER_SLOTS,))]
    return idx, table, scratch


def _peer_u(idx, tok3, gate, tab, eexp, *, tm):
    n, nsel = idx.shape
    assert tm % (PEER_GROUP * PEER_SLOTS) == 0
    idx_spec, tab_spec, scratch = _peer_gather_specs(tm, nsel, tab)
    sel = pl.BlockSpec((nsel, tm), lambda i: (0, i))
    return pl.pallas_call(
        functools.partial(_peer_u_kernel, tm=tm, nsel=nsel),
        grid=(n // tm,),
        in_specs=[idx_spec, pl.BlockSpec((tm, SUBLANES, LANES), lambda i: (i, 0, 0)), sel, tab_spec,
                  _full(eexp.shape)],
        out_specs=sel,
        out_shape=jax.ShapeDtypeStruct((nsel, n), F32),
        scratch_shapes=scratch,
        compiler_params=_params(("arbitrary",)),
        name="peer_u",
    )(idx, tok3, gate, tab, eexp)


def _peer_v(idx, act, tab, eexp, *, tm):
    n, nsel = idx.shape
    assert tm % (PEER_GROUP * PEER_SLOTS) == 0
    idx_spec, tab_spec, scratch = _peer_gather_specs(tm, nsel, tab)
    return pl.pallas_call(
        functools.partial(_peer_v_kernel, tm=tm, nsel=nsel),
        grid=(n // tm,),
        in_specs=[idx_spec, pl.BlockSpec((nsel, tm), lambda i: (0, i)), tab_spec, _full(eexp.shape)],
        out_specs=pl.BlockSpec((tm, SUBLANES, LANES), lambda i: (i, 0, 0)),
        out_shape=jax.ShapeDtypeStruct((n, SUBLANES, LANES), F32),
        scratch_shapes=scratch,
        compiler_params=_params(("arbitrary",)),
        name="peer_v",
    )(idx, act, tab, eexp)


def _final_norm_kernel(h_ref, f_ref, g_ref, o_ref):
    o_ref[...] = _rmsnorm(h_ref[...] + f_ref[...], g_ref[...])


def _final_norm(h, f, g):
    n, d = h.shape
    rows = pl.BlockSpec((CHUNK, d), lambda i: (i, 0))
    return pl.pallas_call(
        _final_norm_kernel,
        grid=(n // CHUNK,),
        in_specs=[rows, rows, _full(g.shape)],
        out_specs=rows,
        out_shape=jax.ShapeDtypeStruct((n, d), F32),
        compiler_params=_params(("parallel",)),
        name="final_norm",
    )(h, f, g)


def _hi_lo(w):
    hi = w.astype(BF16)
    return hi, (w - hi.astype(F32)).astype(BF16)


def _block_ones(n, blk):
    i = np.arange(n)
    return jnp.asarray(i[:, None] // blk == i[None, :] // blk, dtype=BF16)


def _stream(x_rows, lead, conv0, ssd0, shift0, wkv0, w, *, nseq, nblk, lb, has_lead, npad, tb):
    d = x_rows.shape[1]
    nblk_all = nblk + (1 if has_lead else 0)
    width_a = w["ssm_norm"].shape[1]
    width_b = w["ones_b"].shape[0]
    heads_b = width_b // RWKV_HEAD_DIM
    if lb == CHUNK:
        xbc, z, dtp, rw = _proj_in(x_rows, lead, w["norm_mix"], w["w_in"], w["widths"],
                                   nseq=nseq, nblk=nblk_all, has_lead=has_lead)
        nchunk = nblk_all
    else:
        xbc, z, dtp, rw = _proj_in(x_rows, lead, w["norm_mix"], w["w_in"], w["widths"],
                                   nseq=1, nblk=x_rows.shape[0] // CHUNK, has_lead=False)
        nchunk = 1
    ya, conv_new, ssd_new = _ssd(xbc, z, dtp, conv0, ssd0, w["ssd"], nseq=nseq, lb=lb, nchunk=nchunk, npad=npad)
    r, dcy, kp, kn, b, v, g, bonus, shift_new = _rwkv_pre(
        rw, shift0[:, None, :], w["rwkv_pre"], nseq=nseq, lb=lb, nchunk=nchunk,
        npad=npad if lb == CHUNK else 0, width=width_b)

    t_all = nchunk * lb
    t0 = npad if lb == CHUNK else 0
    t_real = t_all - t0
    pairs = nseq * heads_b
    hd = RWKV_HEAD_DIM
    if pairs >= LANES:
        ngroup, dup = pairs // LANES, 1
    else:
        ngroup, dup = 1, LANES // pairs
    seq_per = nseq // ngroup
    vr = hd // dup

    def to_scan_k(a):
        a = a.reshape(ngroup, seq_per, t_all, heads_b, hd)[:, :, t0:]
        a = a.transpose(0, 2, 4, 1, 3).reshape(ngroup, t_real, hd, seq_per * heads_b)
        return jnp.tile(a, (1, 1, 1, dup))

    def to_scan_v(a):
        a = a.reshape(ngroup, seq_per, t_all, heads_b, dup, vr)[:, :, t0:]
        return a.transpose(0, 2, 5, 4, 1, 3).reshape(ngroup, t_real, vr, LANES)

    s0 = wkv0.reshape(ngroup, seq_per, heads_b, dup, vr, hd).transpose(0, 5, 4, 3, 1, 2)
    s0 = s0.reshape(ngroup, hd, vr, LANES)
    ysc, s_new = _rwkv_scan(to_scan_k(kn), to_scan_k(dcy), to_scan_k(b), to_scan_k(kp), to_scan_k(r),
                            to_scan_v(v), s0, tb=tb)
    wkv_new = s_new.reshape(ngroup, hd, vr, dup, seq_per, heads_b).transpose(0, 4, 5, 3, 2, 1)
    wkv_new = wkv_new.reshape(nseq, heads_b, hd, hd)
    skip_t = t_real - nblk * lb if lb == CHUNK else 0
    ys = ysc[:, skip_t:].reshape(ngroup, t_real - skip_t, vr, dup, seq_per, heads_b)
    ys = ys.transpose(0, 4, 1, 5, 3, 2).reshape(nseq * (t_real - skip_t), width_b)

    if lb == CHUNK:
        h1 = _mix_out(ys, bonus, g, ya, x_rows, w["mix_out"], nseq=nseq, nblk=nblk, skip=nchunk - nblk)
    else:
        h1 = _mix_out(ys, bonus, g, ya, x_rows, w["mix_out"], nseq=1, nblk=x_rows.shape[0] // CHUNK, skip=0)

    u, idx, gate = _peer_route(h1, w["route"], w["cand_blocks"], SUBLANES // WORDS)
    n = u.shape[0]
    act = _peer_u(idx, u.reshape(n, SUBLANES, LANES), gate, w["tab_u"], w["eexp"], tm=w["peer_tm"])
    ffn = _peer_v(idx, act, w["tab_v"], w["eexp"], tm=w["peer_tm"])
    y = _final_norm(h1, ffn.reshape(n, d), w["norm_final"])
    return y, conv_new, ssd_new, shift_new[:, 0, :], wkv_new


def kernel(x_prompt, x_sample, state_conv, state_ssd, state_shift, state_wkv, meta_tokens, norm_mix, w_in, conv_w, conv_b, dt_bias, a_log, d_skip, ssm_norm, shift_mu, decay_w0, decay_w2, iclr_a0, iclr_a2, gate_g2, k_k, k_a, r_k, lnx_w, lnx_b, w_out, norm_ffn, w_query, sub_keys, expert_u, expert_v, norm_final):
    bp, seq_p, d = x_prompt.shape
    bs, seq_s, _ = x_sample.shape
    depth = w_in.shape[0]
    assert depth == 1 and seq_p % CHUNK == 0 and (bs * seq_s) % CHUNK == 0 and seq_s % SUBLANES == 0
    heads_a = state_ssd.shape[2]
    width_a = heads_a * SSD_HEAD_DIM
    conv_dim = state_conv.shape[3]
    rw_in = state_shift.shape[2]
    heads_b = state_wkv.shape[2]
    width_b = heads_b * RWKV_HEAD_DIM
    assert heads_a <= LANES and rw_in == 3 * width_b + DECAY_LORA + AAA_LORA + GATE_LORA

    wi = w_in[0]
    ssd_in = width_a + conv_dim + heads_a
    w_cat = jnp.concatenate([
        wi[:, width_a:width_a + conv_dim], wi[:, :width_a],
        jnp.pad(wi[:, width_a + conv_dim:ssd_in], ((0, 0), (0, LANES - heads_a))),
        wi[:, ssd_in:]], axis=1).astype(BF16)
    widths = (conv_dim, width_a, LANES, rw_in)
    pad_h = lambda a: jnp.pad(a.reshape(1, -1), ((0, 0), (0, LANES - heads_a)))
    i_l = np.arange(CHUNK)
    tri = jnp.asarray(i_l[:, None] >= i_l[None, :], dtype=BF16)
    ehead = jnp.asarray(np.arange(LANES)[:, None] == np.arange(width_a)[None, :] // SSD_HEAD_DIM, dtype=BF16)
    ecol = jnp.asarray(np.arange(LANES)[:, None] == np.arange(heads_a * CHUNK)[None, :] // CHUNK, dtype=BF16)
    ssd_consts = (conv_w[0], conv_b[0].reshape(1, -1), pad_h(dt_bias[0]), pad_h(a_log[0]),
                  jnp.repeat(d_skip[0], SSD_HEAD_DIM).reshape(1, -1), ssm_norm[0].reshape(1, -1), tri, ehead, ecol)
    ones_b = _block_ones(width_b, RWKV_HEAD_DIM)
    zero = jnp.zeros((DECAY_LORA, width_b), F32)
    w_wa = jnp.concatenate([jnp.concatenate([decay_w2[0], zero], axis=1),
                            jnp.concatenate([jnp.zeros((AAA_LORA, width_b), F32), iclr_a2[0]], axis=1)], axis=0)
    row = lambda a: a.reshape(1, -1)
    pre_consts = (row(shift_mu[0]), row(decay_w0[0]), row(iclr_a0[0]), *_hi_lo(w_wa), *_hi_lo(gate_g2[0]),
                  row(k_k[0]), row(k_a[0]), row(r_k[0]), ones_b)
    wo = w_out[0].astype(BF16)
    mix_consts = (row(lnx_w[0]), row(lnx_b[0]), ones_b, wo[:width_a], wo[width_a:])
    cand_blocks, cand_ids = _cand_blocks()
    keys = sub_keys[0].transpose(1, 0, 2, 3).reshape(2 * PEER_HEADS, N_KEYS, -1)
    route_consts = (row(norm_ffn[0]), *_hi_lo(w_query[0]), *_hi_lo(keys),
                    jnp.asarray(np.broadcast_to(cand_ids[:, None], (cand_ids.shape[0], CHUNK))))
    nsel = PEER_HEADS * PEER_TOPK
    lane8 = np.arange(nsel * SUBLANES)
    eexp = jnp.asarray(np.arange(nsel)[:, None] == lane8[None, :] // SUBLANES, dtype=BF16)
    w = dict(norm_mix=row(norm_mix[0]), w_in=w_cat, widths=widths, ssm_norm=row(ssm_norm[0]), ones_b=ones_b,
             ssd=ssd_consts, rwkv_pre=pre_consts, mix_out=mix_consts, route=route_consts,
             tab_u=_pack_table(expert_u[0]), tab_v=_pack_table(expert_v[0]), eexp=eexp, cand_blocks=cand_blocks,
             norm_final=row(norm_final), peer_tm=CHUNK)

    npad = CHUNK - N_META
    lead = jnp.concatenate([jnp.zeros((npad, d), F32), meta_tokens.astype(F32)], axis=0)
    zeros = lambda *s: jnp.zeros(s, F32)
    yp, cp, sp, shp, wp = _stream(
        x_prompt.reshape(bp * seq_p, d), lead,
        zeros(bp, CONV_W - 1, conv_dim), zeros(bp, heads_a, SSD_HEAD_DIM, SSD_STATE), zeros(bp, rw_in),
        zeros(bp, heads_b, RWKV_HEAD_DIM, RWKV_HEAD_DIM), w,
        nseq=bp, nblk=seq_p // CHUNK, lb=CHUNK, has_lead=True, npad=npad, tb=N_META)
    ys, cs, ss, shs, ws = _stream(
        x_sample.reshape(bs * seq_s, d), lead, state_conv[0], state_ssd[0], state_shift[0], state_wkv[0], w,
        nseq=bs, nblk=1, lb=seq_s, has_lead=False, npad=CHUNK - seq_s, tb=seq_s)
    return (yp.reshape(bp, seq_p, d), ys.reshape(bs, seq_s, d), cp[None], sp[None], shp[None], wp[None],
            cs[None], ss[None], shs[None], ws[None])
```

```python
import functools

import jax
import jax.numpy as jnp
import numpy as np
from jax import lax
from jax.experimental import pallas as pl
from jax.experimental.pallas import tpu as pltpu

F32 = jnp.float32
BF16 = jnp.bfloat16
I32 = jnp.int32

N_META = 16
SSD_HEAD_DIM = 64
SSD_GROUPS = 2
SSD_STATE = 128
CONV_W = 4
RWKV_HEAD_DIM = 64
DECAY_LORA = 64
AAA_LORA = 64
GATE_LORA = 128
PEER_HEADS = 8
N_KEYS = 128
PEER_TOPK = 16
RMS_EPS = 1e-6
GN_EPS = 64e-5

LANES = 128
SUBLANES = 8
CHUNK = 128
VMEM_LIMIT_BYTES = 56 * 1024 * 1024


def _full(shape):
    zeros = (0,) * len(shape)
    return pl.BlockSpec(shape, lambda *_: zeros)


def _params(semantics, vmem=VMEM_LIMIT_BYTES):
    return pltpu.CompilerParams(dimension_semantics=semantics, vmem_limit_bytes=vmem)


def _split2(x):
    hi = x.astype(BF16)
    lo = (x - hi.astype(F32)).astype(BF16)
    return hi, lo


def _split3(x):
    hi = x.astype(BF16)
    r = x - hi.astype(F32)
    mid = r.astype(BF16)
    lo = (r - mid.astype(F32)).astype(BF16)
    return hi, mid, lo


def _dot(a, b):
    return jnp.dot(a, b, preferred_element_type=F32)


def _dot_nt(a, b):
    return lax.dot_general(a, b, (((1,), (1,)), ((), ())), preferred_element_type=F32)


def _dot_sel(x, sel):
    hi, mid, lo = _split3(x)
    return _dot(hi, sel) + _dot(mid, sel) + _dot(lo, sel)


def _dot_hp(x, w_hi, w_lo):
    hi, lo = _split2(x)
    return _dot(hi, w_hi) + (_dot(lo, w_hi) + _dot(hi, w_lo))


def _dot_hp_nt(x, w_hi, w_lo):
    hi, lo = _split2(x)
    return _dot_nt(hi, w_hi) + (_dot_nt(lo, w_hi) + _dot_nt(hi, w_lo))


def _silu(x):
    return x * jax.nn.sigmoid(x)


def _softplus(x):
    return jnp.maximum(x, 0.0) + jnp.log1p(jnp.exp(-jnp.abs(x)))


def _rmsnorm(x, g):
    ms = jnp.mean(x * x, axis=-1, keepdims=True)
    return x * lax.rsqrt(ms + RMS_EPS) * g


def _proj_in_kernel(x_ref, lead_ref, g_ref, w_ref, xbc_ref, z_ref, dt_ref, rw_ref, *, widths, has_lead):
    x = x_ref[...]
    if has_lead:
        x = jnp.where(pl.program_id(1) == 0, lead_ref[...], x)
    u = _rmsnorm(x, g_ref[...]).astype(BF16)
    p = _dot(u, w_ref[...])
    off = 0
    for ref, w in zip((xbc_ref, z_ref, dt_ref, rw_ref), widths):
        ref[...] = p[:, off:off + w]
        off += w


def _proj_in(rows, lead, g, w_cat, widths, *, nseq, nblk, has_lead):
    d = rows.shape[1]
    nblk_in = nblk - 1 if has_lead else nblk
    shift = 1 if has_lead else 0
    x_map = lambda s, j: (s * nblk_in + jnp.maximum(j - shift, 0), 0)
    o_map = lambda s, j: (s * nblk + j, 0)
    return pl.pallas_call(
        functools.partial(_proj_in_kernel, widths=widths, has_lead=has_lead),
        grid=(nseq, nblk),
        in_specs=[pl.BlockSpec((CHUNK, d), x_map), _full(lead.shape), _full(g.shape), _full(w_cat.shape)],
        out_specs=[pl.BlockSpec((CHUNK, w), o_map) for w in widths],
        out_shape=[jax.ShapeDtypeStruct((nseq * nblk * CHUNK, w), F32) for w in widths],
        compiler_params=_params(("parallel", "parallel")),
        name="proj_in",
    )(rows, lead, g, w_cat)


def _cumsum_rows(x, tri_ref):
    hi, mid, lo = _split3(x)
    tri = tri_ref[...]
    return _dot(tri, hi) + _dot(tri, mid) + _dot(tri, lo)


def _ssd_kernel(xbc_ref, z_ref, dt_ref, cs_ref, h0_ref, cw_ref, cb_ref, dtb_ref, alog_ref,
                dskip_ref, norm_ref, tri_ref, ehead_ref, ecol_ref,
                y_ref, cs_out_ref, h_out_ref, xfull, dtile, *, lb, npad, heads, width):
    c = pl.program_id(1)
    nc = pl.num_programs(1)
    L = CHUNK
    prev = CONV_W - 1
    base = SUBLANES
    state = SSD_STATE
    hd = SSD_HEAD_DIM
    per_group = heads // SSD_GROUPS

    @pl.when(c == 0)
    def _():
        h_out_ref[...] = h0_ref[...]
        if lb == L:
            xfull[base - prev:base, :] = cs_ref[0]

    if lb == L:
        xfull[base:base + L, :] = xbc_ref[...]
        dt_raw = dt_ref[...]
    else:
        xfull[...] = jnp.zeros_like(xfull)
        xfull[base + L - lb - prev:base + L - lb, :] = cs_ref[0]
        xfull[base + L - lb:base + L, :] = xbc_ref[...]
        dtile[...] = jnp.zeros_like(dtile)
        dtile[L - lb:L, :] = dt_ref[...]
        dt_raw = dtile[...]

    conv = cb_ref[...] + xfull[base - prev:base - prev + L, :] * cw_ref[0:1, :]
    for k in range(1, CONV_W):
        conv = conv + xfull[base - prev + k:base - prev + k + L, :] * cw_ref[k:k + 1, :]
    cs_new = xfull[base + L - prev:base + L, :]

    @pl.when(c == nc - 1)
    def _():
        cs_out_ref[0] = cs_new

    xfull[base - prev:base, :] = cs_new

    xbc = _silu(conv)
    xs = xbc[:, :width]
    gn = SSD_GROUPS * state
    row = lax.broadcasted_iota(I32, (L, LANES), 0) + c * L
    dt = jnp.where(row >= npad, _softplus(dt_raw + dtb_ref[...]), 0.0)
    da = dt * (-jnp.exp(alog_ref[...]))
    acum = _cumsum_rows(da, tri_ref)
    acum_t = acum.T
    ehead = ehead_ref[...]
    acum_x = _dot_sel(acum, ehead)
    dt_x = _dot_sel(dt, ehead)
    last_x = acum_x[L - 1:L, :]
    xdt = xs * dt_x
    wend = xdt * jnp.exp(last_x - acum_x)
    colb = _dot_sel(acum, ecol_ref[...])
    causal = (lax.broadcasted_iota(I32, (L, L), 0) >= lax.broadcasted_iota(I32, (L, L), 1))
    lane = lax.broadcasted_iota(I32, (L, LANES), 1)

    y_diag = []
    y_off = []
    for g in range(SSD_GROUPS):
        bm = xbc[:, width + g * state:width + (g + 1) * state].astype(BF16)
        cm = xbc[:, width + gn + g * state:width + gn + (g + 1) * state].astype(BF16)
        cb = _dot_nt(cm, bm)
        h_prev = h_out_ref[0, g * per_group:(g + 1) * per_group].reshape(per_group * hd, state)
        y_off.append(_dot_nt(cm, h_prev.astype(BF16)))
        for pair in range(per_group // 2):
            h_a = g * per_group + 2 * pair
            xp = xdt[:, h_a * hd:(h_a + 2) * hd].astype(BF16)
            outs = []
            for h in (h_a, h_a + 1):
                seg = colb[:, h * L:(h + 1) * L] - acum_t[h:h + 1, :]
                dec = jnp.exp(jnp.where(causal, seg, -jnp.inf))
                outs.append(_dot((cb * dec).astype(BF16), xp))
            y_diag.append(jnp.where(lane < hd, outs[0], outs[1]))
        wg_t = wend[:, g * per_group * hd:(g + 1) * per_group * hd].T.astype(BF16)
        upd = _dot(wg_t, bm)
        for i in range(per_group):
            h = g * per_group + i
            cd = jnp.exp(acum_t[h:h + 1, L - 1:L])
            h_out_ref[0, h] = h_out_ref[0, h] * cd + upd[i * hd:(i + 1) * hd, :]
    y = (jnp.concatenate(y_diag, axis=1) + jnp.concatenate(y_off, axis=1) * jnp.exp(acum_x)
         + xs * dskip_ref[...])
    y = y[L - lb:, :] * _silu(z_ref[...])
    y_ref[...] = _rmsnorm(y, norm_ref[...])


def _ssd(xbc, z, dtp, cs, h0, consts, *, nseq, lb, nchunk, npad):
    heads = h0.shape[1]
    width = z.shape[1]
    conv_dim = xbc.shape[1]
    row_map = lambda s, c: (s * nchunk + c, 0)
    kernel = functools.partial(_ssd_kernel, lb=lb, npad=npad, heads=heads, width=width)
    return pl.pallas_call(
        kernel,
        grid=(nseq, nchunk),
        in_specs=[pl.BlockSpec((lb, conv_dim), row_map),
                  pl.BlockSpec((lb, width), row_map),
                  pl.BlockSpec((lb, LANES), row_map),
                  pl.BlockSpec((1,) + cs.shape[1:], lambda s, c: (s, 0, 0)),
                  pl.BlockSpec((1,) + h0.shape[1:], lambda s, c: (s, 0, 0, 0))]
                 + [_full(a.shape) for a in consts],
        out_specs=[pl.BlockSpec((lb, width), row_map),
                   pl.BlockSpec((1,) + cs.shape[1:], lambda s, c: (s, 0, 0)),
                   pl.BlockSpec((1,) + h0.shape[1:], lambda s, c: (s, 0, 0, 0))],
        out_shape=[jax.ShapeDtypeStruct((nseq * nchunk * lb, width), F32),
                   jax.ShapeDtypeStruct(cs.shape, F32),
                   jax.ShapeDtypeStruct(h0.shape, F32)],
        scratch_shapes=[pltpu.VMEM((SUBLANES + CHUNK, conv_dim), F32), pltpu.VMEM((CHUNK, LANES), F32)],
        compiler_params=_params(("parallel", "arbitrary")),
        name="ssd",
    )(xbc, z, dtp, cs, h0, *consts)


def _rwkv_pre_kernel(rw_ref, sh_ref, mu_ref, w0_ref, a0_ref, wa_hi_ref, wa_lo_ref, g2_hi_ref, g2_lo_ref,
                     kk_ref, ka_ref, rk_ref, ones_ref,
                     r_o, d_o, k_o, kn_o, b_o, v_o, g_o, bonus_o, sh_o, pfull, *, lb, npad, width):
    c = pl.program_id(1)
    nc = pl.num_programs(1)
    base = SUBLANES
    p = rw_ref[...]

    @pl.when(c == 0)
    def _():
        pfull[base - 1:base, :] = sh_ref[0]

    pfull[base:base + lb, :] = p
    prev = pfull[base - 1:base - 1 + lb, :]
    last = p[lb - 1:lb, :]
    pfull[base - 1:base, :] = last

    @pl.when(c == nc - 1)
    def _():
        sh_o[0] = last

    pm = p + (prev - p) * mu_ref[...]
    r = pm[:, :width]
    k = pm[:, width:2 * width]
    v = pm[:, 2 * width:3 * width]
    lora_in = pm[:, 3 * width:3 * width + DECAY_LORA + AAA_LORA]
    lane = lax.broadcasted_iota(I32, lora_in.shape, 1)
    lora_in = jnp.where(lane < DECAY_LORA, jnp.tanh(lora_in), lora_in)
    lora = _dot_hp(lora_in, wa_hi_ref[...], wa_lo_ref[...])
    gate_in = jax.nn.sigmoid(pm[:, 3 * width + DECAY_LORA + AAA_LORA:])
    g_o[...] = _dot_hp(gate_in, g2_hi_ref[...], g2_lo_ref[...])
    w = -_softplus(-(w0_ref[...] + lora[:, :width])) - 0.5
    row = lax.broadcasted_iota(I32, w.shape, 0) + c * lb
    d_o[...] = jnp.where(row >= npad, jnp.exp(-jnp.exp(w)), 1.0)
    a = jax.nn.sigmoid(a0_ref[...] + lora[:, width:])
    ones = ones_ref[...]
    kn = k * kk_ref[...]
    kn = kn / jnp.maximum(jnp.sqrt(_dot_sel(kn * kn, ones)), 1e-12)
    kp = k * (1.0 + (a - 1.0) * ka_ref[...])
    r_o[...] = r
    k_o[...] = kp
    kn_o[...] = kn
    b_o[...] = kn * a
    v_o[...] = v
    bonus_o[...] = _dot_sel(r * kp * rk_ref[...], ones) * v


def _rwkv_pre(rw, sh, consts, *, nseq, lb, nchunk, npad, width):
    rw_in = rw.shape[1]
    row_map = lambda s, c: (s * nchunk + c, 0)
    n = nseq * nchunk * lb
    kernel = functools.partial(_rwkv_pre_kernel, lb=lb, npad=npad, width=width)
    return pl.pallas_call(
        kernel,
        grid=(nseq, nchunk),
        in_specs=[pl.BlockSpec((lb, rw_in), row_map), pl.BlockSpec((1, 1, rw_in), lambda s, c: (s, 0, 0))]
                 + [_full(a.shape) for a in consts],
        out_specs=[pl.BlockSpec((lb, width), row_map)] * 8 + [pl.BlockSpec((1, 1, rw_in), lambda s, c: (s, 0, 0))],
        out_shape=[jax.ShapeDtypeStruct((n, width), F32)] * 8 + [jax.ShapeDtypeStruct(sh.shape, F32)],
        scratch_shapes=[pltpu.VMEM((SUBLANES + lb, rw_in), F32)],
        compiler_params=_params(("parallel", "arbitrary")),
        name="rwkv_pre",
    )(rw, sh, *consts)


SCAN_PIECE = 32


def _rwkv_scan_kernel(kn_ref, d_ref, b_ref, k_ref, r_ref, v_ref, s0_ref, y_ref, s_ref, *, tb, hd):
    @pl.when(pl.program_id(1) == 0)
    def _():
        s_ref[...] = s0_ref[...]

    vr = v_ref.shape[2]

    def step(t, carry):
        for p in range(vr // SCAN_PIECE):
            rows = slice(p * SCAN_PIECE, (p + 1) * SCAN_PIECE)
            v_t = v_ref[0, t, rows, :]
            acc = [jnp.zeros_like(v_t), jnp.zeros_like(v_t)]
            for k in range(hd):
                acc[k % 2] = acc[k % 2] + s_ref[0, k, rows, :] * kn_ref[0, t, pl.ds(k, 1), :]
            sa = acc[0] + acc[1]
            acc = [jnp.zeros_like(v_t), jnp.zeros_like(v_t)]
            for k in range(hd):
                sk = (s_ref[0, k, rows, :] * d_ref[0, t, pl.ds(k, 1), :] - sa * b_ref[0, t, pl.ds(k, 1), :]
                      + v_t * k_ref[0, t, pl.ds(k, 1), :])
                s_ref[0, k, rows, :] = sk
                acc[k % 2] = acc[k % 2] + sk * r_ref[0, t, pl.ds(k, 1), :]
            y_ref[0, t, rows, :] = acc[0] + acc[1]
        return carry

    lax.fori_loop(0, tb, step, 0)


def _rwkv_scan(kn, d, b, k, r, v, s0, *, tb):
    ngroup, t_total, hd, _ = kn.shape
    vr = v.shape[2]
    kspec = pl.BlockSpec((1, tb, hd, LANES), lambda g, i: (g, i, 0, 0))
    vspec = pl.BlockSpec((1, tb, vr, LANES), lambda g, i: (g, i, 0, 0))
    sspec = pl.BlockSpec((1, hd, vr, LANES), lambda g, i: (g, 0, 0, 0))
    return pl.pallas_call(
        functools.partial(_rwkv_scan_kernel, tb=tb, hd=hd),
        grid=(ngroup, t_total // tb),
        in_specs=[kspec] * 5 + [vspec, sspec],
        out_specs=[vspec, sspec],
        out_shape=[jax.ShapeDtypeStruct(v.shape, F32), jax.ShapeDtypeStruct(s0.shape, F32)],
        compiler_params=_params(("parallel", "arbitrary")),
        name="rwkv_scan",
    )(kn, d, b, k, r, v, s0)


def _mix_out_kernel(ys_ref, bonus_ref, g_ref, ya_ref, h_ref, lnw_ref, lnb_ref, ones_ref, wa_ref, wb_ref, o_ref):
    ones = ones_ref[...]
    inv = 1.0 / RWKV_HEAD_DIM
    y = ys_ref[...]
    yc = y - _dot_sel(y, ones) * inv
    var = _dot_sel(yc * yc, ones) * inv
    yb = (yc * lax.rsqrt(var + GN_EPS) * lnw_ref[...] + lnb_ref[...] + bonus_ref[...]) * g_ref[...]
    o_ref[...] = (h_ref[...] + _dot(ya_ref[...].astype(BF16), wa_ref[...])
                  + _dot(yb.astype(BF16), wb_ref[...]))


def _mix_out(ys, bonus, g, ya, h, consts, *, nseq, nblk, skip):
    width = ys.shape[1]
    d = h.shape[1]
    cmap = lambda i: (i, 0)
    pmap = lambda i: ((i // nblk) * (nblk + skip) + skip + i % nblk, 0)
    wide = pl.BlockSpec((CHUNK, width), pmap)
    return pl.pallas_call(
        _mix_out_kernel,
        grid=(nseq * nblk,),
        in_specs=[pl.BlockSpec((CHUNK, width), cmap), wide, wide, wide, pl.BlockSpec((CHUNK, d), cmap)]
                 + [_full(a.shape) for a in consts],
        out_specs=pl.BlockSpec((CHUNK, d), cmap),
        out_shape=jax.ShapeDtypeStruct(h.shape, F32),
        compiler_params=_params(("parallel",)),
        name="mix_out",
    )(ys, bonus, g, ya, h, *consts)


def _cand_blocks():
    k = PEER_TOPK
    blocks = []
    a = 0
    while k // (a + 1) > 1:
        blocks.append((a, 1, -(-(k // (a + 1)) // SUBLANES) * SUBLANES))
        a += 1
    assert (k - a) % SUBLANES == 0
    blocks.append((a, k - a, 1))
    ids = np.concatenate([a0 * k + np.arange(nb) if na == 1 else (a0 + np.arange(na)) * k for a0, na, nb in blocks])
    return tuple(blocks), ids.astype(np.int32)


def _peer_route_kernel(h_ref, g_ref, wq_hi_ref, wq_lo_ref, k_hi_ref, k_lo_ref, cid_ref, u_o, idx_o, gate_o,
                       s_scr, v_scr, p_scr, c_scr, e_scr, ts_scr, ex_scr, *, blocks, idx_scale):
    u = _rmsnorm(h_ref[...], g_ref[...])
    u_o[...] = u
    q_hi, q_lo = _split2(_dot_hp(u, wq_hi_ref[...], wq_lo_ref[...]))
    tm = u.shape[0]
    k = PEER_TOPK
    nchain = 2 * PEER_HEADS
    for c in range(nchain):
        qh = q_hi[:, c * N_KEYS:(c + 1) * N_KEYS]
        ql = q_lo[:, c * N_KEYS:(c + 1) * N_KEYS]
        kh = k_hi_ref[c]
        s_scr[c] = _dot_nt(kh, qh) + (_dot_nt(k_lo_ref[c], qh) + _dot_nt(kh, ql))
    rid = lax.broadcasted_iota(I32, (N_KEYS, tm), 0)

    def top_keys(i, carry):
        for c in range(nchain):
            s = s_scr[c]
            m = jnp.max(s, axis=0, keepdims=True)
            p = jnp.min(jnp.where(s == m, rid, N_KEYS), axis=0, keepdims=True)
            v_scr[c, pl.ds(i, 1), :] = m
            p_scr[c, pl.ds(i, 1), :] = p
            s_scr[c] = jnp.where(rid == p, -jnp.inf, s)
        return carry

    lax.fori_loop(0, k, top_keys, 0)

    for h in range(PEER_HEADS):
        v1, v2 = v_scr[2 * h], v_scr[2 * h + 1]
        i1, i2 = p_scr[2 * h] * N_KEYS, p_scr[2 * h + 1]
        cand, eid = [], []
        for a0, na, nb in blocks:
            cand.append(v1[a0:a0 + na] + v2[0:nb])
            eid.append(i1[a0:a0 + na] + i2[0:nb])
        c_scr[h] = jnp.concatenate(cand, axis=0)
        e_scr[h] = jnp.concatenate(eid, axis=0)
    cid = cid_ref[...]
    ncand = cid.shape[0]

    def top_cands(i, carry):
        for h in range(PEER_HEADS):
            cand = c_scr[h]
            m = jnp.max(cand, axis=0, keepdims=True)
            c = jnp.min(jnp.where(cand == m, cid, k * k), axis=0, keepdims=True)
            hit = cid == c
            e = jnp.max(jnp.where(hit, e_scr[h], -1), axis=0, keepdims=True)
            c_scr[h] = jnp.where(hit, -jnp.inf, cand)
            ts_scr[pl.ds(h * k + i, 1), :] = m
            ex_scr[pl.ds(h * k + i, 1), :] = e * idx_scale
        return carry

    lax.fori_loop(0, k, top_cands, 0)
    idx_o[...] = ex_scr[...].T
    for h in range(PEER_HEADS):
        ts = ts_scr[h * k:(h + 1) * k, :]
        ex = jnp.exp(ts - jnp.max(ts, axis=0, keepdims=True))
        gate_o[h * k:(h + 1) * k, :] = ex / jnp.sum(ex, axis=0, keepdims=True)


def _peer_route(h, consts, blocks, idx_scale):
    n, d = h.shape
    tm = CHUNK
    nsel = PEER_HEADS * PEER_TOPK
    ncand = consts[-1].shape[0]
    rows = pl.BlockSpec((tm, d), lambda i: (i, 0))
    sel = pl.BlockSpec((nsel, tm), lambda i: (0, i))
    return pl.pallas_call(
        functools.partial(_peer_route_kernel, blocks=blocks, idx_scale=idx_scale),
        grid=(n // tm,),
        in_specs=[rows] + [_full(a.shape) for a in consts],
        out_specs=[rows, pl.BlockSpec((tm, nsel), lambda i: (i, 0)), sel],
        out_shape=[jax.ShapeDtypeStruct((n, d), F32), jax.ShapeDtypeStruct((n, nsel), I32),
                   jax.ShapeDtypeStruct((nsel, n), F32)],
        scratch_shapes=[pltpu.VMEM((2 * PEER_HEADS, N_KEYS, tm), F32),
                        pltpu.VMEM((2 * PEER_HEADS, PEER_TOPK, tm), F32),
                        pltpu.VMEM((2 * PEER_HEADS, PEER_TOPK, tm), I32),
                        pltpu.VMEM((PEER_HEADS, ncand, tm), F32),
                        pltpu.VMEM((PEER_HEADS, ncand, tm), I32),
                        pltpu.VMEM((nsel, tm), F32),
                        pltpu.VMEM((nsel, tm), I32)],
        compiler_params=_params(("parallel",)),
        name="peer_route",
    )(h, *consts)


PEER_GROUP = 8
PEER_SLOTS = 2
WORDS = 2


def _pack_kernel(t_ref, o_ref):
    r = t_ref.shape[0]
    o_ref[...] = pltpu.bitcast(t_ref[...].reshape(r * SUBLANES, LANES), I32)


def _pack_table(t):
    n, d = t.shape
    rows = d // LANES
    blk = 512
    t3 = t.astype(BF16).reshape(n, rows, LANES)
    return pl.pallas_call(
        _pack_kernel,
        grid=(n // blk,),
        in_specs=[pl.BlockSpec((blk, rows, LANES), lambda i: (i, 0, 0))],
        out_specs=pl.BlockSpec((blk * rows // WORDS, LANES), lambda i: (i, 0)),
        out_shape=jax.ShapeDtypeStruct((n * rows // WORDS, LANES), I32),
        compiler_params=_params(("parallel",)),
        name="pack_table",
    )(t3)


def _for_each_token(idx_ref, ids, sem, tm, token_fn):
    ngrp = tm // PEER_GROUP

    def copy(grp, slot):
        return pltpu.make_async_copy(idx_ref.at[pl.ds(grp * PEER_GROUP, PEER_GROUP)], ids.at[slot], sem.at[slot])

    for slot in range(PEER_SLOTS):
        copy(slot, slot).start()

    def body(q, carry):
        for slot in range(PEER_SLOTS):
            grp = q * PEER_SLOTS + slot
            copy(grp, slot).wait()
            for s in range(PEER_GROUP):
                token_fn(grp * PEER_GROUP + s, slot, s)

            @pl.when(grp + PEER_SLOTS < ngrp)
            def _():
                copy(grp + PEER_SLOTS, slot).start()
        return carry

    lax.fori_loop(0, ngrp // PEER_SLOTS, body, 0)


def _gather_rows(ids, tab_ref, gbuf, slot, s, nsel):
    rows_per = SUBLANES // WORDS
    for j in range(nsel):
        e = pl.multiple_of(ids[slot, s, j], rows_per)
        gbuf[s, j * rows_per:(j + 1) * rows_per, :] = tab_ref[pl.ds(e, rows_per), :]
    return pltpu.bitcast(gbuf[s], BF16)


def _diag_mask(nsel):
    sub = lax.broadcasted_iota(I32, (SUBLANES, nsel * SUBLANES), 0)
    lane = lax.broadcasted_iota(I32, (SUBLANES, nsel * SUBLANES), 1)
    return sub == lane % SUBLANES


def _gelu(x):
    return 0.5 * x * (1.0 + lax.erf(x * np.float32(1.0 / np.sqrt(2.0))))


def _peer_u_kernel(idx_ref, tok_ref, gate_ref, tab_ref, eexp_ref, act_o, gbuf, rsum, ids, sem, *, tm, nsel):
    diag = _diag_mask(nsel)

    def token(i, slot, s):
        rows = _gather_rows(ids, tab_ref, gbuf, slot, s, nsel)
        tok = tok_ref[i].astype(BF16)
        prod = _dot_nt(tok, rows)
        rsum[pl.ds(i, 1), :] = jnp.sum(jnp.where(diag, prod, 0.0), axis=0, keepdims=True)

    _for_each_token(idx_ref, ids, sem, tm, token)
    eexp = eexp_ref[...]
    hi, mid, lo = _split3(rsum[...])
    pre_t = _dot_nt(eexp, hi) + _dot_nt(eexp, mid) + _dot_nt(eexp, lo)
    act_o[...] = _gelu(pre_t) * gate_ref[...]


def _peer_v_kernel(idx_ref, act_ref, tab_ref, eexp_ref, out_o, gbuf, arep, ids, sem, *, tm, nsel):
    diag = _diag_mask(nsel)
    arep[...] = _dot(act_ref[...].T.astype(BF16), eexp_ref[...])

    def token(i, slot, s):
        rows = _gather_rows(ids, tab_ref, gbuf, slot, s, nsel)
        a = jnp.broadcast_to(arep[pl.ds(i, 1), :], diag.shape)
        out_o[i] = _dot(jnp.where(diag, a, 0.0).astype(BF16), rows)

    _for_each_token(idx_ref, ids, sem, tm, token)


def _peer_gather_specs(tm, nsel, tab):
    idx = pl.BlockSpec((tm, nsel), lambda i: (i, 0))
    table = pl.BlockSpec(tab.shape, lambda i: (0, 0), pipeline_mode=pl.Buffered(1))
    scratch = [pltpu.VMEM((PEER_GROUP, nsel * SUBLANES // WORDS, LANES), I32),
               pltpu.VMEM((tm, nsel * SUBLANES), F32),
               pltpu.SMEM((PEER_SLOTS, PEER_GROUP, nsel), I32),
               pltpu.SemaphoreType.DMA((PEER_SLOTS,))]
    return idx, table, scratch


def _peer_u(idx, tok3, gate, tab, eexp, *, tm):
    n, nsel = idx.shape
    assert tm % (PEER_GROUP * PEER_SLOTS) == 0
    idx_spec, tab_spec, scratch = _peer_gather_specs(tm, nsel, tab)
    sel = pl.BlockSpec((nsel, tm), lambda i: (0, i))
    return pl.pallas_call(
        functools.partial(_peer_u_kernel, tm=tm, nsel=nsel),
        grid=(n // tm,),
        in_specs=[idx_spec, pl.BlockSpec((tm, SUBLANES, LANES), lambda i: (i, 0, 0)), sel, tab_spec,
                  _full(eexp.shape)],
        out_specs=sel,
        out_shape=jax.ShapeDtypeStruct((nsel, n), F32),
        scratch_shapes=scratch,
        compiler_params=_params(("arbitrary",)),
        name="peer_u",
    )(idx, tok3, gate, tab, eexp)


def _peer_v(idx, act, tab, eexp, *, tm):
    n, nsel = idx.shape
    assert tm % (PEER_GROUP * PEER_SLOTS) == 0
    idx_spec, tab_spec, scratch = _peer_gather_specs(tm, nsel, tab)
    return pl.pallas_call(
        functools.partial(_peer_v_kernel, tm=tm, nsel=nsel),
        grid=(n // tm,),
        in_specs=[idx_spec, pl.BlockSpec((nsel, tm), lambda i: (0, i)), tab_spec, _full(eexp.shape)],
        out_specs=pl.BlockSpec((tm, SUBLANES, LANES), lambda i: (i, 0, 0)),
        out_shape=jax.ShapeDtypeStruct((n, SUBLANES, LANES), F32),
        scratch_shapes=scratch,
        compiler_params=_params(("arbitrary",)),
        name="peer_v",
    )(idx, act, tab, eexp)


def _final_norm_kernel(h_ref, f_ref, g_ref, o_ref):
    o_ref[...] = _rmsnorm(h_ref[...] + f_ref[...], g_ref[...])


def _final_norm(h, f, g):
    n, d = h.shape
    rows = pl.BlockSpec((CHUNK, d), lambda i: (i, 0))
    return pl.pallas_call(
        _final_norm_kernel,
        grid=(n // CHUNK,),
        in_specs=[rows, rows, _full(g.shape)],
        out_specs=rows,
        out_shape=jax.ShapeDtypeStruct((n, d), F32),
        compiler_params=_params(("parallel",)),
        name="final_norm",
    )(h, f, g)


def _hi_lo(w):
    hi = w.astype(BF16)
    return hi, (w - hi.astype(F32)).astype(BF16)


def _block_ones(n, blk):
    i = np.arange(n)
    return jnp.asarray(i[:, None] // blk == i[None, :] // blk, dtype=BF16)


def _stream(x_rows, lead, conv0, ssd0, shift0, wkv0, w, *, nseq, nblk, lb, has_lead, npad, tb):
    d = x_rows.shape[1]
    nblk_all = nblk + (1 if has_lead else 0)
    width_a = w["ssm_norm"].shape[1]
    width_b = w["ones_b"].shape[0]
    heads_b = width_b // RWKV_HEAD_DIM
    if lb == CHUNK:
        xbc, z, dtp, rw = _proj_in(x_rows, lead, w["norm_mix"], w["w_in"], w["widths"],
                                   nseq=nseq, nblk=nblk_all, has_lead=has_lead)
        nchunk = nblk_all
    else:
        xbc, z, dtp, rw = _proj_in(x_rows, lead, w["norm_mix"], w["w_in"], w["widths"],
                                   nseq=1, nblk=x_rows.shape[0] // CHUNK, has_lead=False)
        nchunk = 1
    ya, conv_new, ssd_new = _ssd(xbc, z, dtp, conv0, ssd0, w["ssd"], nseq=nseq, lb=lb, nchunk=nchunk, npad=npad)
    r, dcy, kp, kn, b, v, g, bonus, shift_new = _rwkv_pre(
        rw, shift0[:, None, :], w["rwkv_pre"], nseq=nseq, lb=lb, nchunk=nchunk,
        npad=npad if lb == CHUNK else 0, width=width_b)

    t_all = nchunk * lb
    t0 = npad if lb == CHUNK else 0
    t_real = t_all - t0
    pairs = nseq * heads_b
    hd = RWKV_HEAD_DIM
    if pairs >= LANES:
        ngroup, dup = pairs // LANES, 1
    else:
        ngroup, dup = 1, LANES // pairs
    seq_per = nseq // ngroup
    vr = hd // dup

    def to_scan_k(a):
        a = a.reshape(ngroup, seq_per, t_all, heads_b, hd)[:, :, t0:]
        a = a.transpose(0, 2, 4, 1, 3).reshape(ngroup, t_real, hd, seq_per * heads_b)
        return jnp.tile(a, (1, 1, 1, dup))

    def to_scan_v(a):
        a = a.reshape(ngroup, seq_per, t_all, heads_b, dup, vr)[:, :, t0:]
        return a.transpose(0, 2, 5, 4, 1, 3).reshape(ngroup, t_real, vr, LANES)

    s0 = wkv0.reshape(ngroup, seq_per, heads_b, dup, vr, hd).transpose(0, 5, 4, 3, 1, 2)
    s0 = s0.reshape(ngroup, hd, vr, LANES)
    ysc, s_new = _rwkv_scan(to_scan_k(kn), to_scan_k(dcy), to_scan_k(b), to_scan_k(kp), to_scan_k(r),
                            to_scan_v(v), s0, tb=tb)
    wkv_new = s_new.reshape(ngroup, hd, vr, dup, seq_per, heads_b).transpose(0, 4, 5, 3, 2, 1)
    wkv_new = wkv_new.reshape(nseq, heads_b, hd, hd)
    skip_t = t_real - nblk * lb if lb == CHUNK else 0
    ys = ysc[:, skip_t:].reshape(ngroup, t_real - skip_t, vr, dup, seq_per, heads_b)
    ys = ys.transpose(0, 4, 1, 5, 3, 2).reshape(nseq * (t_real - skip_t), width_b)

    if lb == CHUNK:
        h1 = _mix_out(ys, bonus, g, ya, x_rows, w["mix_out"], nseq=nseq, nblk=nblk, skip=nchunk - nblk)
    else:
        h1 = _mix_out(ys, bonus, g, ya, x_rows, w["mix_out"], nseq=1, nblk=x_rows.shape[0] // CHUNK, skip=0)

    u, idx, gate = _peer_route(h1, w["route"], w["cand_blocks"], SUBLANES // WORDS)
    n = u.shape[0]
    act = _peer_u(idx, u.reshape(n, SUBLANES, LANES), gate, w["tab_u"], w["eexp"], tm=w["peer_tm"])
    ffn = _peer_v(idx, act, w["tab_v"], w["eexp"], tm=w["peer_tm"])
    y = _final_norm(h1, ffn.reshape(n, d), w["norm_final"])
    return y, conv_new, ssd_new, shift_new[:, 0, :], wkv_new


def kernel(x_prompt, x_sample, state_conv, state_ssd, state_shift, state_wkv, meta_tokens, norm_mix, w_in, conv_w, conv_b, dt_bias, a_log, d_skip, ssm_norm, shift_mu, decay_w0, decay_w2, iclr_a0, iclr_a2, gate_g2, k_k, k_a, r_k, lnx_w, lnx_b, w_out, norm_ffn, w_query, sub_keys, expert_u, expert_v, norm_final):
    bp, seq_p, d = x_prompt.shape
    bs, seq_s, _ = x_sample.shape
    depth = w_in.shape[0]
    assert depth == 1 and seq_p % CHUNK == 0 and (bs * seq_s) % CHUNK == 0 and seq_s % SUBLANES == 0
    heads_a = state_ssd.shape[2]
    width_a = heads_a * SSD_HEAD_DIM
    conv_dim = state_conv.shape[3]
    rw_in = state_shift.shape[2]
    heads_b = state_wkv.shape[2]
    width_b = heads_b * RWKV_HEAD_DIM
    assert heads_a <= LANES and rw_in == 3 * width_b + DECAY_LORA + AAA_LORA + GATE_LORA

    wi = w_in[0]
    ssd_in = width_a + conv_dim + heads_a
    w_cat = jnp.concatenate([
        wi[:, width_a:width_a + conv_dim], wi[:, :width_a],
        jnp.pad(wi[:, width_a + conv_dim:ssd_in], ((0, 0), (0, LANES - heads_a))),
        wi[:, ssd_in:]], axis=1).astype(BF16)
    widths = (conv_dim, width_a, LANES, rw_in)
    pad_h = lambda a: jnp.pad(a.reshape(1, -1), ((0, 0), (0, LANES - heads_a)))
    i_l = np.arange(CHUNK)
    tri = jnp.asarray(i_l[:, None] >= i_l[None, :], dtype=BF16)
    ehead = jnp.asarray(np.arange(LANES)[:, None] == np.arange(width_a)[None, :] // SSD_HEAD_DIM, dtype=BF16)
    ecol = jnp.asarray(np.arange(LANES)[:, None] == np.arange(heads_a * CHUNK)[None, :] // CHUNK, dtype=BF16)
    ssd_consts = (conv_w[0], conv_b[0].reshape(1, -1), pad_h(dt_bias[0]), pad_h(a_log[0]),
                  jnp.repeat(d_skip[0], SSD_HEAD_DIM).reshape(1, -1), ssm_norm[0].reshape(1, -1), tri, ehead, ecol)
    ones_b = _block_ones(width_b, RWKV_HEAD_DIM)
    zero = jnp.zeros((DECAY_LORA, width_b), F32)
    w_wa = jnp.concatenate([jnp.concatenate([decay_w2[0], zero], axis=1),
                            jnp.concatenate([jnp.zeros((AAA_LORA, width_b), F32), iclr_a2[0]], axis=1)], axis=0)
    row = lambda a: a.reshape(1, -1)
    pre_consts = (row(shift_mu[0]), row(decay_w0[0]), row(iclr_a0[0]), *_hi_lo(w_wa), *_hi_lo(gate_g2[0]),
                  row(k_k[0]), row(k_a[0]), row(r_k[0]), ones_b)
    wo = w_out[0].astype(BF16)
    mix_consts = (row(lnx_w[0]), row(lnx_b[0]), ones_b, wo[:width_a], wo[width_a:])
    cand_blocks, cand_ids = _cand_blocks()
    keys = sub_keys[0].transpose(1, 0, 2, 3).reshape(2 * PEER_HEADS, N_KEYS, -1)
    route_consts = (row(norm_ffn[0]), *_hi_lo(w_query[0]), *_hi_lo(keys),
                    jnp.asarray(np.broadcast_to(cand_ids[:, None], (cand_ids.shape[0], CHUNK))))
    nsel = PEER_HEADS * PEER_TOPK
    lane8 = np.arange(nsel * SUBLANES)
    eexp = jnp.asarray(np.arange(nsel)[:, None] == lane8[None, :] // SUBLANES, dtype=BF16)
    w = dict(norm_mix=row(norm_mix[0]), w_in=w_cat, widths=widths, ssm_norm=row(ssm_norm[0]), ones_b=ones_b,
             ssd=ssd_consts, rwkv_pre=pre_consts, mix_out=mix_consts, route=route_consts,
             tab_u=_pack_table(expert_u[0]), tab_v=_pack_table(expert_v[0]), eexp=eexp, cand_blocks=cand_blocks,
             norm_final=row(norm_final), peer_tm=CHUNK)

    npad = CHUNK - N_META
    lead = jnp.concatenate([jnp.zeros((npad, d), F32), meta_tokens.astype(F32)], axis=0)
    zeros = lambda *s: jnp.zeros(s, F32)
    yp, cp, sp, shp, wp = _stream(
        x_prompt.reshape(bp * seq_p, d), lead,
        zeros(bp, CONV_W - 1, conv_dim), zeros(bp, heads_a, SSD_HEAD_DIM, SSD_STATE), zeros(bp, rw_in),
        zeros(bp, heads_b, RWKV_HEAD_DIM, RWKV_HEAD_DIM), w,
        nseq=bp, nblk=seq_p // CHUNK, lb=CHUNK, has_lead=True, npad=npad, tb=N_META)
    ys, cs, ss, shs, ws = _stream(
        x_sample.reshape(bs * seq_s, d), lead, state_conv[0], state_ssd[0], state_shift[0], state_wkv[0], w,
        nseq=bs, nblk=1, lb=seq_s, has_lead=False, npad=CHUNK - seq_s, tb=seq_s)
    return (yp.reshape(bp, seq_p, d), ys.reshape(bs, seq_s, d), cp[None], sp[None], shp[None], wp[None],
            cs[None], ss[None], shs[None], ws[None])
```

```python
import functools

import jax
import jax.numpy as jnp
import numpy as np
from jax import lax
from jax.experimental import pallas as pl
from jax.experimental.pallas import tpu as pltpu

F32 = jnp.float32
BF16 = jnp.bfloat16
I32 = jnp.int32

N_META = 16
SSD_HEAD_DIM = 64
SSD_GROUPS = 2
SSD_STATE = 128
CONV_W = 4
RWKV_HEAD_DIM = 64
DECAY_LORA = 64
AAA_LORA = 64
GATE_LORA = 128
PEER_HEADS = 8
N_KEYS = 128
PEER_TOPK = 16
RMS_EPS = 1e-6
GN_EPS = 64e-5

LANES = 128
SUBLANES = 8
CHUNK = 128
VMEM_LIMIT_BYTES = 56 * 1024 * 1024


def _full(shape):
    zeros = (0,) * len(shape)
    return pl.BlockSpec(shape, lambda *_: zeros)


def _params(semantics, vmem=VMEM_LIMIT_BYTES):
    return pltpu.CompilerParams(dimension_semantics=semantics, vmem_limit_bytes=vmem)


def _split2(x):
    hi = x.astype(BF16)
    lo = (x - hi.astype(F32)).astype(BF16)
    return hi, lo


def _split3(x):
    hi = x.astype(BF16)
    r = x - hi.astype(F32)
    mid = r.astype(BF16)
    lo = (r - mid.astype(F32)).astype(BF16)
    return hi, mid, lo


def _dot(a, b):
    return jnp.dot(a, b, preferred_element_type=F32)


def _dot_nt(a, b):
    return lax.dot_general(a, b, (((1,), (1,)), ((), ())), preferred_element_type=F32)


def _dot_sel(x, sel):
    hi, mid, lo = _split3(x)
    return _dot(hi, sel) + _dot(mid, sel) + _dot(lo, sel)


def _dot_hp(x, w_hi, w_lo):
    hi, lo = _split2(x)
    return _dot(hi, w_hi) + (_dot(lo, w_hi) + _dot(hi, w_lo))


def _dot_hp_nt(x, w_hi, w_lo):
    hi, lo = _split2(x)
    return _dot_nt(hi, w_hi) + (_dot_nt(lo, w_hi) + _dot_nt(hi, w_lo))


def _silu(x):
    return x * jax.nn.sigmoid(x)


def _softplus(x):
    return jnp.maximum(x, 0.0) + jnp.log1p(jnp.exp(-jnp.abs(x)))


def _rmsnorm(x, g):
    ms = jnp.mean(x * x, axis=-1, keepdims=True)
    return x * lax.rsqrt(ms + RMS_EPS) * g


def _proj_in_kernel(x_ref, lead_ref, g_ref, w_ref, xbc_ref, z_ref, dt_ref, rw_ref, *, widths, has_lead):
    x = x_ref[...]
    if has_lead:
        x = jnp.where(pl.program_id(1) == 0, lead_ref[...], x)
    u = _rmsnorm(x, g_ref[...]).astype(BF16)
    p = _dot(u, w_ref[...])
    off = 0
    for ref, w in zip((xbc_ref, z_ref, dt_ref, rw_ref), widths):
        ref[...] = p[:, off:off + w]
        off += w


def _proj_in(rows, lead, g, w_cat, widths, *, nseq, nblk, has_lead):
    d = rows.shape[1]
    nblk_in = nblk - 1 if has_lead else nblk
    shift = 1 if has_lead else 0
    x_map = lambda s, j: (s * nblk_in + jnp.maximum(j - shift, 0), 0)
    o_map = lambda s, j: (s * nblk + j, 0)
    return pl.pallas_call(
        functools.partial(_proj_in_kernel, widths=widths, has_lead=has_lead),
        grid=(nseq, nblk),
        in_specs=[pl.BlockSpec((CHUNK, d), x_map), _full(lead.shape), _full(g.shape), _full(w_cat.shape)],
        out_specs=[pl.BlockSpec((CHUNK, w), o_map) for w in widths],
        out_shape=[jax.ShapeDtypeStruct((nseq * nblk * CHUNK, w), F32) for w in widths],
        compiler_params=_params(("parallel", "parallel")),
        name="proj_in",
    )(rows, lead, g, w_cat)


def _cumsum_rows(x, tri_ref):
    hi, mid, lo = _split3(x)
    tri = tri_ref[...]
    return _dot(tri, hi) + _dot(tri, mid) + _dot(tri, lo)


def _ssd_kernel(xbc_ref, z_ref, dt_ref, cs_ref, h0_ref, cw_ref, cb_ref, dtb_ref, alog_ref,
                dskip_ref, norm_ref, tri_ref, ehead_ref, ecol_ref,
                y_ref, cs_out_ref, h_out_ref, xfull, dtile, *, lb, npad, heads, width):
    c = pl.program_id(1)
    nc = pl.num_programs(1)
    L = CHUNK
    prev = CONV_W - 1
    base = SUBLANES
    state = SSD_STATE
    hd = SSD_HEAD_DIM
    per_group = heads // SSD_GROUPS

    @pl.when(c == 0)
    def _():
        h_out_ref[...] = h0_ref[...]
        if lb == L:
            xfull[base - prev:base, :] = cs_ref[0]

    if lb == L:
        xfull[base:base + L, :] = xbc_ref[...]
        dt_raw = dt_ref[...]
    else:
        xfull[...] = jnp.zeros_like(xfull)
        xfull[base + L - lb - prev:base + L - lb, :] = cs_ref[0]
        xfull[base + L - lb:base + L, :] = xbc_ref[...]
        dtile[...] = jnp.zeros_like(dtile)
        dtile[L - lb:L, :] = dt_ref[...]
        dt_raw = dtile[...]

    conv = cb_ref[...] + xfull[base - prev:base - prev + L, :] * cw_ref[0:1, :]
    for k in range(1, CONV_W):
        conv = conv + xfull[base - prev + k:base - prev + k + L, :] * cw_ref[k:k + 1, :]
    cs_new = xfull[base + L - prev:base + L, :]

    @pl.when(c == nc - 1)
    def _():
        cs_out_ref[0] = cs_new

    xfull[base - prev:base, :] = cs_new

    xbc = _silu(conv)
    xs = xbc[:, :width]
    gn = SSD_GROUPS * state
    row = lax.broadcasted_iota(I32, (L, LANES), 0) + c * L
    dt = jnp.where(row >= npad, _softplus(dt_raw + dtb_ref[...]), 0.0)
    da = dt * (-jnp.exp(alog_ref[...]))
    acum = _cumsum_rows(da, tri_ref)
    acum_t = acum.T
    ehead = ehead_ref[...]
    acum_x = _dot_sel(acum, ehead)
    dt_x = _dot_sel(dt, ehead)
    last_x = acum_x[L - 1:L, :]
    xdt = xs * dt_x
    wend = xdt * jnp.exp(last_x - acum_x)
    colb = _dot_sel(acum, ecol_ref[...])
    causal = (lax.broadcasted_iota(I32, (L, L), 0) >= lax.broadcasted_iota(I32, (L, L), 1))
    lane = lax.broadcasted_iota(I32, (L, LANES), 1)

    y_diag = []
    y_off = []
    for g in range(SSD_GROUPS):
        bm = xbc[:, width + g * state:width + (g + 1) * state].astype(BF16)
        cm = xbc[:, width + gn + g * state:width + gn + (g + 1) * state].astype(BF16)
        cb = _dot_nt(cm, bm)
        h_prev = h_out_ref[0, g * per_group:(g + 1) * per_group].reshape(per_group * hd, state)
        y_off.append(_dot_nt(cm, h_prev.astype(BF16)))
        for pair in range(per_group // 2):
            h_a = g * per_group + 2 * pair
            xp = xdt[:, h_a * hd:(h_a + 2) * hd].astype(BF16)
            outs = []
            for h in (h_a, h_a + 1):
                seg = colb[:, h * L:(h + 1) * L] - acum_t[h:h + 1, :]
                dec = jnp.exp(jnp.where(causal, seg, -jnp.inf))
                outs.append(_dot((cb * dec).astype(BF16), xp))
            y_diag.append(jnp.where(lane < hd, outs[0], outs[1]))
        wg_t = wend[:, g * per_group * hd:(g + 1) * per_group * hd].T.astype(BF16)
        upd = _dot(wg_t, bm)
        for i in range(per_group):
            h = g * per_group + i
            cd = jnp.exp(acum_t[h:h + 1, L - 1:L])
            h_out_ref[0, h] = h_out_ref[0, h] * cd + upd[i * hd:(i + 1) * hd, :]
    y = (jnp.concatenate(y_diag, axis=1) + jnp.concatenate(y_off, axis=1) * jnp.exp(acum_x)
         + xs * dskip_ref[...])
    y = y[L - lb:, :] * _silu(z_ref[...])
    y_ref[...] = _rmsnorm(y, norm_ref[...])


def _ssd(xbc, z, dtp, cs, h0, consts, *, nseq, lb, nchunk, npad):
    heads = h0.shape[1]
    width = z.shape[1]
    conv_dim = xbc.shape[1]
    row_map = lambda s, c: (s * nchunk + c, 0)
    kernel = functools.partial(_ssd_kernel, lb=lb, npad=npad, heads=heads, width=width)
    return pl.pallas_call(
        kernel,
        grid=(nseq, nchunk),
        in_specs=[pl.BlockSpec((lb, conv_dim), row_map),
                  pl.BlockSpec((lb, width), row_map),
                  pl.BlockSpec((lb, LANES), row_map),
                  pl.BlockSpec((1,) + cs.shape[1:], lambda s, c: (s, 0, 0)),
                  pl.BlockSpec((1,) + h0.shape[1:], lambda s, c: (s, 0, 0, 0))]
                 + [_full(a.shape) for a in consts],
        out_specs=[pl.BlockSpec((lb, width), row_map),
                   pl.BlockSpec((1,) + cs.shape[1:], lambda s, c: (s, 0, 0)),
                   pl.BlockSpec((1,) + h0.shape[1:], lambda s, c: (s, 0, 0, 0))],
        out_shape=[jax.ShapeDtypeStruct((nseq * nchunk * lb, width), F32),
                   jax.ShapeDtypeStruct(cs.shape, F32),
                   jax.ShapeDtypeStruct(h0.shape, F32)],
        scratch_shapes=[pltpu.VMEM((SUBLANES + CHUNK, conv_dim), F32), pltpu.VMEM((CHUNK, LANES), F32)],
        compiler_params=_params(("parallel", "arbitrary")),
        name="ssd",
    )(xbc, z, dtp, cs, h0, *consts)


def _rwkv_pre_kernel(rw_ref, sh_ref, mu_ref, w0_ref, a0_ref, wa_hi_ref, wa_lo_ref, g2_hi_ref, g2_lo_ref,
                     kk_ref, ka_ref, rk_ref, ones_ref,
                     *refs, lb, npad, width, stacked):
    if stacked:
        sc_o, g_o, bonus_o, sh_o, pfull = refs
    else:
        kn_o, d_o, b_o, k_o, r_o, v_o, g_o, bonus_o, sh_o, pfull = refs
    c = pl.program_id(1)
    nc = pl.num_programs(1)
    base = SUBLANES
    p = rw_ref[...]

    @pl.when(c == 0)
    def _():
        pfull[base - 1:base, :] = sh_ref[0]

    pfull[base:base + lb, :] = p
    prev = pfull[base - 1:base - 1 + lb, :]
    last = p[lb - 1:lb, :]
    pfull[base - 1:base, :] = last

    @pl.when(c == nc - 1)
    def _():
        sh_o[0] = last

    pm = p + (prev - p) * mu_ref[...]
    r = pm[:, :width]
    k = pm[:, width:2 * width]
    v = pm[:, 2 * width:3 * width]
    lora_in = pm[:, 3 * width:3 * width + DECAY_LORA + AAA_LORA]
    lane = lax.broadcasted_iota(I32, lora_in.shape, 1)
    lora_in = jnp.where(lane < DECAY_LORA, jnp.tanh(lora_in), lora_in)
    lora = _dot_hp(lora_in, wa_hi_ref[...], wa_lo_ref[...])
    gate_in = jax.nn.sigmoid(pm[:, 3 * width + DECAY_LORA + AAA_LORA:])
    g_o[...] = _dot_hp(gate_in, g2_hi_ref[...], g2_lo_ref[...])
    w = -_softplus(-(w0_ref[...] + lora[:, :width])) - 0.5
    row = lax.broadcasted_iota(I32, w.shape, 0) + c * lb
    dcy = jnp.where(row >= npad, jnp.exp(-jnp.exp(w)), 1.0)
    a = jax.nn.sigmoid(a0_ref[...] + lora[:, width:])
    ones = ones_ref[...]
    kn = k * kk_ref[...]
    kn = kn / jnp.maximum(jnp.sqrt(_dot_sel(kn * kn, ones)), 1e-12)
    kp = k * (1.0 + (a - 1.0) * ka_ref[...])
    bonus_o[...] = _dot_sel(r * kp * rk_ref[...], ones) * v
    scan_in = (kn, dcy, kn * a, kp, r, v)
    if stacked:
        for j, x in enumerate(scan_in):
            sc_o[j, 0] = x.T
    else:
        for ref, x in zip((kn_o, d_o, b_o, k_o, r_o, v_o), scan_in):
            ref[...] = x


def _rwkv_pre(rw, sh, consts, *, nseq, lb, nchunk, npad, width, stacked):
    rw_in = rw.shape[1]
    row_map = lambda s, c: (s * nchunk + c, 0)
    n = nseq * nchunk * lb
    rows = pl.BlockSpec((lb, width), row_map)
    if stacked:
        scan_specs = [pl.BlockSpec((6, 1, width, lb), lambda s, c: (0, s, 0, c))]
        scan_shapes = [jax.ShapeDtypeStruct((6, nseq, width, nchunk * lb), F32)]
    else:
        scan_specs = [rows] * 6
        scan_shapes = [jax.ShapeDtypeStruct((n, width), F32)] * 6
    kernel = functools.partial(_rwkv_pre_kernel, lb=lb, npad=npad, width=width, stacked=stacked)
    return pl.pallas_call(
        kernel,
        grid=(nseq, nchunk),
        in_specs=[pl.BlockSpec((lb, rw_in), row_map), pl.BlockSpec((1, 1, rw_in), lambda s, c: (s, 0, 0))]
                 + [_full(a.shape) for a in consts],
        out_specs=scan_specs + [rows, rows, pl.BlockSpec((1, 1, rw_in), lambda s, c: (s, 0, 0))],
        out_shape=scan_shapes + [jax.ShapeDtypeStruct((n, width), F32)] * 2 + [jax.ShapeDtypeStruct(sh.shape, F32)],
        scratch_shapes=[pltpu.VMEM((SUBLANES + lb, rw_in), F32)],
        compiler_params=_params(("parallel", "arbitrary")),
        name="rwkv_pre",
    )(rw, sh, *consts)


def _to_scan_k_kernel(x_ref, o_ref, *, nseq, heads, hd, dup):
    for k in range(hd):
        rows = jnp.concatenate([x_ref[0, s, pl.ds(k, heads, stride=hd), :] for s in range(nseq)], axis=0)
        o_ref[0, k] = jnp.concatenate([rows] * dup, axis=0).T


def _to_scan_v_kernel(x_ref, o_ref, *, nseq, heads, hd, dup):
    vr = hd // dup
    for v in range(vr):
        rows = jnp.concatenate([x_ref[0, s, pl.ds(p * vr + v, heads, stride=hd), :]
                                for p in range(dup) for s in range(nseq)], axis=0)
        o_ref[:, v, :] = rows.T


def _to_scan(sc, *, heads, dup):
    _, nseq, width, t_all = sc.shape
    hd = width // heads
    kw = dict(nseq=nseq, heads=heads, hd=hd, dup=dup)
    ks = pl.pallas_call(
        functools.partial(_to_scan_k_kernel, **kw),
        grid=(5, t_all // CHUNK),
        in_specs=[pl.BlockSpec((1, nseq, width, CHUNK), lambda j, c: (j, 0, 0, c))],
        out_specs=pl.BlockSpec((1, hd, CHUNK, LANES), lambda j, c: (j, 0, c, 0)),
        out_shape=jax.ShapeDtypeStruct((5, hd, t_all, LANES), F32),
        compiler_params=_params(("parallel", "parallel")),
        name="to_scan_k",
    )(sc)
    vs = pl.pallas_call(
        functools.partial(_to_scan_v_kernel, **kw),
        grid=(t_all // CHUNK,),
        in_specs=[pl.BlockSpec((1, nseq, width, CHUNK), lambda c: (5, 0, 0, c))],
        out_specs=pl.BlockSpec((CHUNK, hd // dup, LANES), lambda c: (c, 0, 0)),
        out_shape=jax.ShapeDtypeStruct((t_all, hd // dup, LANES), F32),
        compiler_params=_params(("parallel",)),
        name="to_scan_v",
    )(sc)
    return ks, vs


SCAN_PIECE = 32


def _rwkv_scan_kernel(kn_ref, d_ref, b_ref, k_ref, r_ref, v_ref, s0_ref, y_ref, s_ref, *, tb, hd, kmajor):
    @pl.when(pl.program_id(1) == 0)
    def _():
        s_ref[...] = s0_ref[...]

    vr = y_ref.shape[2]

    def krow(ref, t, k):
        return ref[0, k, pl.ds(t, 1), :] if kmajor else ref[0, t, pl.ds(k, 1), :]

    def step(t, carry):
        for p in range(vr // SCAN_PIECE):
            rows = slice(p * SCAN_PIECE, (p + 1) * SCAN_PIECE)
            v_t = v_ref[t, rows, :] if kmajor else v_ref[0, t, rows, :]
            acc = [jnp.zeros_like(v_t), jnp.zeros_like(v_t)]
            for k in range(hd):
                acc[k % 2] = acc[k % 2] + s_ref[0, k, rows, :] * krow(kn_ref, t, k)
            sa = acc[0] + acc[1]
            acc = [jnp.zeros_like(v_t), jnp.zeros_like(v_t)]
            for k in range(hd):
                sk = s_ref[0, k, rows, :] * krow(d_ref, t, k) - sa * krow(b_ref, t, k) + v_t * krow(k_ref, t, k)
                s_ref[0, k, rows, :] = sk
                acc[k % 2] = acc[k % 2] + sk * krow(r_ref, t, k)
            y_ref[0, t, rows, :] = acc[0] + acc[1]
        return carry

    lax.fori_loop(0, tb, step, 0)


def _scan_call(operands, in_specs, s0, *, ngroup, nsteps, tb, hd, vr, kmajor):
    vspec = pl.BlockSpec((1, tb, vr, LANES), lambda g, i: (g, i, 0, 0))
    sspec = pl.BlockSpec((1, hd, vr, LANES), lambda g, i: (g, 0, 0, 0))
    return pl.pallas_call(
        functools.partial(_rwkv_scan_kernel, tb=tb, hd=hd, kmajor=kmajor),
        grid=(ngroup, nsteps // tb),
        in_specs=in_specs + [sspec],
        out_specs=[vspec, sspec],
        out_shape=[jax.ShapeDtypeStruct((ngroup, nsteps, vr, LANES), F32), jax.ShapeDtypeStruct(s0.shape, F32)],
        compiler_params=_params(("parallel", "arbitrary")),
        name="rwkv_scan",
    )(*operands, s0)


def _rwkv_scan(kn, d, b, k, r, v, s0, *, tb):
    ngroup, t_total, hd, _ = kn.shape
    vr = v.shape[2]
    kspec = pl.BlockSpec((1, tb, hd, LANES), lambda g, i: (g, i, 0, 0))
    vspec = pl.BlockSpec((1, tb, vr, LANES), lambda g, i: (g, i, 0, 0))
    return _scan_call((kn, d, b, k, r, v), [kspec] * 5 + [vspec], s0, ngroup=ngroup, nsteps=t_total, tb=tb, hd=hd,
                      vr=vr, kmajor=False)


def _rwkv_scan_kmajor(ks, vs, s0, *, tb, t0):
    _, hd, t_all, _ = ks.shape
    vr = vs.shape[1]
    assert t0 % tb == 0
    off = t0 // tb
    kspecs = [pl.BlockSpec((1, hd, tb, LANES), lambda g, i, j=j: (j, 0, i + off, 0)) for j in range(5)]
    vspec = pl.BlockSpec((tb, vr, LANES), lambda g, i: (i + off, 0, 0))
    return _scan_call((ks,) * 5 + (vs,), kspecs + [vspec], s0, ngroup=1, nsteps=t_all - t0, tb=tb, hd=hd, vr=vr,
                      kmajor=True)


def _mix_out_kernel(ys_ref, bonus_ref, g_ref, ya_ref, h_ref, lnw_ref, lnb_ref, ones_ref, wa_ref, wb_ref, o_ref):
    ones = ones_ref[...]
    inv = 1.0 / RWKV_HEAD_DIM
    y = ys_ref[...]
    yc = y - _dot_sel(y, ones) * inv
    var = _dot_sel(yc * yc, ones) * inv
    yb = (yc * lax.rsqrt(var + GN_EPS) * lnw_ref[...] + lnb_ref[...] + bonus_ref[...]) * g_ref[...]
    o_ref[...] = (h_ref[...] + _dot(ya_ref[...].astype(BF16), wa_ref[...])
                  + _dot(yb.astype(BF16), wb_ref[...]))


def _mix_out(ys, bonus, g, ya, h, consts, *, nseq, nblk, skip):
    width = ys.shape[1]
    d = h.shape[1]
    cmap = lambda i: (i, 0)
    pmap = lambda i: ((i // nblk) * (nblk + skip) + skip + i % nblk, 0)
    wide = pl.BlockSpec((CHUNK, width), pmap)
    return pl.pallas_call(
        _mix_out_kernel,
        grid=(nseq * nblk,),
        in_specs=[pl.BlockSpec((CHUNK, width), cmap), wide, wide, wide, pl.BlockSpec((CHUNK, d), cmap)]
                 + [_full(a.shape) for a in consts],
        out_specs=pl.BlockSpec((CHUNK, d), cmap),
        out_shape=jax.ShapeDtypeStruct(h.shape, F32),
        compiler_params=_params(("parallel",)),
        name="mix_out",
    )(ys, bonus, g, ya, h, *consts)


def _cand_blocks():
    k = PEER_TOPK
    blocks = []
    a = 0
    while k // (a + 1) > 1:
        blocks.append((a, 1, -(-(k // (a + 1)) // SUBLANES) * SUBLANES))
        a += 1
    assert (k - a) % SUBLANES == 0
    blocks.append((a, k - a, 1))
    ids = np.concatenate([a0 * k + np.arange(nb) if na == 1 else (a0 + np.arange(na)) * k for a0, na, nb in blocks])
    return tuple(blocks), ids.astype(np.int32)


def _peer_route_kernel(h_ref, g_ref, wq_hi_ref, wq_lo_ref, k_hi_ref, k_lo_ref, cid_ref, u_o, idx_o, gate_o,
                       s_scr, v_scr, p_scr, c_scr, e_scr, ts_scr, ex_scr, *, blocks, idx_scale):
    u = _rmsnorm(h_ref[...], g_ref[...])
    u_o[...] = u
    q_hi, q_lo = _split2(_dot_hp(u, wq_hi_ref[...], wq_lo_ref[...]))
    tm = u.shape[0]
    k = PEER_TOPK
    nchain = 2 * PEER_HEADS
    for c in range(nchain):
        qh = q_hi[:, c * N_KEYS:(c + 1) * N_KEYS]
        ql = q_lo[:, c * N_KEYS:(c + 1) * N_KEYS]
        kh = k_hi_ref[c]
        s_scr[c] = _dot_nt(kh, qh) + (_dot_nt(k_lo_ref[c], qh) + _dot_nt(kh, ql))
    rid = lax.broadcasted_iota(I32, (N_KEYS, tm), 0)

    def top_keys(i, carry):
        for c in range(nchain):
            s = s_scr[c]
            m = jnp.max(s, axis=0, keepdims=True)
            p = jnp.min(jnp.where(s == m, rid, N_KEYS), axis=0, keepdims=True)
            v_scr[c, pl.ds(i, 1), :] = m
            p_scr[c, pl.ds(i, 1), :] = p
            s_scr[c] = jnp.where(rid == p, -jnp.inf, s)
        return carry

    lax.fori_loop(0, k, top_keys, 0)

    for h in range(PEER_HEADS):
        v1, v2 = v_scr[2 * h], v_scr[2 * h + 1]
        i1, i2 = p_scr[2 * h] * N_KEYS, p_scr[2 * h + 1]
        cand, eid = [], []
        for a0, na, nb in blocks:
            cand.append(v1[a0:a0 + na] + v2[0:nb])
            eid.append(i1[a0:a0 + na] + i2[0:nb])
        c_scr[h] = jnp.concatenate(cand, axis=0)
        e_scr[h] = jnp.concatenate(eid, axis=0)
    cid = cid_ref[...]
    ncand = cid.shape[0]

    def top_cands(i, carry):
        for h in range(PEER_HEADS):
            cand = c_scr[h]
            m = jnp.max(cand, axis=0, keepdims=True)
            c = jnp.min(jnp.where(cand == m, cid, k * k), axis=0, keepdims=True)
            hit = cid == c
            e = jnp.max(jnp.where(hit, e_scr[h], -1), axis=0, keepdims=True)
            c_scr[h] = jnp.where(hit, -jnp.inf, cand)
            ts_scr[pl.ds(h * k + i, 1), :] = m
            ex_scr[pl.ds(h * k + i, 1), :] = e * idx_scale
        return carry

    lax.fori_loop(0, k, top_cands, 0)
    idx_o[...] = ex_scr[...].T
    for h in range(PEER_HEADS):
        ts = ts_scr[h * k:(h + 1) * k, :]
        ex = jnp.exp(ts - jnp.max(ts, axis=0, keepdims=True))
        gate_o[h * k:(h + 1) * k, :] = ex / jnp.sum(ex, axis=0, keepdims=True)


def _peer_route(h, consts, blocks, idx_scale):
    n, d = h.shape
    tm = CHUNK
    nsel = PEER_HEADS * PEER_TOPK
    ncand = consts[-1].shape[0]
    rows = pl.BlockSpec((tm, d), lambda i: (i, 0))
    sel = pl.BlockSpec((nsel, tm), lambda i: (0, i))
    return pl.pallas_call(
        functools.partial(_peer_route_kernel, blocks=blocks, idx_scale=idx_scale),
        grid=(n // tm,),
        in_specs=[rows] + [_full(a.shape) for a in consts],
        out_specs=[rows, pl.BlockSpec((tm, nsel), lambda i: (i, 0)), sel],
        out_shape=[jax.ShapeDtypeStruct((n, d), F32), jax.ShapeDtypeStruct((n, nsel), I32),
                   jax.ShapeDtypeStruct((nsel, n), F32)],
        scratch_shapes=[pltpu.VMEM((2 * PEER_HEADS, N_KEYS, tm), F32),
                        pltpu.VMEM((2 * PEER_HEADS, PEER_TOPK, tm), F32),
                        pltpu.VMEM((2 * PEER_HEADS, PEER_TOPK, tm), I32),
                        pltpu.VMEM((PEER_HEADS, ncand, tm), F32),
                        pltpu.VMEM((PEER_HEADS, ncand, tm), I32),
                        pltpu.VMEM((nsel, tm), F32),
                        pltpu.VMEM((nsel, tm), I32)],
        compiler_params=_params(("parallel",)),
        name="peer_route",
    )(h, *consts)


PEER_GROUP = 8
PEER_SLOTS = 2
WORDS = 2


def _pack_kernel(t_ref, o_ref):
    r = t_ref.shape[0]
    o_ref[...] = pltpu.bitcast(t_ref[...].reshape(r * SUBLANES, LANES), I32)


def _pack_table(t):
    n, d = t.shape
    rows = d // LANES
    blk = 512
    t3 = t.astype(BF16).reshape(n, rows, LANES)
    return pl.pallas_call(
        _pack_kernel,
        grid=(n // blk,),
        in_specs=[pl.BlockSpec((blk, rows, LANES), lambda i: (i, 0, 0))],
        out_specs=pl.BlockSpec((blk * rows // WORDS, LANES), lambda i: (i, 0)),
        out_shape=jax.ShapeDtypeStruct((n * rows // WORDS, LANES), I32),
        compiler_params=_params(("parallel",)),
        name="pack_table",
    )(t3)


def _for_each_token(idx_ref, ids, sem, tm, token_fn):
    ngrp = tm // PEER_GROUP

    def copy(grp, slot):
        return pltpu.make_async_copy(idx_ref.at[pl.ds(grp * PEER_GROUP, PEER_GROUP)], ids.at[slot], sem.at[slot])

    for slot in range(PEER_SLOTS):
        copy(slot, slot).start()

    def body(q, carry):
        for slot in range(PEER_SLOTS):
            grp = q * PEER_SLOTS + slot
            copy(grp, slot).wait()
            for s in range(PEER_GROUP):
                token_fn(grp * PEER_GROUP + s, slot, s)

            @pl.when(grp + PEER_SLOTS < ngrp)
            def _():
                copy(grp + PEER_SLOTS, slot).start()
        return carry

    lax.fori_loop(0, ngrp // PEER_SLOTS, body, 0)


def _gather_rows(ids, tab_ref, gbuf, slot, s, nsel):
    rows_per = SUBLANES // WORDS
    for j in range(nsel):
        e = pl.multiple_of(ids[slot, s, j], rows_per)
        gbuf[s, j * rows_per:(j + 1) * rows_per, :] = tab_ref[pl.ds(e, rows_per), :]
    return pltpu.bitcast(gbuf[s], BF16)


def _diag_mask(nsel):
    sub = lax.broadcasted_iota(I32, (SUBLANES, nsel * SUBLANES), 0)
    lane = lax.broadcasted_iota(I32, (SUBLANES, nsel * SUBLANES), 1)
    return sub == lane % SUBLANES


def _gelu(x):
    return 0.5 * x * (1.0 + lax.erf(x * np.float32(1.0 / np.sqrt(2.0))))


def _peer_u_kernel(idx_ref, tok_ref, gate_ref, tab_ref, eexp_ref, act_o, gbuf, rsum, ids, sem, *, tm, nsel):
    diag = _diag_mask(nsel)

    def token(i, slot, s):
        rows = _gather_rows(ids, tab_ref, gbuf, slot, s, nsel)
        tok = tok_ref[i].astype(BF16)
        prod = _dot_nt(tok, rows)
        rsum[pl.ds(i, 1), :] = jnp.sum(jnp.where(diag, prod, 0.0), axis=0, keepdims=True)

    _for_each_token(idx_ref, ids, sem, tm, token)
    eexp = eexp_ref[...]
    hi, mid, lo = _split3(rsum[...])
    pre_t = _dot_nt(eexp, hi) + _dot_nt(eexp, mid) + _dot_nt(eexp, lo)
    act_o[...] = _gelu(pre_t) * gate_ref[...]


def _peer_v_kernel(idx_ref, act_ref, tab_ref, eexp_ref, out_o, gbuf, arep, ids, sem, *, tm, nsel):
    diag = _diag_mask(nsel)
    arep[...] = _dot(act_ref[...].T.astype(BF16), eexp_ref[...])

    def token(i, slot, s):
        rows = _gather_rows(ids, tab_ref, gbuf, slot, s, nsel)
        a = jnp.broadcast_to(arep[pl.ds(i, 1), :], diag.shape)
        out_o[i] = _dot(jnp.where(diag, a, 0.0).astype(BF16), rows)

    _for_each_token(idx_ref, ids, sem, tm, token)


def _peer_gather_specs(tm, nsel, tab):
    idx = pl.BlockSpec((tm, nsel), lambda i: (i, 0))
    table = pl.BlockSpec(tab.shape, lambda i: (0, 0), pipeline_mode=pl.Buffered(1))
    scratch = [pltpu.VMEM((PEER_GROUP, nsel * SUBLANES // WORDS, LANES), I32),
               pltpu.VMEM((tm, nsel * SUBLANES), F32),
               pltpu.SMEM((PEER_SLOTS, PEER_GROUP, nsel), I32),
               pltpu.SemaphoreType.DMA((PEER_SLOTS,))]
    return idx, table, scratch


def _peer_u(idx, tok3, gate, tab, eexp, *, tm):
    n, nsel = idx.shape
    assert tm % (PEER_GROUP * PEER_SLOTS) == 0
    idx_spec, tab_spec, scratch = _peer_gather_specs(tm, nsel, tab)
    sel = pl.BlockSpec((nsel, tm), lambda i: (0, i))
    return pl.pallas_call(
        functools.partial(_peer_u_kernel, tm=tm, nsel=nsel),
        grid=(n // tm,),
        in_specs=[idx_spec, pl.BlockSpec((tm, SUBLANES, LANES), lambda i: (i, 0, 0)), sel, tab_spec,
                  _full(eexp.shape)],
        out_specs=sel,
        out_shape=jax.ShapeDtypeStruct((nsel, n), F32),
        scratch_shapes=scratch,
        compiler_params=_params(("arbitrary",)),
        name="peer_u",
    )(idx, tok3, gate, tab, eexp)


def _peer_v(idx, act, tab, eexp, *, tm):
    n, nsel = idx.shape
    assert tm % (PEER_GROUP * PEER_SLOTS) == 0
    idx_spec, tab_spec, scratch = _peer_gather_specs(tm, nsel, tab)
    return pl.pallas_call(
        functools.partial(_peer_v_kernel, tm=tm, nsel=nsel),
        grid=(n // tm,),
        in_specs=[idx_spec, pl.BlockSpec((nsel, tm), lambda i: (0, i)), tab_spec, _full(eexp.shape)],
        out_specs=pl.BlockSpec((tm, SUBLANES, LANES), lambda i: (i, 0, 0)),
        out_shape=jax.ShapeDtypeStruct((n, SUBLANES, LANES), F32),
        scratch_shapes=scratch,
        compiler_params=_params(("arbitrary",)),
        name="peer_v",
    )(idx, act, tab, eexp)


def _final_norm_kernel(h_ref, f_ref, g_ref, o_ref):
    o_ref[...] = _rmsnorm(h_ref[...] + f_ref[...], g_ref[...])


def _final_norm(h, f, g):
    n, d = h.shape
    rows = pl.BlockSpec((CHUNK, d), lambda i: (i, 0))
    return pl.pallas_call(
        _final_norm_kernel,
        grid=(n // CHUNK,),
        in_specs=[rows, rows, _full(g.shape)],
        out_specs=rows,
        out_shape=jax.ShapeDtypeStruct((n, d), F32),
        compiler_params=_params(("parallel",)),
        name="final_norm",
    )(h, f, g)


def _hi_lo(w):
    hi = w.astype(BF16)
    return hi, (w - hi.astype(F32)).astype(BF16)


def _block_ones(n, blk):
    i = np.arange(n)
    return jnp.asarray(i[:, None] // blk == i[None, :] // blk, dtype=BF16)


def _stream(x_rows, lead, conv0, ssd0, shift0, wkv0, w, *, nseq, nblk, lb, has_lead, npad, tb):
    d = x_rows.shape[1]
    nblk_all = nblk + (1 if has_lead else 0)
    width_a = w["ssm_norm"].shape[1]
    width_b = w["ones_b"].shape[0]
    heads_b = width_b // RWKV_HEAD_DIM
    if lb == CHUNK:
        xbc, z, dtp, rw = _proj_in(x_rows, lead, w["norm_mix"], w["w_in"], w["widths"],
                                   nseq=nseq, nblk=nblk_all, has_lead=has_lead)
        nchunk = nblk_all
    else:
        xbc, z, dtp, rw = _proj_in(x_rows, lead, w["norm_mix"], w["w_in"], w["widths"],
                                   nseq=1, nblk=x_rows.shape[0] // CHUNK, has_lead=False)
        nchunk = 1
    ya, conv_new, ssd_new = _ssd(xbc, z, dtp, conv0, ssd0, w["ssd"], nseq=nseq, lb=lb, nchunk=nchunk, npad=npad)
    scan_kernels = lb == CHUNK and heads_b == SUBLANES and 2 * nseq * heads_b == LANES
    *scan_in, g, bonus, shift_new = _rwkv_pre(
        rw, shift0[:, None, :], w["rwkv_pre"], nseq=nseq, lb=lb, nchunk=nchunk,
        npad=npad if lb == CHUNK else 0, width=width_b, stacked=scan_kernels)

    t_all = nchunk * lb
    t0 = npad if lb == CHUNK else 0
    t_real = t_all - t0
    pairs = nseq * heads_b
    hd = RWKV_HEAD_DIM
    if pairs >= LANES:
        ngroup, dup = pairs // LANES, 1
    else:
        ngroup, dup = 1, LANES // pairs
    seq_per = nseq // ngroup
    vr = hd // dup

    def to_scan_k(a):
        a = a.reshape(ngroup, seq_per, t_all, heads_b, hd)[:, :, t0:]
        a = a.transpose(0, 2, 4, 1, 3).reshape(ngroup, t_real, hd, seq_per * heads_b)
        return jnp.tile(a, (1, 1, 1, dup))

    def to_scan_v(a):
        a = a.reshape(ngroup, seq_per, t_all, heads_b, dup, vr)[:, :, t0:]
        return a.transpose(0, 2, 5, 4, 1, 3).reshape(ngroup, t_real, vr, LANES)

    s0 = wkv0.reshape(ngroup, seq_per, heads_b, dup, vr, hd).transpose(0, 5, 4, 3, 1, 2)
    s0 = s0.reshape(ngroup, hd, vr, LANES)
    if scan_kernels:
        ks, vs = _to_scan(scan_in[0], heads=heads_b, dup=dup)
        ysc, s_new = _rwkv_scan_kmajor(ks, vs, s0, tb=tb, t0=t0)
    else:
        ysc, s_new = _rwkv_scan(*[to_scan_k(a) for a in scan_in[:5]], to_scan_v(scan_in[5]), s0, tb=tb)
    wkv_new = s_new.reshape(ngroup, hd, vr, dup, seq_per, heads_b).transpose(0, 4, 5, 3, 2, 1)
    wkv_new = wkv_new.reshape(nseq, heads_b, hd, hd)
    skip_t = t_real - nblk * lb if lb == CHUNK else 0
    ys = ysc[:, skip_t:].reshape(ngroup, t_real - skip_t, vr, dup, seq_per, heads_b)
    ys = ys.transpose(0, 4, 1, 5, 3, 2).reshape(nseq * (t_real - skip_t), width_b)

    if lb == CHUNK:
        h1 = _mix_out(ys, bonus, g, ya, x_rows, w["mix_out"], nseq=nseq, nblk=nblk, skip=nchunk - nblk)
    else:
        h1 = _mix_out(ys, bonus, g, ya, x_rows, w["mix_out"], nseq=1, nblk=x_rows.shape[0] // CHUNK, skip=0)

    u, idx, gate = _peer_route(h1, w["route"], w["cand_blocks"], SUBLANES // WORDS)
    n = u.shape[0]
    act = _peer_u(idx, u.reshape(n, SUBLANES, LANES), gate, w["tab_u"], w["eexp"], tm=w["peer_tm"])
    ffn = _peer_v(idx, act, w["tab_v"], w["eexp"], tm=w["peer_tm"])
    y = _final_norm(h1, ffn.reshape(n, d), w["norm_final"])
    return y, conv_new, ssd_new, shift_new[:, 0, :], wkv_new


def kernel(x_prompt, x_sample, state_conv, state_ssd, state_shift, state_wkv, meta_tokens, norm_mix, w_in, conv_w, conv_b, dt_bias, a_log, d_skip, ssm_norm, shift_mu, decay_w0, decay_w2, iclr_a0, iclr_a2, gate_g2, k_k, k_a, r_k, lnx_w, lnx_b, w_out, norm_ffn, w_query, sub_keys, expert_u, expert_v, norm_final):
    bp, seq_p, d = x_prompt.shape
    bs, seq_s, _ = x_sample.shape
    depth = w_in.shape[0]
    assert depth == 1 and seq_p % CHUNK == 0 and (bs * seq_s) % CHUNK == 0 and seq_s % SUBLANES == 0
    heads_a = state_ssd.shape[2]
    width_a = heads_a * SSD_HEAD_DIM
    conv_dim = state_conv.shape[3]
    rw_in = state_shift.shape[2]
    heads_b = state_wkv.shape[2]
    width_b = heads_b * RWKV_HEAD_DIM
    assert heads_a <= LANES and rw_in == 3 * width_b + DECAY_LORA + AAA_LORA + GATE_LORA

    wi = w_in[0]
    ssd_in = width_a + conv_dim + heads_a
    w_cat = jnp.concatenate([
        wi[:, width_a:width_a + conv_dim], wi[:, :width_a],
        jnp.pad(wi[:, width_a + conv_dim:ssd_in], ((0, 0), (0, LANES - heads_a))),
        wi[:, ssd_in:]], axis=1).astype(BF16)
    widths = (conv_dim, width_a, LANES, rw_in)
    pad_h = lambda a: jnp.pad(a.reshape(1, -1), ((0, 0), (0, LANES - heads_a)))
    i_l = np.arange(CHUNK)
    tri = jnp.asarray(i_l[:, None] >= i_l[None, :], dtype=BF16)
    ehead = jnp.asarray(np.arange(LANES)[:, None] == np.arange(width_a)[None, :] // SSD_HEAD_DIM, dtype=BF16)
    ecol = jnp.asarray(np.arange(LANES)[:, None] == np.arange(heads_a * CHUNK)[None, :] // CHUNK, dtype=BF16)
    ssd_consts = (conv_w[0], conv_b[0].reshape(1, -1), pad_h(dt_bias[0]), pad_h(a_log[0]),
                  jnp.repeat(d_skip[0], SSD_HEAD_DIM).reshape(1, -1), ssm_norm[0].reshape(1, -1), tri, ehead, ecol)
    ones_b = _block_ones(width_b, RWKV_HEAD_DIM)
    zero = jnp.zeros((DECAY_LORA, width_b), F32)
    w_wa = jnp.concatenate([jnp.concatenate([decay_w2[0], zero], axis=1),
                            jnp.concatenate([jnp.zeros((AAA_LORA, width_b), F32), iclr_a2[0]], axis=1)], axis=0)
    row = lambda a: a.reshape(1, -1)
    pre_consts = (row(shift_mu[0]), row(decay_w0[0]), row(iclr_a0[0]), *_hi_lo(w_wa), *_hi_lo(gate_g2[0]),
                  row(k_k[0]), row(k_a[0]), row(r_k[0]), ones_b)
    wo = w_out[0].astype(BF16)
    mix_consts = (row(lnx_w[0]), row(lnx_b[0]), ones_b, wo[:width_a], wo[width_a:])
    cand_blocks, cand_ids = _cand_blocks()
    keys = sub_keys[0].transpose(1, 0, 2, 3).reshape(2 * PEER_HEADS, N_KEYS, -1)
    route_consts = (row(norm_ffn[0]), *_hi_lo(w_query[0]), *_hi_lo(keys),
                    jnp.asarray(np.broadcast_to(cand_ids[:, None], (cand_ids.shape[0], CHUNK))))
    nsel = PEER_HEADS * PEER_TOPK
    lane8 = np.arange(nsel * SUBLANES)
    eexp = jnp.asarray(np.arange(nsel)[:, None] == lane8[None, :] // SUBLANES, dtype=BF16)
    w = dict(norm_mix=row(norm_mix[0]), w_in=w_cat, widths=widths, ssm_norm=row(ssm_norm[0]), ones_b=ones_b,
             ssd=ssd_consts, rwkv_pre=pre_consts, mix_out=mix_consts, route=route_consts,
             tab_u=_pack_table(expert_u[0]), tab_v=_pack_table(expert_v[0]), eexp=eexp, cand_blocks=cand_blocks,
             norm_final=row(norm_final), peer_tm=CHUNK)

    npad = CHUNK - N_META
    lead = jnp.concatenate([jnp.zeros((npad, d), F32), meta_tokens.astype(F32)], axis=0)
    zeros = lambda *s: jnp.zeros(s, F32)
    yp, cp, sp, shp, wp = _stream(
        x_prompt.reshape(bp * seq_p, d), lead,
        zeros(bp, CONV_W - 1, conv_dim), zeros(bp, heads_a, SSD_HEAD_DIM, SSD_STATE), zeros(bp, rw_in),
        zeros(bp, heads_b, RWKV_HEAD_DIM, RWKV_HEAD_DIM), w,
        nseq=bp, nblk=seq_p // CHUNK, lb=CHUNK, has_lead=True, npad=npad, tb=N_META)
    ys, cs, ss, shs, ws = _stream(
        x_sample.reshape(bs * seq_s, d), lead, state_conv[0], state_ssd[0], state_shift[0], state_wkv[0], w,
        nseq=bs, nblk=1, lb=seq_s, has_lead=False, npad=CHUNK - seq_s, tb=seq_s)
    return (yp.reshape(bp, seq_p, d), ys.reshape(bs, seq_s, d), cp[None], sp[None], shp[None], wp[None],
            cs[None], ss[None], shs[None], ws[None])
```

```python
import functools

import jax
import jax.numpy as jnp
import numpy as np
from jax import lax
from jax.experimental import pallas as pl
from jax.experimental.pallas import tpu as pltpu

F32 = jnp.float32
BF16 = jnp.bfloat16
I32 = jnp.int32

N_META = 16
SSD_HEAD_DIM = 64
SSD_GROUPS = 2
SSD_STATE = 128
CONV_W = 4
RWKV_HEAD_DIM = 64
DECAY_LORA = 64
AAA_LORA = 64
GATE_LORA = 128
PEER_HEADS = 8
N_KEYS = 128
PEER_TOPK = 16
RMS_EPS = 1e-6
GN_EPS = 64e-5

LANES = 128
SUBLANES = 8
CHUNK = 128
VMEM_LIMIT_BYTES = 56 * 1024 * 1024


def _full(shape):
    zeros = (0,) * len(shape)
    return pl.BlockSpec(shape, lambda *_: zeros)


def _params(semantics, vmem=VMEM_LIMIT_BYTES):
    return pltpu.CompilerParams(dimension_semantics=semantics, vmem_limit_bytes=vmem)


def _split2(x):
    hi = x.astype(BF16)
    lo = (x - hi.astype(F32)).astype(BF16)
    return hi, lo


def _split3(x):
    hi = x.astype(BF16)
    r = x - hi.astype(F32)
    mid = r.astype(BF16)
    lo = (r - mid.astype(F32)).astype(BF16)
    return hi, mid, lo


def _dot(a, b):
    return jnp.dot(a, b, preferred_element_type=F32)


def _dot_nt(a, b):
    return lax.dot_general(a, b, (((1,), (1,)), ((), ())), preferred_element_type=F32)


def _dot_sel(x, sel):
    hi, mid, lo = _split3(x)
    return _dot(hi, sel) + _dot(mid, sel) + _dot(lo, sel)


def _dot_hp(x, w_hi, w_lo):
    hi, lo = _split2(x)
    return _dot(hi, w_hi) + (_dot(lo, w_hi) + _dot(hi, w_lo))


def _dot_hp_nt(x, w_hi, w_lo):
    hi, lo = _split2(x)
    return _dot_nt(hi, w_hi) + (_dot_nt(lo, w_hi) + _dot_nt(hi, w_lo))


def _silu(x):
    return x * jax.nn.sigmoid(x)


def _softplus(x):
    return jnp.maximum(x, 0.0) + jnp.log1p(jnp.exp(-jnp.abs(x)))


def _rmsnorm(x, g):
    ms = jnp.mean(x * x, axis=-1, keepdims=True)
    return x * lax.rsqrt(ms + RMS_EPS) * g


def _proj_in_kernel(x_ref, lead_ref, g_ref, w_ref, xbc_ref, z_ref, dt_ref, rw_ref, *, widths, has_lead):
    x = x_ref[...]
    if has_lead:
        x = jnp.where(pl.program_id(1) == 0, lead_ref[...], x)
    u = _rmsnorm(x, g_ref[...]).astype(BF16)
    p = _dot(u, w_ref[...])
    off = 0
    for ref, w in zip((xbc_ref, z_ref, dt_ref, rw_ref), widths):
        ref[...] = p[:, off:off + w]
        off += w


def _proj_in(rows, lead, g, w_cat, widths, *, nseq, nblk, has_lead):
    d = rows.shape[1]
    nblk_in = nblk - 1 if has_lead else nblk
    shift = 1 if has_lead else 0
    x_map = lambda s, j: (s * nblk_in + jnp.maximum(j - shift, 0), 0)
    o_map = lambda s, j: (s * nblk + j, 0)
    return pl.pallas_call(
        functools.partial(_proj_in_kernel, widths=widths, has_lead=has_lead),
        grid=(nseq, nblk),
        in_specs=[pl.BlockSpec((CHUNK, d), x_map), _full(lead.shape), _full(g.shape), _full(w_cat.shape)],
        out_specs=[pl.BlockSpec((CHUNK, w), o_map) for w in widths],
        out_shape=[jax.ShapeDtypeStruct((nseq * nblk * CHUNK, w), F32) for w in widths],
        compiler_params=_params(("parallel", "parallel")),
        name="proj_in",
    )(rows, lead, g, w_cat)


def _cumsum_rows(x, tri_ref):
    hi, mid, lo = _split3(x)
    tri = tri_ref[...]
    return _dot(tri, hi) + _dot(tri, mid) + _dot(tri, lo)


def _ssd_kernel(xbc_ref, z_ref, dt_ref, cs_ref, h0_ref, cw_ref, cb_ref, dtb_ref, alog_ref,
                dskip_ref, norm_ref, tri_ref, ehead_ref, ecol_ref,
                y_ref, cs_out_ref, h_out_ref, xfull, dtile, *, lb, npad, heads, width):
    c = pl.program_id(1)
    nc = pl.num_programs(1)
    L = CHUNK
    prev = CONV_W - 1
    base = SUBLANES
    state = SSD_STATE
    hd = SSD_HEAD_DIM
    per_group = heads // SSD_GROUPS

    @pl.when(c == 0)
    def _():
        h_out_ref[...] = h0_ref[...]
        if lb == L:
            xfull[base - prev:base, :] = cs_ref[0]

    if lb == L:
        xfull[base:base + L, :] = xbc_ref[...]
        dt_raw = dt_ref[...]
    else:
        xfull[...] = jnp.zeros_like(xfull)
        xfull[base + L - lb - prev:base + L - lb, :] = cs_ref[0]
        xfull[base + L - lb:base + L, :] = xbc_ref[...]
        dtile[...] = jnp.zeros_like(dtile)
        dtile[L - lb:L, :] = dt_ref[...]
        dt_raw = dtile[...]

    conv = cb_ref[...] + xfull[base - prev:base - prev + L, :] * cw_ref[0:1, :]
    for k in range(1, CONV_W):
        conv = conv + xfull[base - prev + k:base - prev + k + L, :] * cw_ref[k:k + 1, :]
    cs_new = xfull[base + L - prev:base + L, :]

    @pl.when(c == nc - 1)
    def _():
        cs_out_ref[0] = cs_new

    xfull[base - prev:base, :] = cs_new

    xbc = _silu(conv)
    xs = xbc[:, :width]
    gn = SSD_GROUPS * state
    row = lax.broadcasted_iota(I32, (L, LANES), 0) + c * L
    dt = jnp.where(row >= npad, _softplus(dt_raw + dtb_ref[...]), 0.0)
    da = dt * (-jnp.exp(alog_ref[...]))
    acum = _cumsum_rows(da, tri_ref)
    acum_t = acum.T
    ehead = ehead_ref[...]
    acum_x = _dot_sel(acum, ehead)
    dt_x = _dot_sel(dt, ehead)
    last_x = acum_x[L - 1:L, :]
    xdt = xs * dt_x
    wend = xdt * jnp.exp(last_x - acum_x)
    colb = _dot_sel(acum, ecol_ref[...])
    causal = (lax.broadcasted_iota(I32, (L, L), 0) >= lax.broadcasted_iota(I32, (L, L), 1))
    lane = lax.broadcasted_iota(I32, (L, LANES), 1)

    y_diag = []
    y_off = []
    for g in range(SSD_GROUPS):
        bm = xbc[:, width + g * state:width + (g + 1) * state].astype(BF16)
        cm = xbc[:, width + gn + g * state:width + gn + (g + 1) * state].astype(BF16)
        cb = _dot_nt(cm, bm)
        h_prev = h_out_ref[0, g * per_group:(g + 1) * per_group].reshape(per_group * hd, state)
        y_off.append(_dot_nt(cm, h_prev.astype(BF16)))
        for pair in range(per_group // 2):
            h_a = g * per_group + 2 * pair
            xp = xdt[:, h_a * hd:(h_a + 2) * hd].astype(BF16)
            outs = []
            for h in (h_a, h_a + 1):
                seg = colb[:, h * L:(h + 1) * L] - acum_t[h:h + 1, :]
                dec = jnp.exp(jnp.where(causal, seg, -jnp.inf))
                outs.append(_dot((cb * dec).astype(BF16), xp))
            y_diag.append(jnp.where(lane < hd, outs[0], outs[1]))
        wg_t = wend[:, g * per_group * hd:(g + 1) * per_group * hd].T.astype(BF16)
        upd = _dot(wg_t, bm)
        for i in range(per_group):
            h = g * per_group + i
            cd = jnp.exp(acum_t[h:h + 1, L - 1:L])
            h_out_ref[0, h] = h_out_ref[0, h] * cd + upd[i * hd:(i + 1) * hd, :]
    y = (jnp.concatenate(y_diag, axis=1) + jnp.concatenate(y_off, axis=1) * jnp.exp(acum_x)
         + xs * dskip_ref[...])
    y = y[L - lb:, :] * _silu(z_ref[...])
    y_ref[...] = _rmsnorm(y, norm_ref[...])


def _ssd(xbc, z, dtp, cs, h0, consts, *, nseq, lb, nchunk, npad):
    heads = h0.shape[1]
    width = z.shape[1]
    conv_dim = xbc.shape[1]
    row_map = lambda s, c: (s * nchunk + c, 0)
    kernel = functools.partial(_ssd_kernel, lb=lb, npad=npad, heads=heads, width=width)
    return pl.pallas_call(
        kernel,
        grid=(nseq, nchunk),
        in_specs=[pl.BlockSpec((lb, conv_dim), row_map),
                  pl.BlockSpec((lb, width), row_map),
                  pl.BlockSpec((lb, LANES), row_map),
                  pl.BlockSpec((1,) + cs.shape[1:], lambda s, c: (s, 0, 0)),
                  pl.BlockSpec((1,) + h0.shape[1:], lambda s, c: (s, 0, 0, 0))]
                 + [_full(a.shape) for a in consts],
        out_specs=[pl.BlockSpec((lb, width), row_map),
                   pl.BlockSpec((1,) + cs.shape[1:], lambda s, c: (s, 0, 0)),
                   pl.BlockSpec((1,) + h0.shape[1:], lambda s, c: (s, 0, 0, 0))],
        out_shape=[jax.ShapeDtypeStruct((nseq * nchunk * lb, width), F32),
                   jax.ShapeDtypeStruct(cs.shape, F32),
                   jax.ShapeDtypeStruct(h0.shape, F32)],
        scratch_shapes=[pltpu.VMEM((SUBLANES + CHUNK, conv_dim), F32), pltpu.VMEM((CHUNK, LANES), F32)],
        compiler_params=_params(("parallel", "arbitrary")),
        name="ssd",
    )(xbc, z, dtp, cs, h0, *consts)


def _rwkv_pre_kernel(rw_ref, sh_ref, mu_ref, w0_ref, a0_ref, wa_hi_ref, wa_lo_ref, g2_hi_ref, g2_lo_ref,
                     kk_ref, ka_ref, rk_ref, ones_ref,
                     *refs, lb, npad, width, stacked):
    if stacked:
        sc_o, g_o, bonus_o, sh_o, pfull = refs
    else:
        kn_o, d_o, b_o, k_o, r_o, v_o, g_o, bonus_o, sh_o, pfull = refs
    c = pl.program_id(1)
    nc = pl.num_programs(1)
    base = SUBLANES
    p = rw_ref[...]

    @pl.when(c == 0)
    def _():
        pfull[base - 1:base, :] = sh_ref[0]

    pfull[base:base + lb, :] = p
    prev = pfull[base - 1:base - 1 + lb, :]
    last = p[lb - 1:lb, :]
    pfull[base - 1:base, :] = last

    @pl.when(c == nc - 1)
    def _():
        sh_o[0] = last

    pm = p + (prev - p) * mu_ref[...]
    r = pm[:, :width]
    k = pm[:, width:2 * width]
    v = pm[:, 2 * width:3 * width]
    lora_in = pm[:, 3 * width:3 * width + DECAY_LORA + AAA_LORA]
    lane = lax.broadcasted_iota(I32, lora_in.shape, 1)
    lora_in = jnp.where(lane < DECAY_LORA, jnp.tanh(lora_in), lora_in)
    lora = _dot_hp(lora_in, wa_hi_ref[...], wa_lo_ref[...])
    gate_in = jax.nn.sigmoid(pm[:, 3 * width + DECAY_LORA + AAA_LORA:])
    g_o[...] = _dot_hp(gate_in, g2_hi_ref[...], g2_lo_ref[...])
    w = -_softplus(-(w0_ref[...] + lora[:, :width])) - 0.5
    row = lax.broadcasted_iota(I32, w.shape, 0) + c * lb
    dcy = jnp.where(row >= npad, jnp.exp(-jnp.exp(w)), 1.0)
    a = jax.nn.sigmoid(a0_ref[...] + lora[:, width:])
    ones = ones_ref[...]
    kn = k * kk_ref[...]
    kn = kn / jnp.maximum(jnp.sqrt(_dot_sel(kn * kn, ones)), 1e-12)
    kp = k * (1.0 + (a - 1.0) * ka_ref[...])
    bonus_o[...] = _dot_sel(r * kp * rk_ref[...], ones) * v
    scan_in = (kn, dcy, kn * a, kp, r, v)
    if stacked:
        for j, x in enumerate(scan_in):
            sc_o[j, 0] = x.T
    else:
        for ref, x in zip((kn_o, d_o, b_o, k_o, r_o, v_o), scan_in):
            ref[...] = x


def _rwkv_pre(rw, sh, consts, *, nseq, lb, nchunk, npad, width, stacked):
    rw_in = rw.shape[1]
    row_map = lambda s, c: (s * nchunk + c, 0)
    n = nseq * nchunk * lb
    rows = pl.BlockSpec((lb, width), row_map)
    if stacked:
        scan_specs = [pl.BlockSpec((6, 1, width, lb), lambda s, c: (0, s, 0, c))]
        scan_shapes = [jax.ShapeDtypeStruct((6, nseq, width, nchunk * lb), F32)]
    else:
        scan_specs = [rows] * 6
        scan_shapes = [jax.ShapeDtypeStruct((n, width), F32)] * 6
    kernel = functools.partial(_rwkv_pre_kernel, lb=lb, npad=npad, width=width, stacked=stacked)
    return pl.pallas_call(
        kernel,
        grid=(nseq, nchunk),
        in_specs=[pl.BlockSpec((lb, rw_in), row_map), pl.BlockSpec((1, 1, rw_in), lambda s, c: (s, 0, 0))]
                 + [_full(a.shape) for a in consts],
        out_specs=scan_specs + [rows, rows, pl.BlockSpec((1, 1, rw_in), lambda s, c: (s, 0, 0))],
        out_shape=scan_shapes + [jax.ShapeDtypeStruct((n, width), F32)] * 2 + [jax.ShapeDtypeStruct(sh.shape, F32)],
        scratch_shapes=[pltpu.VMEM((SUBLANES + lb, rw_in), F32)],
        compiler_params=_params(("parallel", "arbitrary")),
        name="rwkv_pre",
    )(rw, sh, *consts)


def _to_scan_k_kernel(x_ref, o_ref, *, nseq, heads, hd, dup):
    for k in range(hd):
        rows = jnp.concatenate([x_ref[0, s, pl.ds(k, heads, stride=hd), :] for s in range(nseq)], axis=0)
        o_ref[0, k] = jnp.concatenate([rows] * dup, axis=0).T


def _to_scan_v_kernel(x_ref, o_ref, *, nseq, heads, hd, dup):
    vr = hd // dup
    for v in range(vr):
        rows = jnp.concatenate([x_ref[0, s, pl.ds(p * vr + v, heads, stride=hd), :]
                                for p in range(dup) for s in range(nseq)], axis=0)
        o_ref[:, v, :] = rows.T


def _to_scan(sc, *, heads, dup):
    _, nseq, width, t_all = sc.shape
    hd = width // heads
    kw = dict(nseq=nseq, heads=heads, hd=hd, dup=dup)
    ks = pl.pallas_call(
        functools.partial(_to_scan_k_kernel, **kw),
        grid=(5, t_all // CHUNK),
        in_specs=[pl.BlockSpec((1, nseq, width, CHUNK), lambda j, c: (j, 0, 0, c))],
        out_specs=pl.BlockSpec((1, hd, CHUNK, LANES), lambda j, c: (j, 0, c, 0)),
        out_shape=jax.ShapeDtypeStruct((5, hd, t_all, LANES), F32),
        compiler_params=_params(("parallel", "parallel")),
        name="to_scan_k",
    )(sc)
    vs = pl.pallas_call(
        functools.partial(_to_scan_v_kernel, **kw),
        grid=(t_all // CHUNK,),
        in_specs=[pl.BlockSpec((1, nseq, width, CHUNK), lambda c: (5, 0, 0, c))],
        out_specs=pl.BlockSpec((CHUNK, hd // dup, LANES), lambda c: (c, 0, 0)),
        out_shape=jax.ShapeDtypeStruct((t_all, hd // dup, LANES), F32),
        compiler_params=_params(("parallel",)),
        name="to_scan_v",
    )(sc)
    return ks, vs


SCAN_PIECE = 32


def _rwkv_scan_kernel(kn_ref, d_ref, b_ref, k_ref, r_ref, v_ref, s0_ref, y_ref, s_ref, *, tb, hd, kmajor):
    @pl.when(pl.program_id(1) == 0)
    def _():
        s_ref[...] = s0_ref[...]

    vr = y_ref.shape[2]

    def krow(ref, t, k):
        return ref[0, k, pl.ds(t, 1), :] if kmajor else ref[0, t, pl.ds(k, 1), :]

    def step(t, carry):
        for p in range(vr // SCAN_PIECE):
            rows = slice(p * SCAN_PIECE, (p + 1) * SCAN_PIECE)
            v_t = v_ref[t, rows, :] if kmajor else v_ref[0, t, rows, :]
            acc = [jnp.zeros_like(v_t), jnp.zeros_like(v_t)]
            for k in range(hd):
                acc[k % 2] = acc[k % 2] + s_ref[0, k, rows, :] * krow(kn_ref, t, k)
            sa = acc[0] + acc[1]
            acc = [jnp.zeros_like(v_t), jnp.zeros_like(v_t)]
            for k in range(hd):
                sk = s_ref[0, k, rows, :] * krow(d_ref, t, k) - sa * krow(b_ref, t, k) + v_t * krow(k_ref, t, k)
                s_ref[0, k, rows, :] = sk
                acc[k % 2] = acc[k % 2] + sk * krow(r_ref, t, k)
            y_ref[0, t, rows, :] = acc[0] + acc[1]
        return carry

    lax.fori_loop(0, tb, step, 0)


def _scan_call(operands, in_specs, s0, *, ngroup, nsteps, tb, hd, vr, kmajor):
    vspec = pl.BlockSpec((1, tb, vr, LANES), lambda g, i: (g, i, 0, 0))
    sspec = pl.BlockSpec((1, hd, vr, LANES), lambda g, i: (g, 0, 0, 0))
    return pl.pallas_call(
        functools.partial(_rwkv_scan_kernel, tb=tb, hd=hd, kmajor=kmajor),
        grid=(ngroup, nsteps // tb),
        in_specs=in_specs + [sspec],
        out_specs=[vspec, sspec],
        out_shape=[jax.ShapeDtypeStruct((ngroup, nsteps, vr, LANES), F32), jax.ShapeDtypeStruct(s0.shape, F32)],
        compiler_params=_params(("parallel", "arbitrary")),
        name="rwkv_scan",
    )(*operands, s0)


def _rwkv_scan(kn, d, b, k, r, v, s0, *, tb):
    ngroup, t_total, hd, _ = kn.shape
    vr = v.shape[2]
    kspec = pl.BlockSpec((1, tb, hd, LANES), lambda g, i: (g, i, 0, 0))
    vspec = pl.BlockSpec((1, tb, vr, LANES), lambda g, i: (g, i, 0, 0))
    return _scan_call((kn, d, b, k, r, v), [kspec] * 5 + [vspec], s0, ngroup=ngroup, nsteps=t_total, tb=tb, hd=hd,
                      vr=vr, kmajor=False)


def _rwkv_scan_kmajor(ks, vs, s0, *, tb, t0):
    _, hd, t_all, _ = ks.shape
    vr = vs.shape[1]
    assert t0 % tb == 0
    off = t0 // tb
    kspecs = [pl.BlockSpec((1, hd, tb, LANES), lambda g, i, j=j: (j, 0, i + off, 0)) for j in range(5)]
    vspec = pl.BlockSpec((tb, vr, LANES), lambda g, i: (i + off, 0, 0))
    return _scan_call((ks,) * 5 + (vs,), kspecs + [vspec], s0, ngroup=1, nsteps=t_all - t0, tb=tb, hd=hd, vr=vr,
                      kmajor=True)


def _mix_out_kernel(ys_ref, bonus_ref, g_ref, ya_ref, h_ref, lnw_ref, lnb_ref, ones_ref, wa_ref, wb_ref, o_ref):
    ones = ones_ref[...]
    inv = 1.0 / RWKV_HEAD_DIM
    y = ys_ref[...]
    yc = y - _dot_sel(y, ones) * inv
    var = _dot_sel(yc * yc, ones) * inv
    yb = (yc * lax.rsqrt(var + GN_EPS) * lnw_ref[...] + lnb_ref[...] + bonus_ref[...]) * g_ref[...]
    o_ref[...] = (h_ref[...] + _dot(ya_ref[...].astype(BF16), wa_ref[...])
                  + _dot(yb.astype(BF16), wb_ref[...]))


def _mix_out(ys, bonus, g, ya, h, consts, *, nseq, nblk, skip):
    width = ys.shape[1]
    d = h.shape[1]
    cmap = lambda i: (i, 0)
    pmap = lambda i: ((i // nblk) * (nblk + skip) + skip + i % nblk, 0)
    wide = pl.BlockSpec((CHUNK, width), pmap)
    return pl.pallas_call(
        _mix_out_kernel,
        grid=(nseq * nblk,),
        in_specs=[pl.BlockSpec((CHUNK, width), cmap), wide, wide, wide, pl.BlockSpec((CHUNK, d), cmap)]
                 + [_full(a.shape) for a in consts],
        out_specs=pl.BlockSpec((CHUNK, d), cmap),
        out_shape=jax.ShapeDtypeStruct(h.shape, F32),
        compiler_params=_params(("parallel",)),
        name="mix_out",
    )(ys, bonus, g, ya, h, *consts)


def _cand_blocks():
    k = PEER_TOPK
    blocks = []
    a = 0
    while k // (a + 1) > 1:
        blocks.append((a, 1, -(-(k // (a + 1)) // SUBLANES) * SUBLANES))
        a += 1
    assert (k - a) % SUBLANES == 0
    blocks.append((a, k - a, 1))
    ids = np.concatenate([a0 * k + np.arange(nb) if na == 1 else (a0 + np.arange(na)) * k for a0, na, nb in blocks])
    return tuple(blocks), ids.astype(np.int32)


def _peer_route_kernel(h_ref, g_ref, wq_hi_ref, wq_lo_ref, k_hi_ref, k_lo_ref, cid_ref, u_o, idx_o, gate_o,
                       s_scr, v_scr, p_scr, c_scr, e_scr, ts_scr, ex_scr, *, blocks, idx_scale):
    u = _rmsnorm(h_ref[...], g_ref[...])
    u_o[...] = u
    q_hi, q_lo = _split2(_dot_hp(u, wq_hi_ref[...], wq_lo_ref[...]))
    tm = u.shape[0]
    k = PEER_TOPK
    nchain = 2 * PEER_HEADS
    for c in range(nchain):
        qh = q_hi[:, c * N_KEYS:(c + 1) * N_KEYS]
        ql = q_lo[:, c * N_KEYS:(c + 1) * N_KEYS]
        kh = k_hi_ref[c]
        s_scr[c] = _dot_nt(kh, qh) + (_dot_nt(k_lo_ref[c], qh) + _dot_nt(kh, ql))
    rid = lax.broadcasted_iota(I32, (N_KEYS, tm), 0)

    def top_keys(i, carry):
        for c in range(nchain):
            s = s_scr[c]
            m = jnp.max(s, axis=0, keepdims=True)
            p = jnp.min(jnp.where(s == m, rid, N_KEYS), axis=0, keepdims=True)
            v_scr[c, pl.ds(i, 1), :] = m
            p_scr[c, pl.ds(i, 1), :] = p
            s_scr[c] = jnp.where(rid == p, -jnp.inf, s)
        return carry

    lax.fori_loop(0, k, top_keys, 0)

    for h in range(PEER_HEADS):
        v1, v2 = v_scr[2 * h], v_scr[2 * h + 1]
        i1, i2 = p_scr[2 * h] * N_KEYS, p_scr[2 * h + 1]
        cand, eid = [], []
        for a0, na, nb in blocks:
            cand.append(v1[a0:a0 + na] + v2[0:nb])
            eid.append(i1[a0:a0 + na] + i2[0:nb])
        c_scr[h] = jnp.concatenate(cand, axis=0)
        e_scr[h] = jnp.concatenate(eid, axis=0)
    cid = cid_ref[...]
    ncand = cid.shape[0]

    def top_cands(i, carry):
        for h in range(PEER_HEADS):
            cand = c_scr[h]
            m = jnp.max(cand, axis=0, keepdims=True)
            c = jnp.min(jnp.where(cand == m, cid, k * k), axis=0, keepdims=True)
            hit = cid == c
            e = jnp.max(jnp.where(hit, e_scr[h], -1), axis=0, keepdims=True)
            c_scr[h] = jnp.where(hit, -jnp.inf, cand)
            ts_scr[pl.ds(h * k + i, 1), :] = m
            ex_scr[pl.ds(h * k + i, 1), :] = e * idx_scale
        return carry

    lax.fori_loop(0, k, top_cands, 0)
    idx_o[...] = ex_scr[...].T
    for h in range(PEER_HEADS):
        ts = ts_scr[h * k:(h + 1) * k, :]
        ex = jnp.exp(ts - jnp.max(ts, axis=0, keepdims=True))
        gate_o[h * k:(h + 1) * k, :] = ex / jnp.sum(ex, axis=0, keepdims=True)


def _peer_route(h, consts, blocks, idx_scale):
    n, d = h.shape
    tm = CHUNK
    nsel = PEER_HEADS * PEER_TOPK
    ncand = consts[-1].shape[0]
    rows = pl.BlockSpec((tm, d), lambda i: (i, 0))
    sel = pl.BlockSpec((nsel, tm), lambda i: (0, i))
    return pl.pallas_call(
        functools.partial(_peer_route_kernel, blocks=blocks, idx_scale=idx_scale),
        grid=(n // tm,),
        in_specs=[rows] + [_full(a.shape) for a in consts],
        out_specs=[rows, pl.BlockSpec((tm, nsel), lambda i: (i, 0)), sel],
        out_shape=[jax.ShapeDtypeStruct((n, d), F32), jax.ShapeDtypeStruct((n, nsel), I32),
                   jax.ShapeDtypeStruct((nsel, n), F32)],
        scratch_shapes=[pltpu.VMEM((2 * PEER_HEADS, N_KEYS, tm), F32),
                        pltpu.VMEM((2 * PEER_HEADS, PEER_TOPK, tm), F32),
                        pltpu.VMEM((2 * PEER_HEADS, PEER_TOPK, tm), I32),
                        pltpu.VMEM((PEER_HEADS, ncand, tm), F32),
                        pltpu.VMEM((PEER_HEADS, ncand, tm), I32),
                        pltpu.VMEM((nsel, tm), F32),
                        pltpu.VMEM((nsel, tm), I32)],
        compiler_params=_params(("parallel",)),
        name="peer_route",
    )(h, *consts)


PEER_GROUP = 16
PEER_SLOTS = 2
WORDS = 2


def _pack_kernel(t_ref, o_ref):
    r = t_ref.shape[0]
    o_ref[...] = pltpu.bitcast(t_ref[...].reshape(r * SUBLANES, LANES), I32)


def _pack_table(t):
    n, d = t.shape
    rows = d // LANES
    blk = 512
    t3 = t.astype(BF16).reshape(n, rows, LANES)
    return pl.pallas_call(
        _pack_kernel,
        grid=(n // blk,),
        in_specs=[pl.BlockSpec((blk, rows, LANES), lambda i: (i, 0, 0))],
        out_specs=pl.BlockSpec((blk * rows // WORDS, LANES), lambda i: (i, 0)),
        out_shape=jax.ShapeDtypeStruct((n * rows // WORDS, LANES), I32),
        compiler_params=_params(("parallel",)),
        name="pack_table",
    )(t3)


def _for_each_token(idx_ref, ids, sem, tm, token_fn):
    ngrp = tm // PEER_GROUP

    def copy(grp, slot):
        return pltpu.make_async_copy(idx_ref.at[pl.ds(grp * PEER_GROUP, PEER_GROUP)], ids.at[slot], sem.at[slot])

    for slot in range(PEER_SLOTS):
        copy(slot, slot).start()

    def body(q, carry):
        for slot in range(PEER_SLOTS):
            grp = q * PEER_SLOTS + slot
            copy(grp, slot).wait()
            for s in range(PEER_GROUP):
                token_fn(grp * PEER_GROUP + s, slot, s)

            @pl.when(grp + PEER_SLOTS < ngrp)
            def _():
                copy(grp + PEER_SLOTS, slot).start()
        return carry

    lax.fori_loop(0, ngrp // PEER_SLOTS, body, 0)


def _gather_rows(ids, tab_ref, gbuf, slot, s, nsel):
    rows_per = SUBLANES // WORDS
    for j in range(nsel):
        e = pl.multiple_of(ids[slot, s, j], rows_per)
        gbuf[s, j * rows_per:(j + 1) * rows_per, :] = tab_ref[pl.ds(e, rows_per), :]
    return pltpu.bitcast(gbuf[s], BF16)


def _diag_mask(nsel):
    sub = lax.broadcasted_iota(I32, (SUBLANES, nsel * SUBLANES), 0)
    lane = lax.broadcasted_iota(I32, (SUBLANES, nsel * SUBLANES), 1)
    return sub == lane % SUBLANES


def _gelu(x):
    return 0.5 * x * (1.0 + lax.erf(x * np.float32(1.0 / np.sqrt(2.0))))


def _peer_u_kernel(idx_ref, tok_ref, gate_ref, tab_ref, eexp_ref, act_o, gbuf, rsum, ids, sem, *, tm, nsel):
    diag = _diag_mask(nsel)

    def token(i, slot, s):
        rows = _gather_rows(ids, tab_ref, gbuf, slot, s, nsel)
        tok = tok_ref[i].astype(BF16)
        prod = _dot_nt(tok, rows)
        rsum[pl.ds(i, 1), :] = jnp.sum(jnp.where(diag, prod, 0.0), axis=0, keepdims=True)

    _for_each_token(idx_ref, ids, sem, tm, token)
    eexp = eexp_ref[...]
    hi, mid, lo = _split3(rsum[...])
    pre_t = _dot_nt(eexp, hi) + _dot_nt(eexp, mid) + _dot_nt(eexp, lo)
    act_o[...] = _gelu(pre_t) * gate_ref[...]


def _peer_v_kernel(idx_ref, act_ref, h_ref, g_ref, tab_ref, eexp_ref, out_o, gbuf, arep, ids, sem, *, tm, nsel):
    diag = _diag_mask(nsel)
    arep[...] = _dot(act_ref[...].T.astype(BF16), eexp_ref[...])

    def token(i, slot, s):
        rows = _gather_rows(ids, tab_ref, gbuf, slot, s, nsel)
        a = jnp.broadcast_to(arep[pl.ds(i, 1), :], diag.shape)
        out_o[i] = h_ref[i] + _dot(jnp.where(diag, a, 0.0).astype(BF16), rows)

    _for_each_token(idx_ref, ids, sem, tm, token)
    x = out_o[...]
    ms = jnp.sum(jnp.sum(x * x, axis=2, keepdims=True), axis=1, keepdims=True) * (1.0 / (SUBLANES * LANES))
    out_o[...] = x * lax.rsqrt(ms + RMS_EPS) * g_ref[...]


def _peer_gather_specs(tm, nsel, tab):
    idx = pl.BlockSpec((tm, nsel), lambda i: (i, 0))
    table = pl.BlockSpec(tab.shape, lambda i: (0, 0), pipeline_mode=pl.Buffered(1))
    scratch = [pltpu.VMEM((PEER_GROUP, nsel * SUBLANES // WORDS, LANES), I32),
               pltpu.VMEM((tm, nsel * SUBLANES), F32),
               pltpu.SMEM((PEER_SLOTS, PEER_GROUP, nsel), I32),
               pltpu.SemaphoreType.DMA((PEER_SLOTS,))]
    return idx, table, scratch


def _peer_u(idx, tok3, gate, tab, eexp, *, tm):
    n, nsel = idx.shape
    assert tm % (PEER_GROUP * PEER_SLOTS) == 0
    idx_spec, tab_spec, scratch = _peer_gather_specs(tm, nsel, tab)
    sel = pl.BlockSpec((nsel, tm), lambda i: (0, i))
    return pl.pallas_call(
        functools.partial(_peer_u_kernel, tm=tm, nsel=nsel),
        grid=(n // tm,),
        in_specs=[idx_spec, pl.BlockSpec((tm, SUBLANES, LANES), lambda i: (i, 0, 0)), sel, tab_spec,
                  _full(eexp.shape)],
        out_specs=sel,
        out_shape=jax.ShapeDtypeStruct((nsel, n), F32),
        scratch_shapes=scratch,
        compiler_params=_params(("arbitrary",)),
        name="peer_u",
    )(idx, tok3, gate, tab, eexp)


def _peer_v(idx, act, h3, gain, tab, eexp, *, tm):
    n, nsel = idx.shape
    assert tm % (PEER_GROUP * PEER_SLOTS) == 0
    idx_spec, tab_spec, scratch = _peer_gather_specs(tm, nsel, tab)
    tiles = pl.BlockSpec((tm, SUBLANES, LANES), lambda i: (i, 0, 0))
    return pl.pallas_call(
        functools.partial(_peer_v_kernel, tm=tm, nsel=nsel),
        grid=(n // tm,),
        in_specs=[idx_spec, pl.BlockSpec((nsel, tm), lambda i: (0, i)), tiles, _full(gain.shape), tab_spec,
                  _full(eexp.shape)],
        out_specs=tiles,
        out_shape=jax.ShapeDtypeStruct((n, SUBLANES, LANES), F32),
        scratch_shapes=scratch,
        compiler_params=_params(("arbitrary",)),
        name="peer_v",
    )(idx, act, h3, gain, tab, eexp)


def _hi_lo(w):
    hi = w.astype(BF16)
    return hi, (w - hi.astype(F32)).astype(BF16)


def _block_ones(n, blk):
    i = np.arange(n)
    return jnp.asarray(i[:, None] // blk == i[None, :] // blk, dtype=BF16)


def _stream(x_rows, lead, conv0, ssd0, shift0, wkv0, w, *, nseq, nblk, lb, has_lead, npad, tb):
    d = x_rows.shape[1]
    nblk_all = nblk + (1 if has_lead else 0)
    width_a = w["ssm_norm"].shape[1]
    width_b = w["ones_b"].shape[0]
    heads_b = width_b // RWKV_HEAD_DIM
    if lb == CHUNK:
        xbc, z, dtp, rw = _proj_in(x_rows, lead, w["norm_mix"], w["w_in"], w["widths"],
                                   nseq=nseq, nblk=nblk_all, has_lead=has_lead)
        nchunk = nblk_all
    else:
        xbc, z, dtp, rw = _proj_in(x_rows, lead, w["norm_mix"], w["w_in"], w["widths"],
                                   nseq=1, nblk=x_rows.shape[0] // CHUNK, has_lead=False)
        nchunk = 1
    ya, conv_new, ssd_new = _ssd(xbc, z, dtp, conv0, ssd0, w["ssd"], nseq=nseq, lb=lb, nchunk=nchunk, npad=npad)
    scan_kernels = lb == CHUNK and heads_b == SUBLANES and 2 * nseq * heads_b == LANES
    *scan_in, g, bonus, shift_new = _rwkv_pre(
        rw, shift0[:, None, :], w["rwkv_pre"], nseq=nseq, lb=lb, nchunk=nchunk,
        npad=npad if lb == CHUNK else 0, width=width_b, stacked=scan_kernels)

    t_all = nchunk * lb
    t0 = npad if lb == CHUNK else 0
    t_real = t_all - t0
    pairs = nseq * heads_b
    hd = RWKV_HEAD_DIM
    if pairs >= LANES:
        ngroup, dup = pairs // LANES, 1
    else:
        ngroup, dup = 1, LANES // pairs
    seq_per = nseq // ngroup
    vr = hd // dup

    def to_scan_k(a):
        a = a.reshape(ngroup, seq_per, t_all, heads_b, hd)[:, :, t0:]
        a = a.transpose(0, 2, 4, 1, 3).reshape(ngroup, t_real, hd, seq_per * heads_b)
        return jnp.tile(a, (1, 1, 1, dup))

    def to_scan_v(a):
        a = a.reshape(ngroup, seq_per, t_all, heads_b, dup, vr)[:, :, t0:]
        return a.transpose(0, 2, 5, 4, 1, 3).reshape(ngroup, t_real, vr, LANES)

    s0 = wkv0.reshape(ngroup, seq_per, heads_b, dup, vr, hd).transpose(0, 5, 4, 3, 1, 2)
    s0 = s0.reshape(ngroup, hd, vr, LANES)
    if scan_kernels:
        ks, vs = _to_scan(scan_in[0], heads=heads_b, dup=dup)
        ysc, s_new = _rwkv_scan_kmajor(ks, vs, s0, tb=tb, t0=t0)
    else:
        ysc, s_new = _rwkv_scan(*[to_scan_k(a) for a in scan_in[:5]], to_scan_v(scan_in[5]), s0, tb=tb)
    wkv_new = s_new.reshape(ngroup, hd, vr, dup, seq_per, heads_b).transpose(0, 4, 5, 3, 2, 1)
    wkv_new = wkv_new.reshape(nseq, heads_b, hd, hd)
    skip_t = t_real - nblk * lb if lb == CHUNK else 0
    ys = ysc[:, skip_t:].reshape(ngroup, t_real - skip_t, vr, dup, seq_per, heads_b)
    ys = ys.transpose(0, 4, 1, 5, 3, 2).reshape(nseq * (t_real - skip_t), width_b)

    if lb == CHUNK:
        h1 = _mix_out(ys, bonus, g, ya, x_rows, w["mix_out"], nseq=nseq, nblk=nblk, skip=nchunk - nblk)
    else:
        h1 = _mix_out(ys, bonus, g, ya, x_rows, w["mix_out"], nseq=1, nblk=x_rows.shape[0] // CHUNK, skip=0)

    u, idx, gate = _peer_route(h1, w["route"], w["cand_blocks"], SUBLANES // WORDS)
    n = u.shape[0]
    act = _peer_u(idx, u.reshape(n, SUBLANES, LANES), gate, w["tab_u"], w["eexp"], tm=w["peer_tm"])
    y = _peer_v(idx, act, h1.reshape(n, SUBLANES, LANES), w["norm_final"], w["tab_v"], w["eexp"], tm=w["peer_tm"])
    return y.reshape(n, d), conv_new, ssd_new, shift_new[:, 0, :], wkv_new


def kernel(x_prompt, x_sample, state_conv, state_ssd, state_shift, state_wkv, meta_tokens, norm_mix, w_in, conv_w, conv_b, dt_bias, a_log, d_skip, ssm_norm, shift_mu, decay_w0, decay_w2, iclr_a0, iclr_a2, gate_g2, k_k, k_a, r_k, lnx_w, lnx_b, w_out, norm_ffn, w_query, sub_keys, expert_u, expert_v, norm_final):
    bp, seq_p, d = x_prompt.shape
    bs, seq_s, _ = x_sample.shape
    depth = w_in.shape[0]
    assert depth == 1 and seq_p % CHUNK == 0 and (bs * seq_s) % CHUNK == 0 and seq_s % SUBLANES == 0
    assert d == SUBLANES * LANES
    heads_a = state_ssd.shape[2]
    width_a = heads_a * SSD_HEAD_DIM
    conv_dim = state_conv.shape[3]
    rw_in = state_shift.shape[2]
    heads_b = state_wkv.shape[2]
    width_b = heads_b * RWKV_HEAD_DIM
    assert heads_a <= LANES and rw_in == 3 * width_b + DECAY_LORA + AAA_LORA + GATE_LORA

    wi = w_in[0]
    ssd_in = width_a + conv_dim + heads_a
    w_cat = jnp.concatenate([
        wi[:, width_a:width_a + conv_dim], wi[:, :width_a],
        jnp.pad(wi[:, width_a + conv_dim:ssd_in], ((0, 0), (0, LANES - heads_a))),
        wi[:, ssd_in:]], axis=1).astype(BF16)
    widths = (conv_dim, width_a, LANES, rw_in)
    pad_h = lambda a: jnp.pad(a.reshape(1, -1), ((0, 0), (0, LANES - heads_a)))
    i_l = np.arange(CHUNK)
    tri = jnp.asarray(i_l[:, None] >= i_l[None, :], dtype=BF16)
    ehead = jnp.asarray(np.arange(LANES)[:, None] == np.arange(width_a)[None, :] // SSD_HEAD_DIM, dtype=BF16)
    ecol = jnp.asarray(np.arange(LANES)[:, None] == np.arange(heads_a * CHUNK)[None, :] // CHUNK, dtype=BF16)
    ssd_consts = (conv_w[0], conv_b[0].reshape(1, -1), pad_h(dt_bias[0]), pad_h(a_log[0]),
                  jnp.repeat(d_skip[0], SSD_HEAD_DIM).reshape(1, -1), ssm_norm[0].reshape(1, -1), tri, ehead, ecol)
    ones_b = _block_ones(width_b, RWKV_HEAD_DIM)
    zero = jnp.zeros((DECAY_LORA, width_b), F32)
    w_wa = jnp.concatenate([jnp.concatenate([decay_w2[0], zero], axis=1),
                            jnp.concatenate([jnp.zeros((AAA_LORA, width_b), F32), iclr_a2[0]], axis=1)], axis=0)
    row = lambda a: a.reshape(1, -1)
    pre_consts = (row(shift_mu[0]), row(decay_w0[0]), row(iclr_a0[0]), *_hi_lo(w_wa), *_hi_lo(gate_g2[0]),
                  row(k_k[0]), row(k_a[0]), row(r_k[0]), ones_b)
    wo = w_out[0].astype(BF16)
    mix_consts = (row(lnx_w[0]), row(lnx_b[0]), ones_b, wo[:width_a], wo[width_a:])
    cand_blocks, cand_ids = _cand_blocks()
    keys = sub_keys[0].transpose(1, 0, 2, 3).reshape(2 * PEER_HEADS, N_KEYS, -1)
    route_consts = (row(norm_ffn[0]), *_hi_lo(w_query[0]), *_hi_lo(keys),
                    jnp.asarray(np.broadcast_to(cand_ids[:, None], (cand_ids.shape[0], CHUNK))))
    nsel = PEER_HEADS * PEER_TOPK
    lane8 = np.arange(nsel * SUBLANES)
    eexp = jnp.asarray(np.arange(nsel)[:, None] == lane8[None, :] // SUBLANES, dtype=BF16)
    w = dict(norm_mix=row(norm_mix[0]), w_in=w_cat, widths=widths, ssm_norm=row(ssm_norm[0]), ones_b=ones_b,
             ssd=ssd_consts, rwkv_pre=pre_consts, mix_out=mix_consts, route=route_consts,
             tab_u=_pack_table(expert_u[0]), tab_v=_pack_table(expert_v[0]), eexp=eexp, cand_blocks=cand_blocks,
             norm_final=norm_final.reshape(SUBLANES, LANES), peer_tm=CHUNK)

    npad = CHUNK - N_META
    lead = jnp.concatenate([jnp.zeros((npad, d), F32), meta_tokens.astype(F32)], axis=0)
    zeros = lambda *s: jnp.zeros(s, F32)
    yp, cp, sp, shp, wp = _stream(
        x_prompt.reshape(bp * seq_p, d), lead,
        zeros(bp, CONV_W - 1, conv_dim), zeros(bp, heads_a, SSD_HEAD_DIM, SSD_STATE), zeros(bp, rw_in),
        zeros(bp, heads_b, RWKV_HEAD_DIM, RWKV_HEAD_DIM), w,
        nseq=bp, nblk=seq_p // CHUNK, lb=CHUNK, has_lead=True, npad=npad, tb=N_META)
    ys, cs, ss, shs, ws = _stream(
        x_sample.reshape(bs * seq_s, d), lead, state_conv[0], state_ssd[0], state_shift[0], state_wkv[0], w,
        nseq=bs, nblk=1, lb=seq_s, has_lead=False, npad=CHUNK - seq_s, tb=seq_s)
    return (yp.reshape(bp, seq_p, d), ys.reshape(bs, seq_s, d), cp[None], sp[None], shp[None], wp[None],
            cs[None], ss[None], shs[None], ws[None])
```

```python
import functools

import jax
import jax.numpy as jnp
import numpy as np
from jax import lax
from jax.experimental import pallas as pl
from jax.experimental.pallas import tpu as pltpu

F32 = jnp.float32
BF16 = jnp.bfloat16
I32 = jnp.int32

N_META = 16
SSD_HEAD_DIM = 64
SSD_GROUPS = 2
SSD_STATE = 128
CONV_W = 4
RWKV_HEAD_DIM = 64
DECAY_LORA = 64
AAA_LORA = 64
GATE_LORA = 128
PEER_HEADS = 8
N_KEYS = 128
PEER_TOPK = 16
RMS_EPS = 1e-6
GN_EPS = 64e-5

LANES = 128
SUBLANES = 8
CHUNK = 128
VMEM_LIMIT_BYTES = 56 * 1024 * 1024


def _full(shape):
    zeros = (0,) * len(shape)
    return pl.BlockSpec(shape, lambda *_: zeros)


def _params(semantics, vmem=VMEM_LIMIT_BYTES):
    return pltpu.CompilerParams(dimension_semantics=semantics, vmem_limit_bytes=vmem)


def _split2(x):
    hi = x.astype(BF16)
    lo = (x - hi.astype(F32)).astype(BF16)
    return hi, lo


def _split3(x):
    hi = x.astype(BF16)
    r = x - hi.astype(F32)
    mid = r.astype(BF16)
    lo = (r - mid.astype(F32)).astype(BF16)
    return hi, mid, lo


def _dot(a, b):
    return jnp.dot(a, b, preferred_element_type=F32)


def _dot_nt(a, b):
    return lax.dot_general(a, b, (((1,), (1,)), ((), ())), preferred_element_type=F32)


def _dot_sel(x, sel):
    hi, mid, lo = _split3(x)
    return _dot(hi, sel) + _dot(mid, sel) + _dot(lo, sel)


def _dot_hp(x, w_hi, w_lo):
    hi, lo = _split2(x)
    return _dot(hi, w_hi) + (_dot(lo, w_hi) + _dot(hi, w_lo))


def _dot_hp_nt(x, w_hi, w_lo):
    hi, lo = _split2(x)
    return _dot_nt(hi, w_hi) + (_dot_nt(lo, w_hi) + _dot_nt(hi, w_lo))


def _silu(x):
    return x * jax.nn.sigmoid(x)


def _softplus(x):
    return jnp.maximum(x, 0.0) + jnp.log1p(jnp.exp(-jnp.abs(x)))


def _rmsnorm(x, g):
    ms = jnp.mean(x * x, axis=-1, keepdims=True)
    return x * lax.rsqrt(ms + RMS_EPS) * g


def _proj_in_kernel(x_ref, lead_ref, g_ref, w_ref, xbc_ref, z_ref, dt_ref, rw_ref, *, widths, has_lead):
    x = x_ref[...]
    if has_lead:
        x = jnp.where(pl.program_id(1) == 0, lead_ref[...], x)
    u = _rmsnorm(x, g_ref[...]).astype(BF16)
    p = _dot(u, w_ref[...])
    off = 0
    for ref, w in zip((xbc_ref, z_ref, dt_ref, rw_ref), widths):
        ref[...] = p[:, off:off + w]
        off += w


def _proj_in(rows, lead, g, w_cat, widths, *, nseq, nblk, has_lead):
    d = rows.shape[1]
    nblk_in = nblk - 1 if has_lead else nblk
    shift = 1 if has_lead else 0
    x_map = lambda s, j: (s * nblk_in + jnp.maximum(j - shift, 0), 0)
    o_map = lambda s, j: (s * nblk + j, 0)
    return pl.pallas_call(
        functools.partial(_proj_in_kernel, widths=widths, has_lead=has_lead),
        grid=(nseq, nblk),
        in_specs=[pl.BlockSpec((CHUNK, d), x_map), _full(lead.shape), _full(g.shape), _full(w_cat.shape)],
        out_specs=[pl.BlockSpec((CHUNK, w), o_map) for w in widths],
        out_shape=[jax.ShapeDtypeStruct((nseq * nblk * CHUNK, w), F32) for w in widths],
        compiler_params=_params(("parallel", "parallel")),
        name="proj_in",
    )(rows, lead, g, w_cat)


def _cumsum_rows(x, tri_ref):
    hi, mid, lo = _split3(x)
    tri = tri_ref[...]
    return _dot(tri, hi) + _dot(tri, mid) + _dot(tri, lo)


def _ssd_kernel(xbc_ref, z_ref, dt_ref, cs_ref, h0_ref, cw_ref, cb_ref, dtb_ref, alog_ref,
                dskip_ref, norm_ref, tri_ref, ehead_ref, ecol_ref,
                y_ref, cs_out_ref, h_out_ref, xfull, dtile, *, lb, npad, heads, width):
    c = pl.program_id(1)
    nc = pl.num_programs(1)
    L = CHUNK
    prev = CONV_W - 1
    base = SUBLANES
    state = SSD_STATE
    hd = SSD_HEAD_DIM
    per_group = heads // SSD_GROUPS

    @pl.when(c == 0)
    def _():
        h_out_ref[...] = h0_ref[...]
        if lb == L:
            xfull[base - prev:base, :] = cs_ref[0]

    if lb == L:
        xfull[base:base + L, :] = xbc_ref[...]
        dt_raw = dt_ref[...]
    else:
        xfull[...] = jnp.zeros_like(xfull)
        xfull[base + L - lb - prev:base + L - lb, :] = cs_ref[0]
        xfull[base + L - lb:base + L, :] = xbc_ref[...]
        dtile[...] = jnp.zeros_like(dtile)
        dtile[L - lb:L, :] = dt_ref[...]
        dt_raw = dtile[...]

    conv = cb_ref[...] + xfull[base - prev:base - prev + L, :] * cw_ref[0:1, :]
    for k in range(1, CONV_W):
        conv = conv + xfull[base - prev + k:base - prev + k + L, :] * cw_ref[k:k + 1, :]
    cs_new = xfull[base + L - prev:base + L, :]

    @pl.when(c == nc - 1)
    def _():
        cs_out_ref[0] = cs_new

    xfull[base - prev:base, :] = cs_new

    xbc = _silu(conv)
    xs = xbc[:, :width]
    gn = SSD_GROUPS * state
    row = lax.broadcasted_iota(I32, (L, LANES), 0) + c * L
    dt = jnp.where(row >= npad, _softplus(dt_raw + dtb_ref[...]), 0.0)
    da = dt * (-jnp.exp(alog_ref[...]))
    acum = _cumsum_rows(da, tri_ref)
    acum_t = acum.T
    ehead = ehead_ref[...]
    acum_x = _dot_sel(acum, ehead)
    dt_x = _dot_sel(dt, ehead)
    last_x = acum_x[L - 1:L, :]
    xdt = xs * dt_x
    wend = xdt * jnp.exp(last_x - acum_x)
    colb = _dot_sel(acum, ecol_ref[...])
    causal = (lax.broadcasted_iota(I32, (L, L), 0) >= lax.broadcasted_iota(I32, (L, L), 1))
    lane = lax.broadcasted_iota(I32, (L, LANES), 1)

    y_diag = []
    y_off = []
    for g in range(SSD_GROUPS):
        bm = xbc[:, width + g * state:width + (g + 1) * state].astype(BF16)
        cm = xbc[:, width + gn + g * state:width + gn + (g + 1) * state].astype(BF16)
        cb = _dot_nt(cm, bm)
        h_prev = h_out_ref[0, g * per_group:(g + 1) * per_group].reshape(per_group * hd, state)
        y_off.append(_dot_nt(cm, h_prev.astype(BF16)))
        for pair in range(per_group // 2):
            h_a = g * per_group + 2 * pair
            xp = xdt[:, h_a * hd:(h_a + 2) * hd].astype(BF16)
            outs = []
            for h in (h_a, h_a + 1):
                seg = colb[:, h * L:(h + 1) * L] - acum_t[h:h + 1, :]
                dec = jnp.exp(jnp.where(causal, seg, -jnp.inf))
                outs.append(_dot((cb * dec).astype(BF16), xp))
            y_diag.append(jnp.where(lane < hd, outs[0], outs[1]))
        wg_t = wend[:, g * per_group * hd:(g + 1) * per_group * hd].T.astype(BF16)
        upd = _dot(wg_t, bm)
        for i in range(per_group):
            h = g * per_group + i
            cd = jnp.exp(acum_t[h:h + 1, L - 1:L])
            h_out_ref[0, h] = h_out_ref[0, h] * cd + upd[i * hd:(i + 1) * hd, :]
    y = (jnp.concatenate(y_diag, axis=1) + jnp.concatenate(y_off, axis=1) * jnp.exp(acum_x)
         + xs * dskip_ref[...])
    y = y[L - lb:, :] * _silu(z_ref[...])
    y_ref[...] = _rmsnorm(y, norm_ref[...])


def _ssd(xbc, z, dtp, cs, h0, consts, *, nseq, lb, nchunk, npad):
    heads = h0.shape[1]
    width = z.shape[1]
    conv_dim = xbc.shape[1]
    row_map = lambda s, c: (s * nchunk + c, 0)
    kernel = functools.partial(_ssd_kernel, lb=lb, npad=npad, heads=heads, width=width)
    return pl.pallas_call(
        kernel,
        grid=(nseq, nchunk),
        in_specs=[pl.BlockSpec((lb, conv_dim), row_map),
                  pl.BlockSpec((lb, width), row_map),
                  pl.BlockSpec((lb, LANES), row_map),
                  pl.BlockSpec((1,) + cs.shape[1:], lambda s, c: (s, 0, 0)),
                  pl.BlockSpec((1,) + h0.shape[1:], lambda s, c: (s, 0, 0, 0))]
                 + [_full(a.shape) for a in consts],
        out_specs=[pl.BlockSpec((lb, width), row_map),
                   pl.BlockSpec((1,) + cs.shape[1:], lambda s, c: (s, 0, 0)),
                   pl.BlockSpec((1,) + h0.shape[1:], lambda s, c: (s, 0, 0, 0))],
        out_shape=[jax.ShapeDtypeStruct((nseq * nchunk * lb, width), F32),
                   jax.ShapeDtypeStruct(cs.shape, F32),
                   jax.ShapeDtypeStruct(h0.shape, F32)],
        scratch_shapes=[pltpu.VMEM((SUBLANES + CHUNK, conv_dim), F32), pltpu.VMEM((CHUNK, LANES), F32)],
        compiler_params=_params(("parallel", "arbitrary")),
        name="ssd",
    )(xbc, z, dtp, cs, h0, *consts)


def _rwkv_pre_kernel(rw_ref, sh_ref, mu_ref, w0_ref, a0_ref, wa_hi_ref, wa_lo_ref, g2_hi_ref, g2_lo_ref,
                     kk_ref, ka_ref, rk_ref, ones_ref,
                     *refs, lb, npad, width, stacked):
    if stacked:
        sc_o, g_o, bonus_o, sh_o, pfull = refs
    else:
        kn_o, d_o, b_o, k_o, r_o, v_o, g_o, bonus_o, sh_o, pfull = refs
    c = pl.program_id(1)
    nc = pl.num_programs(1)
    base = SUBLANES
    p = rw_ref[...]

    @pl.when(c == 0)
    def _():
        pfull[base - 1:base, :] = sh_ref[0]

    pfull[base:base + lb, :] = p
    prev = pfull[base - 1:base - 1 + lb, :]
    last = p[lb - 1:lb, :]
    pfull[base - 1:base, :] = last

    @pl.when(c == nc - 1)
    def _():
        sh_o[0] = last

    pm = p + (prev - p) * mu_ref[...]
    r = pm[:, :width]
    k = pm[:, width:2 * width]
    v = pm[:, 2 * width:3 * width]
    lora_in = pm[:, 3 * width:3 * width + DECAY_LORA + AAA_LORA]
    lane = lax.broadcasted_iota(I32, lora_in.shape, 1)
    lora_in = jnp.where(lane < DECAY_LORA, jnp.tanh(lora_in), lora_in)
    lora = _dot_hp(lora_in, wa_hi_ref[...], wa_lo_ref[...])
    gate_in = jax.nn.sigmoid(pm[:, 3 * width + DECAY_LORA + AAA_LORA:])
    g_o[...] = _dot_hp(gate_in, g2_hi_ref[...], g2_lo_ref[...])
    w = -_softplus(-(w0_ref[...] + lora[:, :width])) - 0.5
    row = lax.broadcasted_iota(I32, w.shape, 0) + c * lb
    dcy = jnp.where(row >= npad, jnp.exp(-jnp.exp(w)), 1.0)
    a = jax.nn.sigmoid(a0_ref[...] + lora[:, width:])
    ones = ones_ref[...]
    kn = k * kk_ref[...]
    kn = kn / jnp.maximum(jnp.sqrt(_dot_sel(kn * kn, ones)), 1e-12)
    kp = k * (1.0 + (a - 1.0) * ka_ref[...])
    bonus_o[...] = _dot_sel(r * kp * rk_ref[...], ones) * v
    scan_in = (kn, dcy, kn * a, kp, r, v)
    if stacked:
        for j, x in enumerate(scan_in):
            sc_o[j, 0] = x.T
    else:
        for ref, x in zip((kn_o, d_o, b_o, k_o, r_o, v_o), scan_in):
            ref[...] = x


def _rwkv_pre(rw, sh, consts, *, nseq, lb, nchunk, npad, width, stacked):
    rw_in = rw.shape[1]
    row_map = lambda s, c: (s * nchunk + c, 0)
    n = nseq * nchunk * lb
    rows = pl.BlockSpec((lb, width), row_map)
    if stacked:
        scan_specs = [pl.BlockSpec((6, 1, width, lb), lambda s, c: (0, s, 0, c))]
        scan_shapes = [jax.ShapeDtypeStruct((6, nseq, width, nchunk * lb), F32)]
    else:
        scan_specs = [rows] * 6
        scan_shapes = [jax.ShapeDtypeStruct((n, width), F32)] * 6
    kernel = functools.partial(_rwkv_pre_kernel, lb=lb, npad=npad, width=width, stacked=stacked)
    return pl.pallas_call(
        kernel,
        grid=(nseq, nchunk),
        in_specs=[pl.BlockSpec((lb, rw_in), row_map), pl.BlockSpec((1, 1, rw_in), lambda s, c: (s, 0, 0))]
                 + [_full(a.shape) for a in consts],
        out_specs=scan_specs + [rows, rows, pl.BlockSpec((1, 1, rw_in), lambda s, c: (s, 0, 0))],
        out_shape=scan_shapes + [jax.ShapeDtypeStruct((n, width), F32)] * 2 + [jax.ShapeDtypeStruct(sh.shape, F32)],
        scratch_shapes=[pltpu.VMEM((SUBLANES + lb, rw_in), F32)],
        compiler_params=_params(("parallel", "arbitrary")),
        name="rwkv_pre",
    )(rw, sh, *consts)


def _to_scan_k_kernel(x_ref, o_ref, *, nseq, heads, hd, dup):
    nblk, _, tb, _ = o_ref.shape[1:]
    for k in range(hd):
        rows = jnp.concatenate([x_ref[0, s, pl.ds(k, heads, stride=hd), :] for s in range(nseq)], axis=0)
        o_ref[0, :, k] = jnp.concatenate([rows] * dup, axis=0).T.reshape(nblk, tb, LANES)


def _to_scan_v_kernel(x_ref, o_ref, *, nseq, heads, hd, dup):
    vr = hd // dup
    for v in range(vr):
        rows = jnp.concatenate([x_ref[0, s, pl.ds(p * vr + v, heads, stride=hd), :]
                                for p in range(dup) for s in range(nseq)], axis=0)
        o_ref[:, v, :] = rows.T


def _to_scan(sc, *, heads, dup, tb):
    _, nseq, width, t_all = sc.shape
    hd = width // heads
    kw = dict(nseq=nseq, heads=heads, hd=hd, dup=dup)
    ks = pl.pallas_call(
        functools.partial(_to_scan_k_kernel, **kw),
        grid=(5, t_all // CHUNK),
        in_specs=[pl.BlockSpec((1, nseq, width, CHUNK), lambda j, c: (j, 0, 0, c))],
        out_specs=pl.BlockSpec((1, CHUNK // tb, hd, tb, LANES), lambda j, c: (j, c, 0, 0, 0)),
        out_shape=jax.ShapeDtypeStruct((5, t_all // tb, hd, tb, LANES), F32),
        compiler_params=_params(("parallel", "parallel")),
        name="to_scan_k",
    )(sc)
    vs = pl.pallas_call(
        functools.partial(_to_scan_v_kernel, **kw),
        grid=(t_all // CHUNK,),
        in_specs=[pl.BlockSpec((1, nseq, width, CHUNK), lambda c: (5, 0, 0, c))],
        out_specs=pl.BlockSpec((CHUNK, hd // dup, LANES), lambda c: (c, 0, 0)),
        out_shape=jax.ShapeDtypeStruct((t_all, hd // dup, LANES), F32),
        compiler_params=_params(("parallel",)),
        name="to_scan_v",
    )(sc)
    return ks, vs


SCAN_PIECE = 32


def _rwkv_scan_kernel(kn_ref, d_ref, b_ref, k_ref, r_ref, v_ref, s0_ref, y_ref, s_ref, *, tb, hd, kmajor):
    @pl.when(pl.program_id(1) == 0)
    def _():
        s_ref[...] = s0_ref[...]

    vr = y_ref.shape[2]

    def krow(ref, t, k):
        return ref[0, 0, k, pl.ds(t, 1), :] if kmajor else ref[0, t, pl.ds(k, 1), :]

    def step(t, carry):
        piece = min(SCAN_PIECE, vr)
        for p in range(vr // piece):
            rows = slice(p * piece, (p + 1) * piece)
            v_t = v_ref[t, rows, :] if kmajor else v_ref[0, t, rows, :]
            acc = [jnp.zeros_like(v_t), jnp.zeros_like(v_t)]
            for k in range(hd):
                acc[k % 2] = acc[k % 2] + s_ref[0, k, rows, :] * krow(kn_ref, t, k)
            sa = acc[0] + acc[1]
            acc = [jnp.zeros_like(v_t), jnp.zeros_like(v_t)]
            for k in range(hd):
                sk = s_ref[0, k, rows, :] * krow(d_ref, t, k) - sa * krow(b_ref, t, k) + v_t * krow(k_ref, t, k)
                s_ref[0, k, rows, :] = sk
                acc[k % 2] = acc[k % 2] + sk * krow(r_ref, t, k)
            y_ref[0, t, rows, :] = acc[0] + acc[1]
        return carry

    lax.fori_loop(0, tb, step, 0)


def _scan_call(operands, in_specs, s0, *, ngroup, nsteps, tb, hd, vr, kmajor):
    vspec = pl.BlockSpec((1, tb, vr, LANES), lambda g, i: (g, i, 0, 0))
    sspec = pl.BlockSpec((1, hd, vr, LANES), lambda g, i: (g, 0, 0, 0))
    return pl.pallas_call(
        functools.partial(_rwkv_scan_kernel, tb=tb, hd=hd, kmajor=kmajor),
        grid=(ngroup, nsteps // tb),
        in_specs=in_specs + [sspec],
        out_specs=[vspec, sspec],
        out_shape=[jax.ShapeDtypeStruct((ngroup, nsteps, vr, LANES), F32), jax.ShapeDtypeStruct(s0.shape, F32)],
        compiler_params=_params(("parallel", "arbitrary")),
        name="rwkv_scan",
    )(*operands, s0)


def _rwkv_scan(kn, d, b, k, r, v, s0, *, tb):
    ngroup, t_total, hd, _ = kn.shape
    vr = v.shape[2]
    kspec = pl.BlockSpec((1, tb, hd, LANES), lambda g, i: (g, i, 0, 0))
    vspec = pl.BlockSpec((1, tb, vr, LANES), lambda g, i: (g, i, 0, 0))
    return _scan_call((kn, d, b, k, r, v), [kspec] * 5 + [vspec], s0, ngroup=ngroup, nsteps=t_total, tb=tb, hd=hd,
                      vr=vr, kmajor=False)


def _rwkv_scan_kmajor(ks, vs, s0, *, tb, t0):
    _, nblk, hd, tb_k, _ = ks.shape
    t_all, vr, _ = vs.shape
    assert t0 % tb == 0 and tb_k == tb and nblk * tb == t_all
    off = t0 // tb
    kspecs = [pl.BlockSpec((1, 1, hd, tb, LANES), lambda g, i, j=j: (j, i + off, 0, 0, 0)) for j in range(5)]
    vspec = pl.BlockSpec((tb, vr, LANES), lambda g, i: (i + off, 0, 0))
    return _scan_call((ks,) * 5 + (vs,), kspecs + [vspec], s0, ngroup=1, nsteps=t_all - t0, tb=tb, hd=hd, vr=vr,
                      kmajor=True)


def _mix_out_kernel(ys_ref, bonus_ref, g_ref, ya_ref, h_ref, lnw_ref, lnb_ref, ones_ref, wa_ref, wb_ref, o_ref):
    ones = ones_ref[...]
    inv = 1.0 / RWKV_HEAD_DIM
    y = ys_ref[...]
    yc = y - _dot_sel(y, ones) * inv
    var = _dot_sel(yc * yc, ones) * inv
    yb = (yc * lax.rsqrt(var + GN_EPS) * lnw_ref[...] + lnb_ref[...] + bonus_ref[...]) * g_ref[...]
    o_ref[...] = (h_ref[...] + _dot(ya_ref[...].astype(BF16), wa_ref[...])
                  + _dot(yb.astype(BF16), wb_ref[...]))


def _mix_out(ys, bonus, g, ya, h, consts, *, nseq, nblk, skip):
    width = ys.shape[1]
    d = h.shape[1]
    cmap = lambda i: (i, 0)
    pmap = lambda i: ((i // nblk) * (nblk + skip) + skip + i % nblk, 0)
    wide = pl.BlockSpec((CHUNK, width), pmap)
    return pl.pallas_call(
        _mix_out_kernel,
        grid=(nseq * nblk,),
        in_specs=[pl.BlockSpec((CHUNK, width), cmap), wide, wide, wide, pl.BlockSpec((CHUNK, d), cmap)]
                 + [_full(a.shape) for a in consts],
        out_specs=pl.BlockSpec((CHUNK, d), cmap),
        out_shape=jax.ShapeDtypeStruct(h.shape, F32),
        compiler_params=_params(("parallel",)),
        name="mix_out",
    )(ys, bonus, g, ya, h, *consts)


def _cand_rows():
    k = PEER_TOPK
    pairs = []
    for a in range(k):
        nb = k // (a + 1)
        width = -(-nb // SUBLANES) * SUBLANES if nb >= SUBLANES else 1 << (nb - 1).bit_length()
        pairs += [(a, b) for b in range(width)]
    pairs += [None] * (-len(pairs) % SUBLANES)
    ids = np.asarray([k * k if p is None else p[0] * k + p[1] for p in pairs], np.int32)
    return tuple(pairs), ids


def _pick_rows(x, picks):
    wanted = [p for p in picks if p is not None]
    if len(wanted) == SUBLANES and wanted == list(range(wanted[0], wanted[0] + SUBLANES)):
        return x[wanted[0]:wanted[0] + SUBLANES]
    row = lax.broadcasted_iota(I32, (SUBLANES, x.shape[1]), 0)
    out = None
    for val in dict.fromkeys(wanted):
        piece = jnp.broadcast_to(x[val:val + 1], (SUBLANES, x.shape[1]))
        if out is None:
            out = piece
        else:
            mask = functools.reduce(jnp.logical_or, [row == r for r, p in enumerate(picks) if p == val])
            out = jnp.where(mask, piece, out)
    return out


def _peer_route_kernel(h_ref, g_ref, wq_ref, k_hi_ref, k_lo_ref, cid_ref, u_o, idx_o, gate_o,
                       s_scr, v_scr, p_scr, c_scr, e_scr, ts_scr, ex_scr, *, pairs, idx_scale):
    u = _rmsnorm(h_ref[...], g_ref[...])
    u_o[...] = u
    u_hi, u_lo = _split2(u)
    wq = wq_ref[...]
    q_hi, q_lo = _split2(_dot(u_hi, wq) + _dot(u_lo, wq))
    tm = u.shape[0]
    k = PEER_TOPK
    nchain = 2 * PEER_HEADS
    for c in range(nchain):
        qh = q_hi[:, c * N_KEYS:(c + 1) * N_KEYS]
        ql = q_lo[:, c * N_KEYS:(c + 1) * N_KEYS]
        kh = k_hi_ref[c]
        s_scr[c] = _dot_nt(kh, qh) + (_dot_nt(k_lo_ref[c], qh) + _dot_nt(kh, ql))
    rid = lax.broadcasted_iota(I32, (N_KEYS, tm), 0)

    def top_keys(i, carry):
        for c in range(nchain):
            s = s_scr[c]
            m = jnp.max(s, axis=0, keepdims=True)
            p = jnp.min(jnp.where(s == m, rid, N_KEYS), axis=0, keepdims=True)
            v_scr[c, pl.ds(i, 1), :] = m
            p_scr[c, pl.ds(i, 1), :] = p
            s_scr[c] = jnp.where(rid == p, -jnp.inf, s)
        return carry

    lax.fori_loop(0, k, top_keys, 0)

    for h in range(PEER_HEADS):
        v1, v2 = v_scr[2 * h], v_scr[2 * h + 1]
        i1, i2 = p_scr[2 * h] * N_KEYS, p_scr[2 * h + 1]
        for t in range(len(pairs) // SUBLANES):
            tile = pairs[t * SUBLANES:(t + 1) * SUBLANES]
            pa = [None if p is None else p[0] for p in tile]
            pb = [None if p is None else p[1] for p in tile]
            cand = _pick_rows(v1, pa) + _pick_rows(v2, pb)
            if None in tile:
                row = lax.broadcasted_iota(I32, cand.shape, 0)
                pad = functools.reduce(jnp.logical_or, [row == r for r, p in enumerate(tile) if p is None])
                cand = jnp.where(pad, -jnp.inf, cand)
            c_scr[h, t * SUBLANES:(t + 1) * SUBLANES, :] = cand
            e_scr[h, t * SUBLANES:(t + 1) * SUBLANES, :] = _pick_rows(i1, pa) + _pick_rows(i2, pb)
    cid = cid_ref[...]
    ncand = cid.shape[0]

    def top_cands(i, carry):
        for h in range(PEER_HEADS):
            cand = c_scr[h]
            m = jnp.max(cand, axis=0, keepdims=True)
            c = jnp.min(jnp.where(cand == m, cid, k * k), axis=0, keepdims=True)
            hit = cid == c
            e = jnp.max(jnp.where(hit, e_scr[h], -1), axis=0, keepdims=True)
            c_scr[h] = jnp.where(hit, -jnp.inf, cand)
            ts_scr[pl.ds(h * k + i, 1), :] = m
            ex_scr[pl.ds(h * k + i, 1), :] = e * idx_scale
        return carry

    lax.fori_loop(0, k, top_cands, 0)
    idx_o[...] = ex_scr[...].T
    for h in range(PEER_HEADS):
        ts = ts_scr[h * k:(h + 1) * k, :]
        ex = jnp.exp(ts - jnp.max(ts, axis=0, keepdims=True))
        gate_o[h * k:(h + 1) * k, :] = ex / jnp.sum(ex, axis=0, keepdims=True)


def _peer_route(h, consts, pairs, idx_scale):
    n, d = h.shape
    tm = CHUNK
    nsel = PEER_HEADS * PEER_TOPK
    ncand = consts[-1].shape[0]
    rows = pl.BlockSpec((tm, d), lambda i: (i, 0))
    sel = pl.BlockSpec((nsel, tm), lambda i: (0, i))
    return pl.pallas_call(
        functools.partial(_peer_route_kernel, pairs=pairs, idx_scale=idx_scale),
        grid=(n // tm,),
        in_specs=[rows] + [_full(a.shape) for a in consts],
        out_specs=[rows, pl.BlockSpec((tm, nsel), lambda i: (i, 0)), sel],
        out_shape=[jax.ShapeDtypeStruct((n, d), F32), jax.ShapeDtypeStruct((n, nsel), I32),
                   jax.ShapeDtypeStruct((nsel, n), F32)],
        scratch_shapes=[pltpu.VMEM((2 * PEER_HEADS, N_KEYS, tm), F32),
                        pltpu.VMEM((2 * PEER_HEADS, PEER_TOPK, tm), F32),
                        pltpu.VMEM((2 * PEER_HEADS, PEER_TOPK, tm), I32),
                        pltpu.VMEM((PEER_HEADS, ncand, tm), F32),
                        pltpu.VMEM((PEER_HEADS, ncand, tm), I32),
                        pltpu.VMEM((nsel, tm), F32),
                        pltpu.VMEM((nsel, tm), I32)],
        compiler_params=_params(("parallel",)),
        name="peer_route",
    )(h, *consts)


PEER_GROUP = 16
PEER_SLOTS = 2
WORDS = 2


def _pack_kernel(t_ref, o_ref):
    r = t_ref.shape[0]
    o_ref[...] = pltpu.bitcast(t_ref[...].reshape(r * SUBLANES, LANES), I32)


def _pack_table(t):
    n, d = t.shape
    rows = d // LANES
    blk = 512
    t3 = t.astype(BF16).reshape(n, rows, LANES)
    return pl.pallas_call(
        _pack_kernel,
        grid=(n // blk,),
        in_specs=[pl.BlockSpec((blk, rows, LANES), lambda i: (i, 0, 0))],
        out_specs=pl.BlockSpec((blk * rows // WORDS, LANES), lambda i: (i, 0)),
        out_shape=jax.ShapeDtypeStruct((n * rows // WORDS, LANES), I32),
        compiler_params=_params(("parallel",)),
        name="pack_table",
    )(t3)


def _for_each_token(idx_ref, ids, sem, tm, token_fn):
    ngrp = tm // PEER_GROUP

    def copy(grp, slot):
        return pltpu.make_async_copy(idx_ref.at[pl.ds(grp * PEER_GROUP, PEER_GROUP)], ids.at[slot], sem.at[slot])

    for slot in range(PEER_SLOTS):
        copy(slot, slot).start()

    def body(q, carry):
        for slot in range(PEER_SLOTS):
            grp = q * PEER_SLOTS + slot
            copy(grp, slot).wait()
            for s in range(PEER_GROUP):
                token_fn(grp * PEER_GROUP + s, slot, s)

            @pl.when(grp + PEER_SLOTS < ngrp)
            def _():
                copy(grp + PEER_SLOTS, slot).start()
        return carry

    lax.fori_loop(0, ngrp // PEER_SLOTS, body, 0)


def _gather_rows(ids, tab_ref, gbuf, slot, s, nsel):
    rows_per = SUBLANES // WORDS
    for j in range(nsel):
        e = pl.multiple_of(ids[slot, s, j], rows_per)
        gbuf[s, j * rows_per:(j + 1) * rows_per, :] = tab_ref[pl.ds(e, rows_per), :]
    return pltpu.bitcast(gbuf[s], BF16)


def _diag_mask(nsel):
    sub = lax.broadcasted_iota(I32, (SUBLANES, nsel * SUBLANES), 0)
    lane = lax.broadcasted_iota(I32, (SUBLANES, nsel * SUBLANES), 1)
    return sub == lane % SUBLANES


def _gelu(x):
    return 0.5 * x * (1.0 + lax.erf(x * np.float32(1.0 / np.sqrt(2.0))))


def _peer_u_kernel(idx_ref, tok_ref, gate_ref, tab_ref, eexp_ref, act_o, gbuf, rsum, ids, sem, *, tm, nsel):
    diag = _diag_mask(nsel)

    def token(i, slot, s):
        rows = _gather_rows(ids, tab_ref, gbuf, slot, s, nsel)
        tok = tok_ref[i].astype(BF16)
        prod = _dot_nt(tok, rows)
        rsum[pl.ds(i, 1), :] = jnp.sum(jnp.where(diag, prod, 0.0), axis=0, keepdims=True)

    _for_each_token(idx_ref, ids, sem, tm, token)
    eexp = eexp_ref[...]
    hi, mid, lo = _split3(rsum[...])
    pre_t = _dot_nt(eexp, hi) + _dot_nt(eexp, mid) + _dot_nt(eexp, lo)
    act_o[...] = _gelu(pre_t) * gate_ref[...]


def _peer_v_kernel(idx_ref, act_ref, h_ref, g_ref, tab_ref, eexp_ref, out_o, gbuf, arep, ids, sem, *, tm, nsel):
    diag = _diag_mask(nsel)
    arep[...] = _dot(act_ref[...].T.astype(BF16), eexp_ref[...])

    def token(i, slot, s):
        rows = _gather_rows(ids, tab_ref, gbuf, slot, s, nsel)
        a = jnp.broadcast_to(arep[pl.ds(i, 1), :], diag.shape)
        out_o[i] = h_ref[i] + _dot(jnp.where(diag, a, 0.0).astype(BF16), rows)

    _for_each_token(idx_ref, ids, sem, tm, token)
    x = out_o[...]
    ms = jnp.sum(jnp.sum(x * x, axis=2, keepdims=True), axis=1, keepdims=True) * (1.0 / (SUBLANES * LANES))
    out_o[...] = x * lax.rsqrt(ms + RMS_EPS) * g_ref[...]


def _peer_gather_specs(tm, nsel, tab):
    idx = pl.BlockSpec((tm, nsel), lambda i: (i, 0))
    table = pl.BlockSpec(tab.shape, lambda i: (0, 0), pipeline_mode=pl.Buffered(1))
    scratch = [pltpu.VMEM((PEER_GROUP, nsel * SUBLANES // WORDS, LANES), I32),
               pltpu.VMEM((tm, nsel * SUBLANES), F32),
               pltpu.SMEM((PEER_SLOTS, PEER_GROUP, nsel), I32),
               pltpu.SemaphoreType.DMA((PEER_SLOTS,))]
    return idx, table, scratch


def _peer_u(idx, tok3, gate, tab, eexp, *, tm):
    n, nsel = idx.shape
    assert tm % (PEER_GROUP * PEER_SLOTS) == 0
    idx_spec, tab_spec, scratch = _peer_gather_specs(tm, nsel, tab)
    sel = pl.BlockSpec((nsel, tm), lambda i: (0, i))
    return pl.pallas_call(
        functools.partial(_peer_u_kernel, tm=tm, nsel=nsel),
        grid=(n // tm,),
        in_specs=[idx_spec, pl.BlockSpec((tm, SUBLANES, LANES), lambda i: (i, 0, 0)), sel, tab_spec,
                  _full(eexp.shape)],
        out_specs=sel,
        out_shape=jax.ShapeDtypeStruct((nsel, n), F32),
        scratch_shapes=scratch,
        compiler_params=_params(("arbitrary",)),
        name="peer_u",
    )(idx, tok3, gate, tab, eexp)


def _peer_v(idx, act, h3, gain, tab, eexp, *, tm):
    n, nsel = idx.shape
    assert tm % (PEER_GROUP * PEER_SLOTS) == 0
    idx_spec, tab_spec, scratch = _peer_gather_specs(tm, nsel, tab)
    tiles = pl.BlockSpec((tm, SUBLANES, LANES), lambda i: (i, 0, 0))
    return pl.pallas_call(
        functools.partial(_peer_v_kernel, tm=tm, nsel=nsel),
        grid=(n // tm,),
        in_specs=[idx_spec, pl.BlockSpec((nsel, tm), lambda i: (0, i)), tiles, _full(gain.shape), tab_spec,
                  _full(eexp.shape)],
        out_specs=tiles,
        out_shape=jax.ShapeDtypeStruct((n, SUBLANES, LANES), F32),
        scratch_shapes=scratch,
        compiler_params=_params(("arbitrary",)),
        name="peer_v",
    )(idx, act, h3, gain, tab, eexp)


def _hi_lo(w):
    hi = w.astype(BF16)
    return hi, (w - hi.astype(F32)).astype(BF16)


def _block_ones(n, blk):
    i = np.arange(n)
    return jnp.asarray(i[:, None] // blk == i[None, :] // blk, dtype=BF16)


def _stream(x_rows, lead, conv0, ssd0, shift0, wkv0, w, *, nseq, nblk, lb, has_lead, npad, tb):
    d = x_rows.shape[1]
    nblk_all = nblk + (1 if has_lead else 0)
    width_a = w["ssm_norm"].shape[1]
    width_b = w["ones_b"].shape[0]
    heads_b = width_b // RWKV_HEAD_DIM
    if lb == CHUNK:
        xbc, z, dtp, rw = _proj_in(x_rows, lead, w["norm_mix"], w["w_in"], w["widths"],
                                   nseq=nseq, nblk=nblk_all, has_lead=has_lead)
        nchunk = nblk_all
    else:
        xbc, z, dtp, rw = _proj_in(x_rows, lead, w["norm_mix"], w["w_in"], w["widths"],
                                   nseq=1, nblk=x_rows.shape[0] // CHUNK, has_lead=False)
        nchunk = 1
    ya, conv_new, ssd_new = _ssd(xbc, z, dtp, conv0, ssd0, w["ssd"], nseq=nseq, lb=lb, nchunk=nchunk, npad=npad)
    scan_kernels = lb == CHUNK and heads_b == SUBLANES and 2 * nseq * heads_b == LANES
    *scan_in, g, bonus, shift_new = _rwkv_pre(
        rw, shift0[:, None, :], w["rwkv_pre"], nseq=nseq, lb=lb, nchunk=nchunk,
        npad=npad if lb == CHUNK else 0, width=width_b, stacked=scan_kernels)

    t_all = nchunk * lb
    t0 = npad if lb == CHUNK else 0
    t_real = t_all - t0
    pairs = nseq * heads_b
    hd = RWKV_HEAD_DIM
    if pairs >= LANES:
        ngroup, dup = pairs // LANES, 1
    else:
        ngroup, dup = 1, LANES // pairs
    seq_per = nseq // ngroup
    vr = hd // dup

    def to_scan_k(a):
        a = a.reshape(ngroup, seq_per, t_all, heads_b, hd)[:, :, t0:]
        a = a.transpose(0, 2, 4, 1, 3).reshape(ngroup, t_real, hd, seq_per * heads_b)
        return jnp.tile(a, (1, 1, 1, dup))

    def to_scan_v(a):
        a = a.reshape(ngroup, seq_per, t_all, heads_b, dup, vr)[:, :, t0:]
        return a.transpose(0, 2, 5, 4, 1, 3).reshape(ngroup, t_real, vr, LANES)

    s0 = wkv0.reshape(ngroup, seq_per, heads_b, dup, vr, hd).transpose(0, 5, 4, 3, 1, 2)
    s0 = s0.reshape(ngroup, hd, vr, LANES)
    if scan_kernels:
        ks, vs = _to_scan(scan_in[0], heads=heads_b, dup=dup, tb=tb)
        ysc, s_new = _rwkv_scan_kmajor(ks, vs, s0, tb=tb, t0=t0)
    else:
        ysc, s_new = _rwkv_scan(*[to_scan_k(a) for a in scan_in[:5]], to_scan_v(scan_in[5]), s0, tb=tb)
    wkv_new = s_new.reshape(ngroup, hd, vr, dup, seq_per, heads_b).transpose(0, 4, 5, 3, 2, 1)
    wkv_new = wkv_new.reshape(nseq, heads_b, hd, hd)
    skip_t = t_real - nblk * lb if lb == CHUNK else 0
    ys = ysc[:, skip_t:].reshape(ngroup, t_real - skip_t, vr, dup, seq_per, heads_b)
    ys = ys.transpose(0, 4, 1, 5, 3, 2).reshape(nseq * (t_real - skip_t), width_b)

    if lb == CHUNK:
        h1 = _mix_out(ys, bonus, g, ya, x_rows, w["mix_out"], nseq=nseq, nblk=nblk, skip=nchunk - nblk)
    else:
        h1 = _mix_out(ys, bonus, g, ya, x_rows, w["mix_out"], nseq=1, nblk=x_rows.shape[0] // CHUNK, skip=0)

    u, idx, gate = _peer_route(h1, w["route"], w["cand_pairs"], SUBLANES // WORDS)
    n = u.shape[0]
    act = _peer_u(idx, u.reshape(n, SUBLANES, LANES), gate, w["tab_u"], w["eexp"], tm=w["peer_tm"])
    y = _peer_v(idx, act, h1.reshape(n, SUBLANES, LANES), w["norm_final"], w["tab_v"], w["eexp"], tm=w["peer_tm"])
    return y.reshape(n, d), conv_new, ssd_new, shift_new[:, 0, :], wkv_new


def kernel(x_prompt, x_sample, state_conv, state_ssd, state_shift, state_wkv, meta_tokens, norm_mix, w_in, conv_w, conv_b, dt_bias, a_log, d_skip, ssm_norm, shift_mu, decay_w0, decay_w2, iclr_a0, iclr_a2, gate_g2, k_k, k_a, r_k, lnx_w, lnx_b, w_out, norm_ffn, w_query, sub_keys, expert_u, expert_v, norm_final):
    bp, seq_p, d = x_prompt.shape
    bs, seq_s, _ = x_sample.shape
    depth = w_in.shape[0]
    assert depth == 1 and seq_p % CHUNK == 0 and (bs * seq_s) % CHUNK == 0 and seq_s % SUBLANES == 0
    assert d == SUBLANES * LANES
    heads_a = state_ssd.shape[2]
    width_a = heads_a * SSD_HEAD_DIM
    conv_dim = state_conv.shape[3]
    rw_in = state_shift.shape[2]
    heads_b = state_wkv.shape[2]
    width_b = heads_b * RWKV_HEAD_DIM
    assert heads_a <= LANES and rw_in == 3 * width_b + DECAY_LORA + AAA_LORA + GATE_LORA

    wi = w_in[0]
    ssd_in = width_a + conv_dim + heads_a
    w_cat = jnp.concatenate([
        wi[:, width_a:width_a + conv_dim], wi[:, :width_a],
        jnp.pad(wi[:, width_a + conv_dim:ssd_in], ((0, 0), (0, LANES - heads_a))),
        wi[:, ssd_in:]], axis=1).astype(BF16)
    widths = (conv_dim, width_a, LANES, rw_in)
    pad_h = lambda a: jnp.pad(a.reshape(1, -1), ((0, 0), (0, LANES - heads_a)))
    i_l = np.arange(CHUNK)
    tri = jnp.asarray(i_l[:, None] >= i_l[None, :], dtype=BF16)
    ehead = jnp.asarray(np.arange(LANES)[:, None] == np.arange(width_a)[None, :] // SSD_HEAD_DIM, dtype=BF16)
    ecol = jnp.asarray(np.arange(LANES)[:, None] == np.arange(heads_a * CHUNK)[None, :] // CHUNK, dtype=BF16)
    ssd_consts = (conv_w[0], conv_b[0].reshape(1, -1), pad_h(dt_bias[0]), pad_h(a_log[0]),
                  jnp.repeat(d_skip[0], SSD_HEAD_DIM).reshape(1, -1), ssm_norm[0].reshape(1, -1), tri, ehead, ecol)
    ones_b = _block_ones(width_b, RWKV_HEAD_DIM)
    zero = jnp.zeros((DECAY_LORA, width_b), F32)
    w_wa = jnp.concatenate([jnp.concatenate([decay_w2[0], zero], axis=1),
                            jnp.concatenate([jnp.zeros((AAA_LORA, width_b), F32), iclr_a2[0]], axis=1)], axis=0)
    row = lambda a: a.reshape(1, -1)
    pre_consts = (row(shift_mu[0]), row(decay_w0[0]), row(iclr_a0[0]), *_hi_lo(w_wa), *_hi_lo(gate_g2[0]),
                  row(k_k[0]), row(k_a[0]), row(r_k[0]), ones_b)
    wo = w_out[0].astype(BF16)
    mix_consts = (row(lnx_w[0]), row(lnx_b[0]), ones_b, wo[:width_a], wo[width_a:])
    cand_pairs, cand_ids = _cand_rows()
    keys = sub_keys[0].transpose(1, 0, 2, 3).reshape(2 * PEER_HEADS, N_KEYS, -1)
    route_consts = (row(norm_ffn[0]), w_query[0].astype(BF16), *_hi_lo(keys),
                    jnp.asarray(np.broadcast_to(cand_ids[:, None], (cand_ids.shape[0], CHUNK))))
    nsel = PEER_HEADS * PEER_TOPK
    lane8 = np.arange(nsel * SUBLANES)
    eexp = jnp.asarray(np.arange(nsel)[:, None] == lane8[None, :] // SUBLANES, dtype=BF16)
    w = dict(norm_mix=row(norm_mix[0]), w_in=w_cat, widths=widths, ssm_norm=row(ssm_norm[0]), ones_b=ones_b,
             ssd=ssd_consts, rwkv_pre=pre_consts, mix_out=mix_consts, route=route_consts,
             tab_u=_pack_table(expert_u[0]), tab_v=_pack_table(expert_v[0]), eexp=eexp, cand_pairs=cand_pairs,
             norm_final=norm_final.reshape(SUBLANES, LANES), peer_tm=CHUNK)

    npad = CHUNK - N_META
    lead = jnp.concatenate([jnp.zeros((npad, d), F32), meta_tokens.astype(F32)], axis=0)
    zeros = lambda *s: jnp.zeros(s, F32)
    yp, cp, sp, shp, wp = _stream(
        x_prompt.reshape(bp * seq_p, d), lead,
        zeros(bp, CONV_W - 1, conv_dim), zeros(bp, heads_a, SSD_HEAD_DIM, SSD_STATE), zeros(bp, rw_in),
        zeros(bp, heads_b, RWKV_HEAD_DIM, RWKV_HEAD_DIM), w,
        nseq=bp, nblk=seq_p // CHUNK, lb=CHUNK, has_lead=True, npad=npad, tb=N_META)
    ys, cs, ss, shs, ws = _stream(
        x_sample.reshape(bs * seq_s, d), lead, state_conv[0], state_ssd[0], state_shift[0], state_wkv[0], w,
        nseq=bs, nblk=1, lb=seq_s, has_lead=False, npad=CHUNK - seq_s, tb=seq_s)
    return (yp.reshape(bp, seq_p, d), ys.reshape(bs, seq_s, d), cp[None], sp[None], shp[None], wp[None],
            cs[None], ss[None], shs[None], ws[None])
```

```python
import functools

import jax
import jax.numpy as jnp
import numpy as np
from jax import lax
from jax.experimental import pallas as pl
from jax.experimental.pallas import tpu as pltpu

F32 = jnp.float32
BF16 = jnp.bfloat16
I32 = jnp.int32

N_META = 16
SSD_HEAD_DIM = 64
SSD_GROUPS = 2
SSD_STATE = 128
CONV_W = 4
RWKV_HEAD_DIM = 64
DECAY_LORA = 64
AAA_LORA = 64
GATE_LORA = 128
PEER_HEADS = 8
N_KEYS = 128
PEER_TOPK = 16
RMS_EPS = 1e-6
GN_EPS = 64e-5

LANES = 128
SUBLANES = 8
CHUNK = 128
VMEM_LIMIT_BYTES = 56 * 1024 * 1024


def _full(shape):
    zeros = (0,) * len(shape)
    return pl.BlockSpec(shape, lambda *_: zeros)


def _params(semantics, vmem=VMEM_LIMIT_BYTES):
    return pltpu.CompilerParams(dimension_semantics=semantics, vmem_limit_bytes=vmem)


def _split2(x):
    hi = x.astype(BF16)
    lo = (x - hi.astype(F32)).astype(BF16)
    return hi, lo


def _split3(x):
    hi = x.astype(BF16)
    r = x - hi.astype(F32)
    mid = r.astype(BF16)
    lo = (r - mid.astype(F32)).astype(BF16)
    return hi, mid, lo


def _dot(a, b):
    return jnp.dot(a, b, preferred_element_type=F32)


def _dot_nt(a, b):
    return lax.dot_general(a, b, (((1,), (1,)), ((), ())), preferred_element_type=F32)


def _dot_sel(x, sel):
    hi, mid, lo = _split3(x)
    return _dot(hi, sel) + _dot(mid, sel) + _dot(lo, sel)


def _dot_hp(x, w_hi, w_lo):
    hi, lo = _split2(x)
    return _dot(hi, w_hi) + (_dot(lo, w_hi) + _dot(hi, w_lo))


def _dot_hp_nt(x, w_hi, w_lo):
    hi, lo = _split2(x)
    return _dot_nt(hi, w_hi) + (_dot_nt(lo, w_hi) + _dot_nt(hi, w_lo))


def _silu(x):
    return x * jax.nn.sigmoid(x)


def _softplus(x):
    return jnp.maximum(x, 0.0) + jnp.log1p(jnp.exp(-jnp.abs(x)))


def _rmsnorm(x, g):
    ms = jnp.mean(x * x, axis=-1, keepdims=True)
    return x * lax.rsqrt(ms + RMS_EPS) * g


def _proj_in_kernel(x_ref, lead_ref, g_ref, w_ref, xbc_ref, z_ref, dt_ref, rw_ref, *, widths, has_lead):
    x = x_ref[...]
    if has_lead:
        x = jnp.where(pl.program_id(1) == 0, lead_ref[...], x)
    u = _rmsnorm(x, g_ref[...]).astype(BF16)
    p = _dot(u, w_ref[...])
    off = 0
    for ref, w in zip((xbc_ref, z_ref, dt_ref, rw_ref), widths):
        ref[...] = p[:, off:off + w]
        off += w


def _proj_in(rows, lead, g, w_cat, widths, *, nseq, nblk, has_lead):
    d = rows.shape[1]
    nblk_in = nblk - 1 if has_lead else nblk
    shift = 1 if has_lead else 0
    x_map = lambda s, j: (s * nblk_in + jnp.maximum(j - shift, 0), 0)
    o_map = lambda s, j: (s * nblk + j, 0)
    return pl.pallas_call(
        functools.partial(_proj_in_kernel, widths=widths, has_lead=has_lead),
        grid=(nseq, nblk),
        in_specs=[pl.BlockSpec((CHUNK, d), x_map), _full(lead.shape), _full(g.shape), _full(w_cat.shape)],
        out_specs=[pl.BlockSpec((CHUNK, w), o_map) for w in widths],
        out_shape=[jax.ShapeDtypeStruct((nseq * nblk * CHUNK, w), F32) for w in widths],
        compiler_params=_params(("parallel", "parallel")),
        name="proj_in",
    )(rows, lead, g, w_cat)


def _cumsum_rows(x, tri_ref):
    hi, mid, lo = _split3(x)
    tri = tri_ref[...]
    return _dot(tri, hi) + _dot(tri, mid) + _dot(tri, lo)


def _ssd_kernel(xbc_ref, z_ref, dt_ref, cs_ref, h0_ref, cw_ref, cb_ref, dtb_ref, alog_ref,
                dskip_ref, norm_ref, tri_ref, ehead_ref, ecol_ref,
                y_ref, cs_out_ref, h_out_ref, xfull, dtile, *, lb, npad, heads, width):
    c = pl.program_id(1)
    nc = pl.num_programs(1)
    L = CHUNK
    prev = CONV_W - 1
    base = SUBLANES
    state = SSD_STATE
    hd = SSD_HEAD_DIM
    per_group = heads // SSD_GROUPS

    @pl.when(c == 0)
    def _():
        h_out_ref[...] = h0_ref[...]
        if lb == L:
            xfull[base - prev:base, :] = cs_ref[0]

    if lb == L:
        xfull[base:base + L, :] = xbc_ref[...]
        dt_raw = dt_ref[...]
    else:
        xfull[...] = jnp.zeros_like(xfull)
        xfull[base + L - lb - prev:base + L - lb, :] = cs_ref[0]
        xfull[base + L - lb:base + L, :] = xbc_ref[...]
        dtile[...] = jnp.zeros_like(dtile)
        dtile[L - lb:L, :] = dt_ref[...]
        dt_raw = dtile[...]

    conv = cb_ref[...] + xfull[base - prev:base - prev + L, :] * cw_ref[0:1, :]
    for k in range(1, CONV_W):
        conv = conv + xfull[base - prev + k:base - prev + k + L, :] * cw_ref[k:k + 1, :]
    cs_new = xfull[base + L - prev:base + L, :]

    @pl.when(c == nc - 1)
    def _():
        cs_out_ref[0] = cs_new

    xfull[base - prev:base, :] = cs_new

    xbc = _silu(conv)
    xs = xbc[:, :width]
    gn = SSD_GROUPS * state
    row = lax.broadcasted_iota(I32, (L, LANES), 0) + c * L
    dt = jnp.where(row >= npad, _softplus(dt_raw + dtb_ref[...]), 0.0)
    da = dt * (-jnp.exp(alog_ref[...]))
    acum = _cumsum_rows(da, tri_ref)
    acum_t = acum.T
    ehead = ehead_ref[...]
    acum_x = _dot_sel(acum, ehead)
    dt_x = _dot_sel(dt, ehead)
    last_x = acum_x[L - 1:L, :]
    xdt = xs * dt_x
    wend = xdt * jnp.exp(last_x - acum_x)
    colb = _dot_sel(acum, ecol_ref[...])
    causal = (lax.broadcasted_iota(I32, (L, L), 0) >= lax.broadcasted_iota(I32, (L, L), 1))
    lane = lax.broadcasted_iota(I32, (L, LANES), 1)

    y_diag = []
    y_off = []
    for g in range(SSD_GROUPS):
        bm = xbc[:, width + g * state:width + (g + 1) * state].astype(BF16)
        cm = xbc[:, width + gn + g * state:width + gn + (g + 1) * state].astype(BF16)
        cb = _dot_nt(cm, bm)
        h_prev = h_out_ref[0, g * per_group:(g + 1) * per_group].reshape(per_group * hd, state)
        y_off.append(_dot_nt(cm, h_prev.astype(BF16)))
        for pair in range(per_group // 2):
            h_a = g * per_group + 2 * pair
            xp = xdt[:, h_a * hd:(h_a + 2) * hd].astype(BF16)
            outs = []
            for h in (h_a, h_a + 1):
                seg = colb[:, h * L:(h + 1) * L] - acum_t[h:h + 1, :]
                dec = jnp.exp(jnp.where(causal, seg, -jnp.inf))
                outs.append(_dot((cb * dec).astype(BF16), xp))
            y_diag.append(jnp.where(lane < hd, outs[0], outs[1]))
        wg_t = wend[:, g * per_group * hd:(g + 1) * per_group * hd].T.astype(BF16)
        upd = _dot(wg_t, bm)
        for i in range(per_group):
            h = g * per_group + i
            cd = jnp.exp(acum_t[h:h + 1, L - 1:L])
            h_out_ref[0, h] = h_out_ref[0, h] * cd + upd[i * hd:(i + 1) * hd, :]
    y = (jnp.concatenate(y_diag, axis=1) + jnp.concatenate(y_off, axis=1) * jnp.exp(acum_x)
         + xs * dskip_ref[...])
    y = y[L - lb:, :] * _silu(z_ref[...])
    y_ref[...] = _rmsnorm(y, norm_ref[...])


def _ssd(xbc, z, dtp, cs, h0, consts, *, nseq, lb, nchunk, npad):
    heads = h0.shape[1]
    width = z.shape[1]
    conv_dim = xbc.shape[1]
    row_map = lambda s, c: (s * nchunk + c, 0)
    kernel = functools.partial(_ssd_kernel, lb=lb, npad=npad, heads=heads, width=width)
    return pl.pallas_call(
        kernel,
        grid=(nseq, nchunk),
        in_specs=[pl.BlockSpec((lb, conv_dim), row_map),
                  pl.BlockSpec((lb, width), row_map),
                  pl.BlockSpec((lb, LANES), row_map),
                  pl.BlockSpec((1,) + cs.shape[1:], lambda s, c: (s, 0, 0)),
                  pl.BlockSpec((1,) + h0.shape[1:], lambda s, c: (s, 0, 0, 0))]
                 + [_full(a.shape) for a in consts],
        out_specs=[pl.BlockSpec((lb, width), row_map),
                   pl.BlockSpec((1,) + cs.shape[1:], lambda s, c: (s, 0, 0)),
                   pl.BlockSpec((1,) + h0.shape[1:], lambda s, c: (s, 0, 0, 0))],
        out_shape=[jax.ShapeDtypeStruct((nseq * nchunk * lb, width), F32),
                   jax.ShapeDtypeStruct(cs.shape, F32),
                   jax.ShapeDtypeStruct(h0.shape, F32)],
        scratch_shapes=[pltpu.VMEM((SUBLANES + CHUNK, conv_dim), F32), pltpu.VMEM((CHUNK, LANES), F32)],
        compiler_params=_params(("parallel", "arbitrary")),
        name="ssd",
    )(xbc, z, dtp, cs, h0, *consts)


def _rwkv_pre_kernel(rw_ref, sh_ref, mu_ref, w0_ref, a0_ref, wa_hi_ref, wa_lo_ref, g2_hi_ref, g2_lo_ref,
                     kk_ref, ka_ref, rk_ref, ones_ref,
                     *refs, lb, npad, width, stacked):
    if stacked:
        sc_o, g_o, bonus_o, sh_o, pfull = refs
    else:
        kn_o, d_o, b_o, k_o, r_o, v_o, g_o, bonus_o, sh_o, pfull = refs
    c = pl.program_id(1)
    nc = pl.num_programs(1)
    base = SUBLANES
    p = rw_ref[...]

    @pl.when(c == 0)
    def _():
        pfull[base - 1:base, :] = sh_ref[0]

    pfull[base:base + lb, :] = p
    prev = pfull[base - 1:base - 1 + lb, :]
    last = p[lb - 1:lb, :]
    pfull[base - 1:base, :] = last

    @pl.when(c == nc - 1)
    def _():
        sh_o[0] = last

    pm = p + (prev - p) * mu_ref[...]
    r = pm[:, :width]
    k = pm[:, width:2 * width]
    v = pm[:, 2 * width:3 * width]
    lora_in = pm[:, 3 * width:3 * width + DECAY_LORA + AAA_LORA]
    lane = lax.broadcasted_iota(I32, lora_in.shape, 1)
    lora_in = jnp.where(lane < DECAY_LORA, jnp.tanh(lora_in), lora_in)
    lora = _dot_hp(lora_in, wa_hi_ref[...], wa_lo_ref[...])
    gate_in = jax.nn.sigmoid(pm[:, 3 * width + DECAY_LORA + AAA_LORA:])
    g_o[...] = _dot_hp(gate_in, g2_hi_ref[...], g2_lo_ref[...])
    w = -_softplus(-(w0_ref[...] + lora[:, :width])) - 0.5
    row = lax.broadcasted_iota(I32, w.shape, 0) + c * lb
    dcy = jnp.where(row >= npad, jnp.exp(-jnp.exp(w)), 1.0)
    a = jax.nn.sigmoid(a0_ref[...] + lora[:, width:])
    ones = ones_ref[...]
    kn = k * kk_ref[...]
    kn = kn / jnp.maximum(jnp.sqrt(_dot_sel(kn * kn, ones)), 1e-12)
    kp = k * (1.0 + (a - 1.0) * ka_ref[...])
    bonus_o[...] = _dot_sel(r * kp * rk_ref[...], ones) * v
    scan_in = (kn, dcy, kn * a, kp, r, v)
    if stacked:
        for j, x in enumerate(scan_in):
            sc_o[j, 0] = x.T
    else:
        for ref, x in zip((kn_o, d_o, b_o, k_o, r_o, v_o), scan_in):
            ref[...] = x


def _rwkv_pre(rw, sh, consts, *, nseq, lb, nchunk, npad, width, stacked):
    rw_in = rw.shape[1]
    row_map = lambda s, c: (s * nchunk + c, 0)
    n = nseq * nchunk * lb
    rows = pl.BlockSpec((lb, width), row_map)
    if stacked:
        scan_specs = [pl.BlockSpec((6, 1, width, lb), lambda s, c: (0, s, 0, c))]
        scan_shapes = [jax.ShapeDtypeStruct((6, nseq, width, nchunk * lb), F32)]
    else:
        scan_specs = [rows] * 6
        scan_shapes = [jax.ShapeDtypeStruct((n, width), F32)] * 6
    kernel = functools.partial(_rwkv_pre_kernel, lb=lb, npad=npad, width=width, stacked=stacked)
    return pl.pallas_call(
        kernel,
        grid=(nseq, nchunk),
        in_specs=[pl.BlockSpec((lb, rw_in), row_map), pl.BlockSpec((1, 1, rw_in), lambda s, c: (s, 0, 0))]
                 + [_full(a.shape) for a in consts],
        out_specs=scan_specs + [rows, rows, pl.BlockSpec((1, 1, rw_in), lambda s, c: (s, 0, 0))],
        out_shape=scan_shapes + [jax.ShapeDtypeStruct((n, width), F32)] * 2 + [jax.ShapeDtypeStruct(sh.shape, F32)],
        scratch_shapes=[pltpu.VMEM((SUBLANES + lb, rw_in), F32)],
        compiler_params=_params(("parallel", "arbitrary")),
        name="rwkv_pre",
    )(rw, sh, *consts)


def _to_scan_k_kernel(x_ref, o_ref, *, nseq, heads, hd, dup):
    nblk, _, tb, _ = o_ref.shape[1:]
    for k in range(hd):
        rows = jnp.concatenate([x_ref[0, s, pl.ds(k, heads, stride=hd), :] for s in range(nseq)], axis=0)
        o_ref[0, :, k] = jnp.concatenate([rows] * dup, axis=0).T.reshape(nblk, tb, LANES)


def _to_scan_v_kernel(x_ref, o_ref, *, nseq, heads, hd, dup):
    vr = hd // dup
    for v in range(vr):
        rows = jnp.concatenate([x_ref[0, s, pl.ds(p * vr + v, heads, stride=hd), :]
                                for p in range(dup) for s in range(nseq)], axis=0)
        o_ref[:, v, :] = rows.T


def _to_scan(sc, *, heads, dup, tb):
    _, nseq, width, t_all = sc.shape
    hd = width // heads
    kw = dict(nseq=nseq, heads=heads, hd=hd, dup=dup)
    ks = pl.pallas_call(
        functools.partial(_to_scan_k_kernel, **kw),
        grid=(5, t_all // CHUNK),
        in_specs=[pl.BlockSpec((1, nseq, width, CHUNK), lambda j, c: (j, 0, 0, c))],
        out_specs=pl.BlockSpec((1, CHUNK // tb, hd, tb, LANES), lambda j, c: (j, c, 0, 0, 0)),
        out_shape=jax.ShapeDtypeStruct((5, t_all // tb, hd, tb, LANES), F32),
        compiler_params=_params(("parallel", "parallel")),
        name="to_scan_k",
    )(sc)
    vs = pl.pallas_call(
        functools.partial(_to_scan_v_kernel, **kw),
        grid=(t_all // CHUNK,),
        in_specs=[pl.BlockSpec((1, nseq, width, CHUNK), lambda c: (5, 0, 0, c))],
        out_specs=pl.BlockSpec((CHUNK, hd // dup, LANES), lambda c: (c, 0, 0)),
        out_shape=jax.ShapeDtypeStruct((t_all, hd // dup, LANES), F32),
        compiler_params=_params(("parallel",)),
        name="to_scan_v",
    )(sc)
    return ks, vs


SCAN_PIECE = 32
SCAN_UNROLL = 8


def _rwkv_scan_kernel(kn_ref, d_ref, b_ref, k_ref, r_ref, v_ref, s0_ref, y_ref, s_ref, *, tb, hd, kmajor):
    @pl.when(pl.program_id(1) == 0)
    def _():
        s_ref[...] = s0_ref[...]

    vr = y_ref.shape[2]

    def krow(ref, t, k):
        return ref[0, 0, k, pl.ds(t, 1), :] if kmajor else ref[0, t, pl.ds(k, 1), :]

    def one_step(t):
        piece = min(SCAN_PIECE, vr)
        for p in range(vr // piece):
            rows = slice(p * piece, (p + 1) * piece)
            v_t = v_ref[t, rows, :] if kmajor else v_ref[0, t, rows, :]
            acc = [jnp.zeros_like(v_t), jnp.zeros_like(v_t)]
            for k in range(hd):
                acc[k % 2] = acc[k % 2] + s_ref[0, k, rows, :] * krow(kn_ref, t, k)
            sa = acc[0] + acc[1]
            acc = [jnp.zeros_like(v_t), jnp.zeros_like(v_t)]
            for k in range(hd):
                sk = s_ref[0, k, rows, :] * krow(d_ref, t, k) - sa * krow(b_ref, t, k) + v_t * krow(k_ref, t, k)
                s_ref[0, k, rows, :] = sk
                acc[k % 2] = acc[k % 2] + sk * krow(r_ref, t, k)
            y_ref[0, t, rows, :] = acc[0] + acc[1]

    def steps(i, carry):
        base = pl.multiple_of(i * SCAN_UNROLL, SCAN_UNROLL)
        for j in range(SCAN_UNROLL):
            one_step(base + j)
        return carry

    assert tb % SCAN_UNROLL == 0
    lax.fori_loop(0, tb // SCAN_UNROLL, steps, 0)


def _scan_call(operands, in_specs, s0, *, ngroup, nsteps, tb, hd, vr, kmajor):
    vspec = pl.BlockSpec((1, tb, vr, LANES), lambda g, i: (g, i, 0, 0))
    sspec = pl.BlockSpec((1, hd, vr, LANES), lambda g, i: (g, 0, 0, 0))
    return pl.pallas_call(
        functools.partial(_rwkv_scan_kernel, tb=tb, hd=hd, kmajor=kmajor),
        grid=(ngroup, nsteps // tb),
        in_specs=in_specs + [sspec],
        out_specs=[vspec, sspec],
        out_shape=[jax.ShapeDtypeStruct((ngroup, nsteps, vr, LANES), F32), jax.ShapeDtypeStruct(s0.shape, F32)],
        compiler_params=_params(("parallel", "arbitrary")),
        name="rwkv_scan",
    )(*operands, s0)


def _rwkv_scan(kn, d, b, k, r, v, s0, *, tb):
    ngroup, t_total, hd, _ = kn.shape
    vr = v.shape[2]
    kspec = pl.BlockSpec((1, tb, hd, LANES), lambda g, i: (g, i, 0, 0))
    vspec = pl.BlockSpec((1, tb, vr, LANES), lambda g, i: (g, i, 0, 0))
    return _scan_call((kn, d, b, k, r, v), [kspec] * 5 + [vspec], s0, ngroup=ngroup, nsteps=t_total, tb=tb, hd=hd,
                      vr=vr, kmajor=False)


def _rwkv_scan_kmajor(ks, vs, s0, *, tb, t0):
    _, nblk, hd, tb_k, _ = ks.shape
    t_all, vr, _ = vs.shape
    assert t0 % tb == 0 and tb_k == tb and nblk * tb == t_all
    off = t0 // tb
    kspecs = [pl.BlockSpec((1, 1, hd, tb, LANES), lambda g, i, j=j: (j, i + off, 0, 0, 0)) for j in range(5)]
    vspec = pl.BlockSpec((tb, vr, LANES), lambda g, i: (i + off, 0, 0))
    return _scan_call((ks,) * 5 + (vs,), kspecs + [vspec], s0, ngroup=1, nsteps=t_all - t0, tb=tb, hd=hd, vr=vr,
                      kmajor=True)


def _mix_out_kernel(ys_ref, bonus_ref, g_ref, ya_ref, h_ref, lnw_ref, lnb_ref, ones_ref, wa_ref, wb_ref, o_ref):
    ones = ones_ref[...]
    inv = 1.0 / RWKV_HEAD_DIM
    y = ys_ref[...]
    yc = y - _dot_sel(y, ones) * inv
    var = _dot_sel(yc * yc, ones) * inv
    yb = (yc * lax.rsqrt(var + GN_EPS) * lnw_ref[...] + lnb_ref[...] + bonus_ref[...]) * g_ref[...]
    o_ref[...] = (h_ref[...] + _dot(ya_ref[...].astype(BF16), wa_ref[...])
                  + _dot(yb.astype(BF16), wb_ref[...]))


def _mix_out(ys, bonus, g, ya, h, consts, *, nseq, nblk, skip):
    width = ys.shape[1]
    d = h.shape[1]
    cmap = lambda i: (i, 0)
    pmap = lambda i: ((i // nblk) * (nblk + skip) + skip + i % nblk, 0)
    wide = pl.BlockSpec((CHUNK, width), pmap)
    return pl.pallas_call(
        _mix_out_kernel,
        grid=(nseq * nblk,),
        in_specs=[pl.BlockSpec((CHUNK, width), cmap), wide, wide, wide, pl.BlockSpec((CHUNK, d), cmap)]
                 + [_full(a.shape) for a in consts],
        out_specs=pl.BlockSpec((CHUNK, d), cmap),
        out_shape=jax.ShapeDtypeStruct(h.shape, F32),
        compiler_params=_params(("parallel",)),
        name="mix_out",
    )(ys, bonus, g, ya, h, *consts)


def _cand_rows():
    k = PEER_TOPK
    pairs = []
    for a in range(k):
        nb = k // (a + 1)
        width = -(-nb // SUBLANES) * SUBLANES if nb >= SUBLANES else 1 << (nb - 1).bit_length()
        pairs += [(a, b) for b in range(width)]
    pairs += [None] * (-len(pairs) % SUBLANES)
    ids = np.asarray([k * k if p is None else p[0] * k + p[1] for p in pairs], np.int32)
    return tuple(pairs), ids


def _pick_rows(x, picks):
    wanted = [p for p in picks if p is not None]
    if len(wanted) == SUBLANES and wanted == list(range(wanted[0], wanted[0] + SUBLANES)):
        return x[wanted[0]:wanted[0] + SUBLANES]
    row = lax.broadcasted_iota(I32, (SUBLANES, x.shape[1]), 0)
    out = None
    for val in dict.fromkeys(wanted):
        piece = jnp.broadcast_to(x[val:val + 1], (SUBLANES, x.shape[1]))
        if out is None:
            out = piece
        else:
            mask = functools.reduce(jnp.logical_or, [row == r for r, p in enumerate(picks) if p == val])
            out = jnp.where(mask, piece, out)
    return out


def _peer_route_kernel(h_ref, g_ref, wq_ref, k_hi_ref, k_lo_ref, cid_ref, u_o, idx_o, gate_o,
                       s_scr, v_scr, p_scr, c_scr, e_scr, ts_scr, ex_scr, *, pairs, idx_scale):
    u = _rmsnorm(h_ref[...], g_ref[...])
    u_o[...] = u
    u_hi, u_lo = _split2(u)
    wq = wq_ref[...]
    q_hi, q_lo = _split2(_dot(u_hi, wq) + _dot(u_lo, wq))
    tm = u.shape[0]
    k = PEER_TOPK
    nchain = 2 * PEER_HEADS
    for c in range(nchain):
        qh = q_hi[:, c * N_KEYS:(c + 1) * N_KEYS]
        ql = q_lo[:, c * N_KEYS:(c + 1) * N_KEYS]
        kh = k_hi_ref[c]
        s_scr[c] = _dot_nt(kh, qh) + (_dot_nt(k_lo_ref[c], qh) + _dot_nt(kh, ql))
    rid = lax.broadcasted_iota(I32, (N_KEYS, tm), 0)

    def top_keys(i, carry):
        for c in range(nchain):
            s = s_scr[c]
            m = jnp.max(s, axis=0, keepdims=True)
            p = jnp.min(jnp.where(s == m, rid, N_KEYS), axis=0, keepdims=True)
            v_scr[c, pl.ds(i, 1), :] = m
            p_scr[c, pl.ds(i, 1), :] = p
            s_scr[c] = jnp.where(rid == p, -jnp.inf, s)
        return carry

    lax.fori_loop(0, k, top_keys, 0)

    for h in range(PEER_HEADS):
        v1, v2 = v_scr[2 * h], v_scr[2 * h + 1]
        i1, i2 = p_scr[2 * h] * N_KEYS, p_scr[2 * h + 1]
        for t in range(len(pairs) // SUBLANES):
            tile = pairs[t * SUBLANES:(t + 1) * SUBLANES]
            pa = [None if p is None else p[0] for p in tile]
            pb = [None if p is None else p[1] for p in tile]
            cand = _pick_rows(v1, pa) + _pick_rows(v2, pb)
            if None in tile:
                row = lax.broadcasted_iota(I32, cand.shape, 0)
                pad = functools.reduce(jnp.logical_or, [row == r for r, p in enumerate(tile) if p is None])
                cand = jnp.where(pad, -jnp.inf, cand)
            c_scr[h, t * SUBLANES:(t + 1) * SUBLANES, :] = cand
            e_scr[h, t * SUBLANES:(t + 1) * SUBLANES, :] = _pick_rows(i1, pa) + _pick_rows(i2, pb)
    cid = cid_ref[...]
    ncand = cid.shape[0]

    def top_cands(i, carry):
        for h in range(PEER_HEADS):
            cand = c_scr[h]
            m = jnp.max(cand, axis=0, keepdims=True)
            c = jnp.min(jnp.where(cand == m, cid, k * k), axis=0, keepdims=True)
            hit = cid == c
            e = jnp.max(jnp.where(hit, e_scr[h], -1), axis=0, keepdims=True)
            c_scr[h] = jnp.where(hit, -jnp.inf, cand)
            ts_scr[pl.ds(h * k + i, 1), :] = m
            ex_scr[pl.ds(h * k + i, 1), :] = e * idx_scale
        return carry

    lax.fori_loop(0, k, top_cands, 0)
    idx_o[...] = ex_scr[...].T
    for h in range(PEER_HEADS):
        ts = ts_scr[h * k:(h + 1) * k, :]
        ex = jnp.exp(ts - jnp.max(ts, axis=0, keepdims=True))
        gate_o[h * k:(h + 1) * k, :] = ex / jnp.sum(ex, axis=0, keepdims=True)


def _peer_route(h, consts, pairs, idx_scale):
    n, d = h.shape
    tm = CHUNK
    nsel = PEER_HEADS * PEER_TOPK
    ncand = consts[-1].shape[0]
    rows = pl.BlockSpec((tm, d), lambda i: (i, 0))
    sel = pl.BlockSpec((nsel, tm), lambda i: (0, i))
    return pl.pallas_call(
        functools.partial(_peer_route_kernel, pairs=pairs, idx_scale=idx_scale),
        grid=(n // tm,),
        in_specs=[rows] + [_full(a.shape) for a in consts],
        out_specs=[rows, pl.BlockSpec((tm, nsel), lambda i: (i, 0)), sel],
        out_shape=[jax.ShapeDtypeStruct((n, d), F32), jax.ShapeDtypeStruct((n, nsel), I32),
                   jax.ShapeDtypeStruct((nsel, n), F32)],
        scratch_shapes=[pltpu.VMEM((2 * PEER_HEADS, N_KEYS, tm), F32),
                        pltpu.VMEM((2 * PEER_HEADS, PEER_TOPK, tm), F32),
                        pltpu.VMEM((2 * PEER_HEADS, PEER_TOPK, tm), I32),
                        pltpu.VMEM((PEER_HEADS, ncand, tm), F32),
                        pltpu.VMEM((PEER_HEADS, ncand, tm), I32),
                        pltpu.VMEM((nsel, tm), F32),
                        pltpu.VMEM((nsel, tm), I32)],
        compiler_params=_params(("parallel",)),
        name="peer_route",
    )(h, *consts)


PEER_GROUP = 16
PEER_SLOTS = 2
WORDS = 2


def _pack_kernel(t_ref, o_ref):
    r = t_ref.shape[0]
    o_ref[...] = pltpu.bitcast(t_ref[...].reshape(r * SUBLANES, LANES), I32)


def _pack_table(t):
    n, d = t.shape
    rows = d // LANES
    blk = 512
    t3 = t.astype(BF16).reshape(n, rows, LANES)
    return pl.pallas_call(
        _pack_kernel,
        grid=(n // blk,),
        in_specs=[pl.BlockSpec((blk, rows, LANES), lambda i: (i, 0, 0))],
        out_specs=pl.BlockSpec((blk * rows // WORDS, LANES), lambda i: (i, 0)),
        out_shape=jax.ShapeDtypeStruct((n * rows // WORDS, LANES), I32),
        compiler_params=_params(("parallel",)),
        name="pack_table",
    )(t3)


def _for_each_token(idx_ref, ids, sem, tm, token_fn):
    ngrp = tm // PEER_GROUP

    def copy(grp, slot):
        return pltpu.make_async_copy(idx_ref.at[pl.ds(grp * PEER_GROUP, PEER_GROUP)], ids.at[slot], sem.at[slot])

    for slot in range(PEER_SLOTS):
        copy(slot, slot).start()

    def body(q, carry):
        for slot in range(PEER_SLOTS):
            grp = q * PEER_SLOTS + slot
            copy(grp, slot).wait()
            for s in range(PEER_GROUP):
                token_fn(grp * PEER_GROUP + s, slot, s)

            @pl.when(grp + PEER_SLOTS < ngrp)
            def _():
                copy(grp + PEER_SLOTS, slot).start()
        return carry

    lax.fori_loop(0, ngrp // PEER_SLOTS, body, 0)


def _gather_rows(ids, tab_ref, gbuf, slot, s, nsel):
    rows_per = SUBLANES // WORDS
    for j in range(nsel):
        e = pl.multiple_of(ids[slot, s, j], rows_per)
        gbuf[s, j * rows_per:(j + 1) * rows_per, :] = tab_ref[pl.ds(e, rows_per), :]
    return pltpu.bitcast(gbuf[s], BF16)


def _diag_mask(nsel):
    sub = lax.broadcasted_iota(I32, (SUBLANES, nsel * SUBLANES), 0)
    lane = lax.broadcasted_iota(I32, (SUBLANES, nsel * SUBLANES), 1)
    return sub == lane % SUBLANES


def _gelu(x):
    return 0.5 * x * (1.0 + lax.erf(x * np.float32(1.0 / np.sqrt(2.0))))


def _peer_u_kernel(idx_ref, tok_ref, gate_ref, tab_ref, eexp_ref, act_o, gbuf, rsum, ids, sem, *, tm, nsel):
    diag = _diag_mask(nsel)

    def token(i, slot, s):
        rows = _gather_rows(ids, tab_ref, gbuf, slot, s, nsel)
        tok = tok_ref[i].astype(BF16)
        prod = _dot_nt(tok, rows)
        rsum[pl.ds(i, 1), :] = jnp.sum(jnp.where(diag, prod, 0.0), axis=0, keepdims=True)

    _for_each_token(idx_ref, ids, sem, tm, token)
    eexp = eexp_ref[...]
    hi, mid, lo = _split3(rsum[...])
    pre_t = _dot_nt(eexp, hi) + _dot_nt(eexp, mid) + _dot_nt(eexp, lo)
    act_o[...] = _gelu(pre_t) * gate_ref[...]


def _peer_v_kernel(idx_ref, act_ref, h_ref, g_ref, tab_ref, eexp_ref, out_o, gbuf, arep, ids, sem, *, tm, nsel):
    diag = _diag_mask(nsel)
    arep[...] = _dot(act_ref[...].T.astype(BF16), eexp_ref[...])

    def token(i, slot, s):
        rows = _gather_rows(ids, tab_ref, gbuf, slot, s, nsel)
        a = jnp.broadcast_to(arep[pl.ds(i, 1), :], diag.shape)
        out_o[i] = h_ref[i] + _dot(jnp.where(diag, a, 0.0).astype(BF16), rows)

    _for_each_token(idx_ref, ids, sem, tm, token)
    x = out_o[...]
    ms = jnp.sum(jnp.sum(x * x, axis=2, keepdims=True), axis=1, keepdims=True) * (1.0 / (SUBLANES * LANES))
    out_o[...] = x * lax.rsqrt(ms + RMS_EPS) * g_ref[...]


def _peer_gather_specs(tm, nsel, tab):
    idx = pl.BlockSpec((tm, nsel), lambda i: (i, 0))
    table = pl.BlockSpec(tab.shape, lambda i: (0, 0), pipeline_mode=pl.Buffered(1))
    scratch = [pltpu.VMEM((PEER_GROUP, nsel * SUBLANES // WORDS, LANES), I32),
               pltpu.VMEM((tm, nsel * SUBLANES), F32),
               pltpu.SMEM((PEER_SLOTS, PEER_GROUP, nsel), I32),
               pltpu.SemaphoreType.DMA((PEER_SLOTS,))]
    return idx, table, scratch


def _peer_u(idx, tok3, gate, tab, eexp, *, tm):
    n, nsel = idx.shape
    assert tm % (PEER_GROUP * PEER_SLOTS) == 0
    idx_spec, tab_spec, scratch = _peer_gather_specs(tm, nsel, tab)
    sel = pl.BlockSpec((nsel, tm), lambda i: (0, i))
    return pl.pallas_call(
        functools.partial(_peer_u_kernel, tm=tm, nsel=nsel),
        grid=(n // tm,),
        in_specs=[idx_spec, pl.BlockSpec((tm, SUBLANES, LANES), lambda i: (i, 0, 0)), sel, tab_spec,
                  _full(eexp.shape)],
        out_specs=sel,
        out_shape=jax.ShapeDtypeStruct((nsel, n), F32),
        scratch_shapes=scratch,
        compiler_params=_params(("arbitrary",)),
        name="peer_u",
    )(idx, tok3, gate, tab, eexp)


def _peer_v(idx, act, h3, gain, tab, eexp, *, tm):
    n, nsel = idx.shape
    assert tm % (PEER_GROUP * PEER_SLOTS) == 0
    idx_spec, tab_spec, scratch = _peer_gather_specs(tm, nsel, tab)
    tiles = pl.BlockSpec((tm, SUBLANES, LANES), lambda i: (i, 0, 0))
    return pl.pallas_call(
        functools.partial(_peer_v_kernel, tm=tm, nsel=nsel),
        grid=(n // tm,),
        in_specs=[idx_spec, pl.BlockSpec((nsel, tm), lambda i: (0, i)), tiles, _full(gain.shape), tab_spec,
                  _full(eexp.shape)],
        out_specs=tiles,
        out_shape=jax.ShapeDtypeStruct((n, SUBLANES, LANES), F32),
        scratch_shapes=scratch,
        compiler_params=_params(("arbitrary",)),
        name="peer_v",
    )(idx, act, h3, gain, tab, eexp)


def _hi_lo(w):
    hi = w.astype(BF16)
    return hi, (w - hi.astype(F32)).astype(BF16)


def _block_ones(n, blk):
    i = np.arange(n)
    return jnp.asarray(i[:, None] // blk == i[None, :] // blk, dtype=BF16)


def _stream(x_rows, lead, conv0, ssd0, shift0, wkv0, w, *, nseq, nblk, lb, has_lead, npad, tb):
    d = x_rows.shape[1]
    nblk_all = nblk + (1 if has_lead else 0)
    width_a = w["ssm_norm"].shape[1]
    width_b = w["ones_b"].shape[0]
    heads_b = width_b // RWKV_HEAD_DIM
    if lb == CHUNK:
        xbc, z, dtp, rw = _proj_in(x_rows, lead, w["norm_mix"], w["w_in"], w["widths"],
                                   nseq=nseq, nblk=nblk_all, has_lead=has_lead)
        nchunk = nblk_all
    else:
        xbc, z, dtp, rw = _proj_in(x_rows, lead, w["norm_mix"], w["w_in"], w["widths"],
                                   nseq=1, nblk=x_rows.shape[0] // CHUNK, has_lead=False)
        nchunk = 1
    ya, conv_new, ssd_new = _ssd(xbc, z, dtp, conv0, ssd0, w["ssd"], nseq=nseq, lb=lb, nchunk=nchunk, npad=npad)
    scan_kernels = lb == CHUNK and heads_b == SUBLANES and 2 * nseq * heads_b == LANES
    *scan_in, g, bonus, shift_new = _rwkv_pre(
        rw, shift0[:, None, :], w["rwkv_pre"], nseq=nseq, lb=lb, nchunk=nchunk,
        npad=npad if lb == CHUNK else 0, width=width_b, stacked=scan_kernels)

    t_all = nchunk * lb
    t0 = npad if lb == CHUNK else 0
    t_real = t_all - t0
    pairs = nseq * heads_b
    hd = RWKV_HEAD_DIM
    if pairs >= LANES:
        ngroup, dup = pairs // LANES, 1
    else:
        ngroup, dup = 1, LANES // pairs
    seq_per = nseq // ngroup
    vr = hd // dup

    def to_scan_k(a):
        a = a.reshape(ngroup, seq_per, t_all, heads_b, hd)[:, :, t0:]
        a = a.transpose(0, 2, 4, 1, 3).reshape(ngroup, t_real, hd, seq_per * heads_b)
        return jnp.tile(a, (1, 1, 1, dup))

    def to_scan_v(a):
        a = a.reshape(ngroup, seq_per, t_all, heads_b, dup, vr)[:, :, t0:]
        return a.transpose(0, 2, 5, 4, 1, 3).reshape(ngroup, t_real, vr, LANES)

    s0 = wkv0.reshape(ngroup, seq_per, heads_b, dup, vr, hd).transpose(0, 5, 4, 3, 1, 2)
    s0 = s0.reshape(ngroup, hd, vr, LANES)
    if scan_kernels:
        ks, vs = _to_scan(scan_in[0], heads=heads_b, dup=dup, tb=tb)
        ysc, s_new = _rwkv_scan_kmajor(ks, vs, s0, tb=tb, t0=t0)
    else:
        ysc, s_new = _rwkv_scan(*[to_scan_k(a) for a in scan_in[:5]], to_scan_v(scan_in[5]), s0, tb=tb)
    wkv_new = s_new.reshape(ngroup, hd, vr, dup, seq_per, heads_b).transpose(0, 4, 5, 3, 2, 1)
    wkv_new = wkv_new.reshape(nseq, heads_b, hd, hd)
    skip_t = t_real - nblk * lb if lb == CHUNK else 0
    ys = ysc[:, skip_t:].reshape(ngroup, t_real - skip_t, vr, dup, seq_per, heads_b)
    ys = ys.transpose(0, 4, 1, 5, 3, 2).reshape(nseq * (t_real - skip_t), width_b)

    if lb == CHUNK:
        h1 = _mix_out(ys, bonus, g, ya, x_rows, w["mix_out"], nseq=nseq, nblk=nblk, skip=nchunk - nblk)
    else:
        h1 = _mix_out(ys, bonus, g, ya, x_rows, w["mix_out"], nseq=1, nblk=x_rows.shape[0] // CHUNK, skip=0)

    u, idx, gate = _peer_route(h1, w["route"], w["cand_pairs"], SUBLANES // WORDS)
    n = u.shape[0]
    act = _peer_u(idx, u.reshape(n, SUBLANES, LANES), gate, w["tab_u"], w["eexp"], tm=w["peer_tm"])
    y = _peer_v(idx, act, h1.reshape(n, SUBLANES, LANES), w["norm_final"], w["tab_v"], w["eexp"], tm=w["peer_tm"])
    return y.reshape(n, d), conv_new, ssd_new, shift_new[:, 0, :], wkv_new


def kernel(x_prompt, x_sample, state_conv, state_ssd, state_shift, state_wkv, meta_tokens, norm_mix, w_in, conv_w, conv_b, dt_bias, a_log, d_skip, ssm_norm, shift_mu, decay_w0, decay_w2, iclr_a0, iclr_a2, gate_g2, k_k, k_a, r_k, lnx_w, lnx_b, w_out, norm_ffn, w_query, sub_keys, expert_u, expert_v, norm_final):
    bp, seq_p, d = x_prompt.shape
    bs, seq_s, _ = x_sample.shape
    depth = w_in.shape[0]
    assert depth == 1 and seq_p % CHUNK == 0 and (bs * seq_s) % CHUNK == 0 and seq_s % SUBLANES == 0
    assert d == SUBLANES * LANES
    heads_a = state_ssd.shape[2]
    width_a = heads_a * SSD_HEAD_DIM
    conv_dim = state_conv.shape[3]
    rw_in = state_shift.shape[2]
    heads_b = state_wkv.shape[2]
    width_b = heads_b * RWKV_HEAD_DIM
    assert heads_a <= LANES and rw_in == 3 * width_b + DECAY_LORA + AAA_LORA + GATE_LORA

    wi = w_in[0]
    ssd_in = width_a + conv_dim + heads_a
    w_cat = jnp.concatenate([
        wi[:, width_a:width_a + conv_dim], wi[:, :width_a],
        jnp.pad(wi[:, width_a + conv_dim:ssd_in], ((0, 0), (0, LANES - heads_a))),
        wi[:, ssd_in:]], axis=1).astype(BF16)
    widths = (conv_dim, width_a, LANES, rw_in)
    pad_h = lambda a: jnp.pad(a.reshape(1, -1), ((0, 0), (0, LANES - heads_a)))
    i_l = np.arange(CHUNK)
    tri = jnp.asarray(i_l[:, None] >= i_l[None, :], dtype=BF16)
    ehead = jnp.asarray(np.arange(LANES)[:, None] == np.arange(width_a)[None, :] // SSD_HEAD_DIM, dtype=BF16)
    ecol = jnp.asarray(np.arange(LANES)[:, None] == np.arange(heads_a * CHUNK)[None, :] // CHUNK, dtype=BF16)
    ssd_consts = (conv_w[0], conv_b[0].reshape(1, -1), pad_h(dt_bias[0]), pad_h(a_log[0]),
                  jnp.repeat(d_skip[0], SSD_HEAD_DIM).reshape(1, -1), ssm_norm[0].reshape(1, -1), tri, ehead, ecol)
    ones_b = _block_ones(width_b, RWKV_HEAD_DIM)
    zero = jnp.zeros((DECAY_LORA, width_b), F32)
    w_wa = jnp.concatenate([jnp.concatenate([decay_w2[0], zero], axis=1),
                            jnp.concatenate([jnp.zeros((AAA_LORA, width_b), F32), iclr_a2[0]], axis=1)], axis=0)
    row = lambda a: a.reshape(1, -1)
    pre_consts = (row(shift_mu[0]), row(decay_w0[0]), row(iclr_a0[0]), *_hi_lo(w_wa), *_hi_lo(gate_g2[0]),
                  row(k_k[0]), row(k_a[0]), row(r_k[0]), ones_b)
    wo = w_out[0].astype(BF16)
    mix_consts = (row(lnx_w[0]), row(lnx_b[0]), ones_b, wo[:width_a], wo[width_a:])
    cand_pairs, cand_ids = _cand_rows()
    keys = sub_keys[0].transpose(1, 0, 2, 3).reshape(2 * PEER_HEADS, N_KEYS, -1)
    route_consts = (row(norm_ffn[0]), w_query[0].astype(BF16), *_hi_lo(keys),
                    jnp.asarray(np.broadcast_to(cand_ids[:, None], (cand_ids.shape[0], CHUNK))))
    nsel = PEER_HEADS * PEER_TOPK
    lane8 = np.arange(nsel * SUBLANES)
    eexp = jnp.asarray(np.arange(nsel)[:, None] == lane8[None, :] // SUBLANES, dtype=BF16)
    w = dict(norm_mix=row(norm_mix[0]), w_in=w_cat, widths=widths, ssm_norm=row(ssm_norm[0]), ones_b=ones_b,
             ssd=ssd_consts, rwkv_pre=pre_consts, mix_out=mix_consts, route=route_consts,
             tab_u=_pack_table(expert_u[0]), tab_v=_pack_table(expert_v[0]), eexp=eexp, cand_pairs=cand_pairs,
             norm_final=norm_final.reshape(SUBLANES, LANES), peer_tm=CHUNK)

    npad = CHUNK - N_META
    lead = jnp.concatenate([jnp.zeros((npad, d), F32), meta_tokens.astype(F32)], axis=0)
    zeros = lambda *s: jnp.zeros(s, F32)
    yp, cp, sp, shp, wp = _stream(
        x_prompt.reshape(bp * seq_p, d), lead,
        zeros(bp, CONV_W - 1, conv_dim), zeros(bp, heads_a, SSD_HEAD_DIM, SSD_STATE), zeros(bp, rw_in),
        zeros(bp, heads_b, RWKV_HEAD_DIM, RWKV_HEAD_DIM), w,
        nseq=bp, nblk=seq_p // CHUNK, lb=CHUNK, has_lead=True, npad=npad, tb=N_META)
    ys, cs, ss, shs, ws = _stream(
        x_sample.reshape(bs * seq_s, d), lead, state_conv[0], state_ssd[0], state_shift[0], state_wkv[0], w,
        nseq=bs, nblk=1, lb=seq_s, has_lead=False, npad=CHUNK - seq_s, tb=seq_s)
    return (yp.reshape(bp, seq_p, d), ys.reshape(bs, seq_s, d), cp[None], sp[None], shp[None], wp[None],
            cs[None], ss[None], shs[None], ws[None])
```

```python
import functools

import jax
import jax.numpy as jnp
import numpy as np
from jax import lax
from jax.experimental import pallas as pl
from jax.experimental.pallas import tpu as pltpu

F32 = jnp.float32
BF16 = jnp.bfloat16
I32 = jnp.int32

N_META = 16
SSD_HEAD_DIM = 64
SSD_GROUPS = 2
SSD_STATE = 128
CONV_W = 4
RWKV_HEAD_DIM = 64
DECAY_LORA = 64
AAA_LORA = 64
GATE_LORA = 128
PEER_HEADS = 8
N_KEYS = 128
PEER_TOPK = 16
RMS_EPS = 1e-6
GN_EPS = 64e-5

LANES = 128
SUBLANES = 8
CHUNK = 128
VMEM_LIMIT_BYTES = 56 * 1024 * 1024


def _full(shape):
    zeros = (0,) * len(shape)
    return pl.BlockSpec(shape, lambda *_: zeros)


def _params(semantics, vmem=VMEM_LIMIT_BYTES):
    return pltpu.CompilerParams(dimension_semantics=semantics, vmem_limit_bytes=vmem)


def _split2(x):
    hi = x.astype(BF16)
    lo = (x - hi.astype(F32)).astype(BF16)
    return hi, lo


def _split3(x):
    hi = x.astype(BF16)
    r = x - hi.astype(F32)
    mid = r.astype(BF16)
    lo = (r - mid.astype(F32)).astype(BF16)
    return hi, mid, lo


def _dot(a, b):
    return jnp.dot(a, b, preferred_element_type=F32)


def _dot_nt(a, b):
    return lax.dot_general(a, b, (((1,), (1,)), ((), ())), preferred_element_type=F32)


def _dot_sel(x, sel):
    hi, mid, lo = _split3(x)
    return _dot(hi, sel) + _dot(mid, sel) + _dot(lo, sel)


def _dot_hp(x, w_hi, w_lo):
    hi, lo = _split2(x)
    return _dot(hi, w_hi) + (_dot(lo, w_hi) + _dot(hi, w_lo))


def _dot_hp_nt(x, w_hi, w_lo):
    hi, lo = _split2(x)
    return _dot_nt(hi, w_hi) + (_dot_nt(lo, w_hi) + _dot_nt(hi, w_lo))


def _silu(x):
    return x * jax.nn.sigmoid(x)


def _softplus(x):
    return jnp.maximum(x, 0.0) + jnp.log1p(jnp.exp(-jnp.abs(x)))


def _rmsnorm(x, g):
    ms = jnp.mean(x * x, axis=-1, keepdims=True)
    return x * lax.rsqrt(ms + RMS_EPS) * g


def _proj_in_kernel(x_ref, lead_ref, g_ref, w_ref, xbc_ref, z_ref, dt_ref, rw_ref, *, widths, has_lead):
    x = x_ref[...]
    if has_lead:
        x = jnp.where(pl.program_id(1) == 0, lead_ref[...], x)
    u = _rmsnorm(x, g_ref[...]).astype(BF16)
    p = _dot(u, w_ref[...])
    off = 0
    for ref, w in zip((xbc_ref, z_ref, dt_ref, rw_ref), widths):
        ref[...] = p[:, off:off + w]
        off += w


def _proj_in(rows, lead, g, w_cat, widths, *, nseq, nblk, has_lead):
    d = rows.shape[1]
    nblk_in = nblk - 1 if has_lead else nblk
    shift = 1 if has_lead else 0
    x_map = lambda s, j: (s * nblk_in + jnp.maximum(j - shift, 0), 0)
    o_map = lambda s, j: (s * nblk + j, 0)
    return pl.pallas_call(
        functools.partial(_proj_in_kernel, widths=widths, has_lead=has_lead),
        grid=(nseq, nblk),
        in_specs=[pl.BlockSpec((CHUNK, d), x_map), _full(lead.shape), _full(g.shape), _full(w_cat.shape)],
        out_specs=[pl.BlockSpec((CHUNK, w), o_map) for w in widths],
        out_shape=[jax.ShapeDtypeStruct((nseq * nblk * CHUNK, w), F32) for w in widths],
        compiler_params=_params(("parallel", "parallel")),
        name="proj_in",
    )(rows, lead, g, w_cat)


def _cumsum_rows(x, tri_ref):
    hi, mid, lo = _split3(x)
    tri = tri_ref[...]
    return _dot(tri, hi) + _dot(tri, mid) + _dot(tri, lo)


def _ssd_kernel(xbc_ref, z_ref, dt_ref, cs_ref, h0_ref, cw_ref, cb_ref, dtb_ref, alog_ref,
                dskip_ref, norm_ref, tri_ref, ehead_ref, ecol_ref,
                y_ref, cs_out_ref, h_out_ref, xfull, dtile, *, lb, npad, heads, width):
    c = pl.program_id(1)
    nc = pl.num_programs(1)
    L = CHUNK
    prev = CONV_W - 1
    base = SUBLANES
    state = SSD_STATE
    hd = SSD_HEAD_DIM
    per_group = heads // SSD_GROUPS

    @pl.when(c == 0)
    def _():
        h_out_ref[...] = h0_ref[...]
        if lb == L:
            xfull[base - prev:base, :] = cs_ref[0]

    if lb == L:
        xfull[base:base + L, :] = xbc_ref[...]
        dt_raw = dt_ref[...]
    else:
        xfull[...] = jnp.zeros_like(xfull)
        xfull[base + L - lb - prev:base + L - lb, :] = cs_ref[0]
        xfull[base + L - lb:base + L, :] = xbc_ref[...]
        dtile[...] = jnp.zeros_like(dtile)
        dtile[L - lb:L, :] = dt_ref[...]
        dt_raw = dtile[...]

    conv = cb_ref[...] + xfull[base - prev:base - prev + L, :] * cw_ref[0:1, :]
    for k in range(1, CONV_W):
        conv = conv + xfull[base - prev + k:base - prev + k + L, :] * cw_ref[k:k + 1, :]
    cs_new = xfull[base + L - prev:base + L, :]

    @pl.when(c == nc - 1)
    def _():
        cs_out_ref[0] = cs_new

    xfull[base - prev:base, :] = cs_new

    xbc = _silu(conv)
    xs = xbc[:, :width]
    gn = SSD_GROUPS * state
    row = lax.broadcasted_iota(I32, (L, LANES), 0) + c * L
    dt = jnp.where(row >= npad, _softplus(dt_raw + dtb_ref[...]), 0.0)
    da = dt * (-jnp.exp(alog_ref[...]))
    acum = _cumsum_rows(da, tri_ref)
    acum_t = acum.T
    ehead = ehead_ref[...]
    acum_x = _dot_sel(acum, ehead)
    dt_x = _dot_sel(dt, ehead)
    last_x = acum_x[L - 1:L, :]
    xdt = xs * dt_x
    wend = xdt * jnp.exp(last_x - acum_x)
    colb = _dot_sel(acum, ecol_ref[...])
    causal = (lax.broadcasted_iota(I32, (L, L), 0) >= lax.broadcasted_iota(I32, (L, L), 1))
    lane = lax.broadcasted_iota(I32, (L, LANES), 1)

    y_diag = []
    y_off = []
    for g in range(SSD_GROUPS):
        bm = xbc[:, width + g * state:width + (g + 1) * state].astype(BF16)
        cm = xbc[:, width + gn + g * state:width + gn + (g + 1) * state].astype(BF16)
        cb = _dot_nt(cm, bm)
        h_prev = h_out_ref[0, g * per_group:(g + 1) * per_group].reshape(per_group * hd, state)
        y_off.append(_dot_nt(cm, h_prev.astype(BF16)))
        for pair in range(per_group // 2):
            h_a = g * per_group + 2 * pair
            xp = xdt[:, h_a * hd:(h_a + 2) * hd].astype(BF16)
            outs = []
            for h in (h_a, h_a + 1):
                seg = colb[:, h * L:(h + 1) * L] - acum_t[h:h + 1, :]
                dec = jnp.exp(jnp.where(causal, seg, -jnp.inf))
                outs.append(_dot((cb * dec).astype(BF16), xp))
            y_diag.append(jnp.where(lane < hd, outs[0], outs[1]))
        wg_t = wend[:, g * per_group * hd:(g + 1) * per_group * hd].T.astype(BF16)
        upd = _dot(wg_t, bm)
        for i in range(per_group):
            h = g * per_group + i
            cd = jnp.exp(acum_t[h:h + 1, L - 1:L])
            h_out_ref[0, h] = h_out_ref[0, h] * cd + upd[i * hd:(i + 1) * hd, :]
    y = (jnp.concatenate(y_diag, axis=1) + jnp.concatenate(y_off, axis=1) * jnp.exp(acum_x)
         + xs * dskip_ref[...])
    y = y[L - lb:, :] * _silu(z_ref[...])
    y_ref[...] = _rmsnorm(y, norm_ref[...])


def _ssd(xbc, z, dtp, cs, h0, consts, *, nseq, lb, nchunk, npad):
    heads = h0.shape[1]
    width = z.shape[1]
    conv_dim = xbc.shape[1]
    row_map = lambda s, c: (s * nchunk + c, 0)
    kernel = functools.partial(_ssd_kernel, lb=lb, npad=npad, heads=heads, width=width)
    return pl.pallas_call(
        kernel,
        grid=(nseq, nchunk),
        in_specs=[pl.BlockSpec((lb, conv_dim), row_map),
                  pl.BlockSpec((lb, width), row_map),
                  pl.BlockSpec((lb, LANES), row_map),
                  pl.BlockSpec((1,) + cs.shape[1:], lambda s, c: (s, 0, 0)),
                  pl.BlockSpec((1,) + h0.shape[1:], lambda s, c: (s, 0, 0, 0))]
                 + [_full(a.shape) for a in consts],
        out_specs=[pl.BlockSpec((lb, width), row_map),
                   pl.BlockSpec((1,) + cs.shape[1:], lambda s, c: (s, 0, 0)),
                   pl.BlockSpec((1,) + h0.shape[1:], lambda s, c: (s, 0, 0, 0))],
        out_shape=[jax.ShapeDtypeStruct((nseq * nchunk * lb, width), F32),
                   jax.ShapeDtypeStruct(cs.shape, F32),
                   jax.ShapeDtypeStruct(h0.shape, F32)],
        scratch_shapes=[pltpu.VMEM((SUBLANES + CHUNK, conv_dim), F32), pltpu.VMEM((CHUNK, LANES), F32)],
        compiler_params=_params(("parallel", "arbitrary")),
        name="ssd",
    )(xbc, z, dtp, cs, h0, *consts)


def _rwkv_pre_kernel(rw_ref, sh_ref, mu_ref, w0_ref, a0_ref, wa_hi_ref, wa_lo_ref, g2_hi_ref, g2_lo_ref,
                     kk_ref, ka_ref, rk_ref, ones_ref,
                     *refs, lb, npad, width, stacked):
    if stacked:
        sc_o, g_o, bonus_o, sh_o, pfull = refs
    else:
        kn_o, d_o, b_o, k_o, r_o, v_o, g_o, bonus_o, sh_o, pfull = refs
    c = pl.program_id(1)
    nc = pl.num_programs(1)
    base = SUBLANES
    p = rw_ref[...]

    @pl.when(c == 0)
    def _():
        pfull[base - 1:base, :] = sh_ref[0]

    pfull[base:base + lb, :] = p
    prev = pfull[base - 1:base - 1 + lb, :]
    last = p[lb - 1:lb, :]
    pfull[base - 1:base, :] = last

    @pl.when(c == nc - 1)
    def _():
        sh_o[0] = last

    pm = p + (prev - p) * mu_ref[...]
    r = pm[:, :width]
    k = pm[:, width:2 * width]
    v = pm[:, 2 * width:3 * width]
    lora_in = pm[:, 3 * width:3 * width + DECAY_LORA + AAA_LORA]
    lane = lax.broadcasted_iota(I32, lora_in.shape, 1)
    lora_in = jnp.where(lane < DECAY_LORA, jnp.tanh(lora_in), lora_in)
    lora = _dot_hp(lora_in, wa_hi_ref[...], wa_lo_ref[...])
    gate_in = jax.nn.sigmoid(pm[:, 3 * width + DECAY_LORA + AAA_LORA:])
    g_o[...] = _dot_hp(gate_in, g2_hi_ref[...], g2_lo_ref[...])
    w = -_softplus(-(w0_ref[...] + lora[:, :width])) - 0.5
    row = lax.broadcasted_iota(I32, w.shape, 0) + c * lb
    dcy = jnp.where(row >= npad, jnp.exp(-jnp.exp(w)), 1.0)
    a = jax.nn.sigmoid(a0_ref[...] + lora[:, width:])
    ones = ones_ref[...]
    kn = k * kk_ref[...]
    kn = kn / jnp.maximum(jnp.sqrt(_dot_sel(kn * kn, ones)), 1e-12)
    kp = k * (1.0 + (a - 1.0) * ka_ref[...])
    bonus_o[...] = _dot_sel(r * kp * rk_ref[...], ones) * v
    scan_in = (kn, dcy, kn * a, kp, r, v)
    if stacked:
        for j, x in enumerate(scan_in):
            sc_o[j, 0] = x.T
    else:
        for ref, x in zip((kn_o, d_o, b_o, k_o, r_o, v_o), scan_in):
            ref[...] = x


def _rwkv_pre(rw, sh, consts, *, nseq, lb, nchunk, npad, width, stacked):
    rw_in = rw.shape[1]
    row_map = lambda s, c: (s * nchunk + c, 0)
    n = nseq * nchunk * lb
    rows = pl.BlockSpec((lb, width), row_map)
    if stacked:
        scan_specs = [pl.BlockSpec((6, 1, width, lb), lambda s, c: (0, s, 0, c))]
        scan_shapes = [jax.ShapeDtypeStruct((6, nseq, width, nchunk * lb), F32)]
    else:
        scan_specs = [rows] * 6
        scan_shapes = [jax.ShapeDtypeStruct((n, width), F32)] * 6
    kernel = functools.partial(_rwkv_pre_kernel, lb=lb, npad=npad, width=width, stacked=stacked)
    return pl.pallas_call(
        kernel,
        grid=(nseq, nchunk),
        in_specs=[pl.BlockSpec((lb, rw_in), row_map), pl.BlockSpec((1, 1, rw_in), lambda s, c: (s, 0, 0))]
                 + [_full(a.shape) for a in consts],
        out_specs=scan_specs + [rows, rows, pl.BlockSpec((1, 1, rw_in), lambda s, c: (s, 0, 0))],
        out_shape=scan_shapes + [jax.ShapeDtypeStruct((n, width), F32)] * 2 + [jax.ShapeDtypeStruct(sh.shape, F32)],
        scratch_shapes=[pltpu.VMEM((SUBLANES + lb, rw_in), F32)],
        compiler_params=_params(("parallel", "arbitrary")),
        name="rwkv_pre",
    )(rw, sh, *consts)


def _to_scan_k_kernel(x_ref, o_ref, *, nseq, heads, hd, dup):
    nblk, _, tb, _ = o_ref.shape[1:]
    for k in range(hd):
        rows = jnp.concatenate([x_ref[0, s, pl.ds(k, heads, stride=hd), :] for s in range(nseq)], axis=0)
        o_ref[0, :, k] = jnp.concatenate([rows] * dup, axis=0).T.reshape(nblk, tb, LANES)


def _to_scan_v_kernel(x_ref, o_ref, *, nseq, heads, hd, dup):
    vr = hd // dup
    for v in range(vr):
        rows = jnp.concatenate([x_ref[0, s, pl.ds(p * vr + v, heads, stride=hd), :]
                                for p in range(dup) for s in range(nseq)], axis=0)
        o_ref[:, v, :] = rows.T


def _to_scan(sc, *, heads, dup, tb):
    _, nseq, width, t_all = sc.shape
    hd = width // heads
    kw = dict(nseq=nseq, heads=heads, hd=hd, dup=dup)
    ks = pl.pallas_call(
        functools.partial(_to_scan_k_kernel, **kw),
        grid=(5, t_all // CHUNK),
        in_specs=[pl.BlockSpec((1, nseq, width, CHUNK), lambda j, c: (j, 0, 0, c))],
        out_specs=pl.BlockSpec((1, CHUNK // tb, hd, tb, LANES), lambda j, c: (j, c, 0, 0, 0)),
        out_shape=jax.ShapeDtypeStruct((5, t_all // tb, hd, tb, LANES), F32),
        compiler_params=_params(("parallel", "parallel")),
        name="to_scan_k",
    )(sc)
    vs = pl.pallas_call(
        functools.partial(_to_scan_v_kernel, **kw),
        grid=(t_all // CHUNK,),
        in_specs=[pl.BlockSpec((1, nseq, width, CHUNK), lambda c: (5, 0, 0, c))],
        out_specs=pl.BlockSpec((CHUNK, hd // dup, LANES), lambda c: (c, 0, 0)),
        out_shape=jax.ShapeDtypeStruct((t_all, hd // dup, LANES), F32),
        compiler_params=_params(("parallel",)),
        name="to_scan_v",
    )(sc)
    return ks, vs


SCAN_PIECE = 32
SCAN_UNROLL = 8


def _rwkv_scan_kernel(kn_ref, d_ref, b_ref, k_ref, r_ref, v_ref, s0_ref, y_ref, s_ref, *, tb, hd, kmajor):
    @pl.when(pl.program_id(1) == 0)
    def _():
        s_ref[...] = s0_ref[...]

    vr = y_ref.shape[2]

    def krow(ref, t, k):
        return ref[0, 0, k, pl.ds(t, 1), :] if kmajor else ref[0, t, pl.ds(k, 1), :]

    def one_step(t):
        piece = min(SCAN_PIECE, vr)
        for p in range(vr // piece):
            rows = slice(p * piece, (p + 1) * piece)
            v_t = v_ref[t, rows, :] if kmajor else v_ref[0, t, rows, :]
            acc = [jnp.zeros_like(v_t), jnp.zeros_like(v_t)]
            for k in range(hd):
                acc[k % 2] = acc[k % 2] + s_ref[0, k, rows, :] * krow(kn_ref, t, k)
            sa = acc[0] + acc[1]
            acc = [jnp.zeros_like(v_t), jnp.zeros_like(v_t)]
            for k in range(hd):
                sk = s_ref[0, k, rows, :] * krow(d_ref, t, k) - sa * krow(b_ref, t, k) + v_t * krow(k_ref, t, k)
                s_ref[0, k, rows, :] = sk
                acc[k % 2] = acc[k % 2] + sk * krow(r_ref, t, k)
            y_ref[0, t, rows, :] = acc[0] + acc[1]

    def steps(i, carry):
        base = pl.multiple_of(i * SCAN_UNROLL, SCAN_UNROLL)
        for j in range(SCAN_UNROLL):
            one_step(base + j)
        return carry

    assert tb % SCAN_UNROLL == 0
    lax.fori_loop(0, tb // SCAN_UNROLL, steps, 0)


def _scan_call(operands, in_specs, s0, *, ngroup, nsteps, tb, hd, vr, kmajor):
    vspec = pl.BlockSpec((1, tb, vr, LANES), lambda g, i: (g, i, 0, 0))
    sspec = pl.BlockSpec((1, hd, vr, LANES), lambda g, i: (g, 0, 0, 0))
    return pl.pallas_call(
        functools.partial(_rwkv_scan_kernel, tb=tb, hd=hd, kmajor=kmajor),
        grid=(ngroup, nsteps // tb),
        in_specs=in_specs + [sspec],
        out_specs=[vspec, sspec],
        out_shape=[jax.ShapeDtypeStruct((ngroup, nsteps, vr, LANES), F32), jax.ShapeDtypeStruct(s0.shape, F32)],
        compiler_params=_params(("parallel", "arbitrary")),
        name="rwkv_scan",
    )(*operands, s0)


def _rwkv_scan(kn, d, b, k, r, v, s0, *, tb):
    ngroup, t_total, hd, _ = kn.shape
    vr = v.shape[2]
    kspec = pl.BlockSpec((1, tb, hd, LANES), lambda g, i: (g, i, 0, 0))
    vspec = pl.BlockSpec((1, tb, vr, LANES), lambda g, i: (g, i, 0, 0))
    return _scan_call((kn, d, b, k, r, v), [kspec] * 5 + [vspec], s0, ngroup=ngroup, nsteps=t_total, tb=tb, hd=hd,
                      vr=vr, kmajor=False)


def _rwkv_scan_kmajor(ks, vs, s0, *, tb, t0):
    _, nblk, hd, tb_k, _ = ks.shape
    t_all, vr, _ = vs.shape
    assert t0 % tb == 0 and tb_k == tb and nblk * tb == t_all
    off = t0 // tb
    kspecs = [pl.BlockSpec((1, 1, hd, tb, LANES), lambda g, i, j=j: (j, i + off, 0, 0, 0)) for j in range(5)]
    vspec = pl.BlockSpec((tb, vr, LANES), lambda g, i: (i + off, 0, 0))
    return _scan_call((ks,) * 5 + (vs,), kspecs + [vspec], s0, ngroup=1, nsteps=t_all - t0, tb=tb, hd=hd, vr=vr,
                      kmajor=True)


def _mix_out_kernel(ys_ref, bonus_ref, g_ref, ya_ref, h_ref, lnw_ref, lnb_ref, ones_ref, wa_ref, wb_ref, o_ref):
    ones = ones_ref[...]
    inv = 1.0 / RWKV_HEAD_DIM
    y = ys_ref[...]
    yc = y - _dot_sel(y, ones) * inv
    var = _dot_sel(yc * yc, ones) * inv
    yb = (yc * lax.rsqrt(var + GN_EPS) * lnw_ref[...] + lnb_ref[...] + bonus_ref[...]) * g_ref[...]
    o_ref[...] = (h_ref[...] + _dot(ya_ref[...].astype(BF16), wa_ref[...])
                  + _dot(yb.astype(BF16), wb_ref[...]))


def _mix_out(ys, bonus, g, ya, h, consts, *, nseq, nblk, skip):
    width = ys.shape[1]
    d = h.shape[1]
    cmap = lambda i: (i, 0)
    pmap = lambda i: ((i // nblk) * (nblk + skip) + skip + i % nblk, 0)
    wide = pl.BlockSpec((CHUNK, width), pmap)
    return pl.pallas_call(
        _mix_out_kernel,
        grid=(nseq * nblk,),
        in_specs=[pl.BlockSpec((CHUNK, width), cmap), wide, wide, wide, pl.BlockSpec((CHUNK, d), cmap)]
                 + [_full(a.shape) for a in consts],
        out_specs=pl.BlockSpec((CHUNK, d), cmap),
        out_shape=jax.ShapeDtypeStruct(h.shape, F32),
        compiler_params=_params(("parallel",)),
        name="mix_out",
    )(ys, bonus, g, ya, h, *consts)


def _cand_rows():
    k = PEER_TOPK
    pairs = []
    for a in range(k):
        nb = k // (a + 1)
        width = -(-nb // SUBLANES) * SUBLANES if nb >= SUBLANES else 1 << (nb - 1).bit_length()
        pairs += [(a, b) for b in range(width)]
    pairs += [None] * (-len(pairs) % SUBLANES)
    ids = np.asarray([k * k if p is None else p[0] * k + p[1] for p in pairs], np.int32)
    return tuple(pairs), ids


def _pick_rows(x, picks):
    wanted = [p for p in picks if p is not None]
    if len(wanted) == SUBLANES and wanted == list(range(wanted[0], wanted[0] + SUBLANES)):
        return x[wanted[0]:wanted[0] + SUBLANES]
    row = lax.broadcasted_iota(I32, (SUBLANES, x.shape[1]), 0)
    out = None
    for val in dict.fromkeys(wanted):
        piece = jnp.broadcast_to(x[val:val + 1], (SUBLANES, x.shape[1]))
        if out is None:
            out = piece
        else:
            mask = functools.reduce(jnp.logical_or, [row == r for r, p in enumerate(picks) if p == val])
            out = jnp.where(mask, piece, out)
    return out


def _peer_route_kernel(h_ref, g_ref, wq_ref, k_hi_ref, k_lo_ref, cid_ref, u_o, idx_o, gate_o,
                       s_scr, v_scr, p_scr, c_scr, e_scr, ts_scr, ex_scr, *, pairs, idx_scale):
    u = _rmsnorm(h_ref[...], g_ref[...])
    u_o[...] = u
    u_hi, u_lo = _split2(u)
    wq = wq_ref[...]
    q_hi, q_lo = _split2(_dot(u_hi, wq) + _dot(u_lo, wq))
    tm = u.shape[0]
    k = PEER_TOPK
    nchain = 2 * PEER_HEADS
    for c in range(nchain):
        qh = q_hi[:, c * N_KEYS:(c + 1) * N_KEYS]
        ql = q_lo[:, c * N_KEYS:(c + 1) * N_KEYS]
        kh = k_hi_ref[c]
        s_scr[c] = _dot_nt(kh, qh) + (_dot_nt(k_lo_ref[c], qh) + _dot_nt(kh, ql))
    rid = lax.broadcasted_iota(I32, (N_KEYS, tm), 0)

    def top_keys(i, carry):
        for c in range(nchain):
            s = s_scr[c]
            m = jnp.max(s, axis=0, keepdims=True)
            p = jnp.min(jnp.where(s == m, rid, N_KEYS), axis=0, keepdims=True)
            v_scr[c, pl.ds(i, 1), :] = m
            p_scr[c, pl.ds(i, 1), :] = p
            s_scr[c] = jnp.where(rid == p, -jnp.inf, s)
        return carry

    lax.fori_loop(0, k, top_keys, 0)

    for h in range(PEER_HEADS):
        v1, v2 = v_scr[2 * h], v_scr[2 * h + 1]
        i1, i2 = p_scr[2 * h] * N_KEYS, p_scr[2 * h + 1]
        for t in range(len(pairs) // SUBLANES):
            tile = pairs[t * SUBLANES:(t + 1) * SUBLANES]
            pa = [None if p is None else p[0] for p in tile]
            pb = [None if p is None else p[1] for p in tile]
            cand = _pick_rows(v1, pa) + _pick_rows(v2, pb)
            if None in tile:
                row = lax.broadcasted_iota(I32, cand.shape, 0)
                pad = functools.reduce(jnp.logical_or, [row == r for r, p in enumerate(tile) if p is None])
                cand = jnp.where(pad, -jnp.inf, cand)
            c_scr[h, t * SUBLANES:(t + 1) * SUBLANES, :] = cand
            e_scr[h, t * SUBLANES:(t + 1) * SUBLANES, :] = _pick_rows(i1, pa) + _pick_rows(i2, pb)
    cid = cid_ref[...]
    ncand = cid.shape[0]

    def top_cands(i, carry):
        for h in range(PEER_HEADS):
            cand = c_scr[h]
            m = jnp.max(cand, axis=0, keepdims=True)
            c = jnp.min(jnp.where(cand == m, cid, k * k), axis=0, keepdims=True)
            hit = cid == c
            e = jnp.max(jnp.where(hit, e_scr[h], -1), axis=0, keepdims=True)
            c_scr[h] = jnp.where(hit, -jnp.inf, cand)
            ts_scr[pl.ds(h * k + i, 1), :] = m
            ex_scr[pl.ds(h * k + i, 1), :] = e * idx_scale
        return carry

    lax.fori_loop(0, k, top_cands, 0)
    idx_o[...] = ex_scr[...].T
    for h in range(PEER_HEADS):
        ts = ts_scr[h * k:(h + 1) * k, :]
        ex = jnp.exp(ts - jnp.max(ts, axis=0, keepdims=True))
        gate_o[h * k:(h + 1) * k, :] = ex / jnp.sum(ex, axis=0, keepdims=True)


def _peer_route(h, consts, pairs, idx_scale):
    n, d = h.shape
    tm = CHUNK
    nsel = PEER_HEADS * PEER_TOPK
    ncand = consts[-1].shape[0]
    rows = pl.BlockSpec((tm, d), lambda i: (i, 0))
    sel = pl.BlockSpec((nsel, tm), lambda i: (0, i))
    return pl.pallas_call(
        functools.partial(_peer_route_kernel, pairs=pairs, idx_scale=idx_scale),
        grid=(n // tm,),
        in_specs=[rows] + [_full(a.shape) for a in consts],
        out_specs=[rows, pl.BlockSpec((tm, nsel), lambda i: (i, 0)), sel],
        out_shape=[jax.ShapeDtypeStruct((n, d), F32), jax.ShapeDtypeStruct((n, nsel), I32),
                   jax.ShapeDtypeStruct((nsel, n), F32)],
        scratch_shapes=[pltpu.VMEM((2 * PEER_HEADS, N_KEYS, tm), F32),
                        pltpu.VMEM((2 * PEER_HEADS, PEER_TOPK, tm), F32),
                        pltpu.VMEM((2 * PEER_HEADS, PEER_TOPK, tm), I32),
                        pltpu.VMEM((PEER_HEADS, ncand, tm), F32),
                        pltpu.VMEM((PEER_HEADS, ncand, tm), I32),
                        pltpu.VMEM((nsel, tm), F32),
                        pltpu.VMEM((nsel, tm), I32)],
        compiler_params=_params(("parallel",)),
        name="peer_route",
    )(h, *consts)


PEER_GROUP = 32
PEER_SLOTS = 2
WORDS = 2


def _pack_kernel(t_ref, o_ref):
    r = t_ref.shape[0]
    o_ref[...] = pltpu.bitcast(t_ref[...].reshape(r * SUBLANES, LANES), I32)


def _pack_table(t):
    n, d = t.shape
    rows = d // LANES
    blk = 512
    t3 = t.astype(BF16).reshape(n, rows, LANES)
    return pl.pallas_call(
        _pack_kernel,
        grid=(n // blk,),
        in_specs=[pl.BlockSpec((blk, rows, LANES), lambda i: (i, 0, 0))],
        out_specs=pl.BlockSpec((blk * rows // WORDS, LANES), lambda i: (i, 0)),
        out_shape=jax.ShapeDtypeStruct((n * rows // WORDS, LANES), I32),
        compiler_params=_params(("parallel",)),
        name="pack_table",
    )(t3)


def _for_each_token(idx_ref, ids, sem, tm, token_fn):
    ngrp = tm // PEER_GROUP

    def copy(grp, slot):
        return pltpu.make_async_copy(idx_ref.at[pl.ds(grp * PEER_GROUP, PEER_GROUP)], ids.at[slot], sem.at[slot])

    for slot in range(PEER_SLOTS):
        copy(slot, slot).start()

    def body(q, carry):
        for slot in range(PEER_SLOTS):
            grp = q * PEER_SLOTS + slot
            copy(grp, slot).wait()
            for s in range(PEER_GROUP):
                token_fn(grp * PEER_GROUP + s, slot, s)

            @pl.when(grp + PEER_SLOTS < ngrp)
            def _():
                copy(grp + PEER_SLOTS, slot).start()
        return carry

    lax.fori_loop(0, ngrp // PEER_SLOTS, body, 0)


def _gather_rows(ids, tab_ref, gbuf, slot, s, nsel):
    rows_per = SUBLANES // WORDS
    for j in range(nsel):
        e = pl.multiple_of(ids[slot, s, j], rows_per)
        gbuf[s, j * rows_per:(j + 1) * rows_per, :] = tab_ref[pl.ds(e, rows_per), :]
    return pltpu.bitcast(gbuf[s], BF16)


def _diag_mask(nsel):
    sub = lax.broadcasted_iota(I32, (SUBLANES, nsel * SUBLANES), 0)
    lane = lax.broadcasted_iota(I32, (SUBLANES, nsel * SUBLANES), 1)
    return sub == lane % SUBLANES


def _gelu(x):
    return 0.5 * x * (1.0 + lax.erf(x * np.float32(1.0 / np.sqrt(2.0))))


def _peer_u_kernel(idx_ref, tok_ref, gate_ref, tab_ref, eexp_ref, act_o, gbuf, rsum, ids, sem, *, tm, nsel):
    diag = _diag_mask(nsel)

    def token(i, slot, s):
        rows = _gather_rows(ids, tab_ref, gbuf, slot, s, nsel)
        tok = tok_ref[i].astype(BF16)
        prod = _dot_nt(tok, rows)
        rsum[pl.ds(i, 1), :] = jnp.sum(jnp.where(diag, prod, 0.0), axis=0, keepdims=True)

    _for_each_token(idx_ref, ids, sem, tm, token)
    eexp = eexp_ref[...]
    hi, mid, lo = _split3(rsum[...])
    pre_t = _dot_nt(eexp, hi) + _dot_nt(eexp, mid) + _dot_nt(eexp, lo)
    act_o[...] = _gelu(pre_t) * gate_ref[...]


def _peer_v_kernel(idx_ref, act_ref, h_ref, g_ref, tab_ref, eexp_ref, out_o, gbuf, arep, ids, sem, *, tm, nsel):
    diag = _diag_mask(nsel)
    arep[...] = _dot(act_ref[...].T.astype(BF16), eexp_ref[...])

    def token(i, slot, s):
        rows = _gather_rows(ids, tab_ref, gbuf, slot, s, nsel)
        a = jnp.broadcast_to(arep[pl.ds(i, 1), :], diag.shape)
        out_o[i] = h_ref[i] + _dot(jnp.where(diag, a, 0.0).astype(BF16), rows)

    _for_each_token(idx_ref, ids, sem, tm, token)
    x = out_o[...]
    ms = jnp.sum(jnp.sum(x * x, axis=2, keepdims=True), axis=1, keepdims=True) * (1.0 / (SUBLANES * LANES))
    out_o[...] = x * lax.rsqrt(ms + RMS_EPS) * g_ref[...]


def _peer_gather_specs(tm, nsel, tab):
    idx = pl.BlockSpec((tm, nsel), lambda i: (i, 0))
    table = pl.BlockSpec(tab.shape, lambda i: (0, 0), pipeline_mode=pl.Buffered(1))
    scratch = [pltpu.VMEM((PEER_GROUP, nsel * SUBLANES // WORDS, LANES), I32),
               pltpu.VMEM((tm, nsel * SUBLANES), F32),
               pltpu.SMEM((PEER_SLOTS, PEER_GROUP, nsel), I32),
               pltpu.SemaphoreType.DMA((PEER_SLOTS,))]
    return idx, table, scratch


def _peer_u(idx, tok3, gate, tab, eexp, *, tm):
    n, nsel = idx.shape
    assert tm % (PEER_GROUP * PEER_SLOTS) == 0
    idx_spec, tab_spec, scratch = _peer_gather_specs(tm, nsel, tab)
    sel = pl.BlockSpec((nsel, tm), lambda i: (0, i))
    return pl.pallas_call(
        functools.partial(_peer_u_kernel, tm=tm, nsel=nsel),
        grid=(n // tm,),
        in_specs=[idx_spec, pl.BlockSpec((tm, SUBLANES, LANES), lambda i: (i, 0, 0)), sel, tab_spec,
                  _full(eexp.shape)],
        out_specs=sel,
        out_shape=jax.ShapeDtypeStruct((nsel, n), F32),
        scratch_shapes=scratch,
        compiler_params=_params(("arbitrary",)),
        name="peer_u",
    )(idx, tok3, gate, tab, eexp)


def _peer_v(idx, act, h3, gain, tab, eexp, *, tm):
    n, nsel = idx.shape
    assert tm % (PEER_GROUP * PEER_SLOTS) == 0
    idx_spec, tab_spec, scratch = _peer_gather_specs(tm, nsel, tab)
    tiles = pl.BlockSpec((tm, SUBLANES, LANES), lambda i: (i, 0, 0))
    return pl.pallas_call(
        functools.partial(_peer_v_kernel, tm=tm, nsel=nsel),
        grid=(n // tm,),
        in_specs=[idx_spec, pl.BlockSpec((nsel, tm), lambda i: (0, i)), tiles, _full(gain.shape), tab_spec,
                  _full(eexp.shape)],
        out_specs=tiles,
        out_shape=jax.ShapeDtypeStruct((n, SUBLANES, LANES), F32),
        scratch_shapes=scratch,
        compiler_params=_params(("arbitrary",)),
        name="peer_v",
    )(idx, act, h3, gain, tab, eexp)


def _hi_lo(w):
    hi = w.astype(BF16)
    return hi, (w - hi.astype(F32)).astype(BF16)


def _block_ones(n, blk):
    i = np.arange(n)
    return jnp.asarray(i[:, None] // blk == i[None, :] // blk, dtype=BF16)


def _stream(x_rows, lead, conv0, ssd0, shift0, wkv0, w, *, nseq, nblk, lb, has_lead, npad, tb):
    d = x_rows.shape[1]
    nblk_all = nblk + (1 if has_lead else 0)
    width_a = w["ssm_norm"].shape[1]
    width_b = w["ones_b"].shape[0]
    heads_b = width_b // RWKV_HEAD_DIM
    if lb == CHUNK:
        xbc, z, dtp, rw = _proj_in(x_rows, lead, w["norm_mix"], w["w_in"], w["widths"],
                                   nseq=nseq, nblk=nblk_all, has_lead=has_lead)
        nchunk = nblk_all
    else:
        xbc, z, dtp, rw = _proj_in(x_rows, lead, w["norm_mix"], w["w_in"], w["widths"],
                                   nseq=1, nblk=x_rows.shape[0] // CHUNK, has_lead=False)
        nchunk = 1
    ya, conv_new, ssd_new = _ssd(xbc, z, dtp, conv0, ssd0, w["ssd"], nseq=nseq, lb=lb, nchunk=nchunk, npad=npad)
    scan_kernels = lb == CHUNK and heads_b == SUBLANES and 2 * nseq * heads_b == LANES
    *scan_in, g, bonus, shift_new = _rwkv_pre(
        rw, shift0[:, None, :], w["rwkv_pre"], nseq=nseq, lb=lb, nchunk=nchunk,
        npad=npad if lb == CHUNK else 0, width=width_b, stacked=scan_kernels)

    t_all = nchunk * lb
    t0 = npad if lb == CHUNK else 0
    t_real = t_all - t0
    pairs = nseq * heads_b
    hd = RWKV_HEAD_DIM
    if pairs >= LANES:
        ngroup, dup = pairs // LANES, 1
    else:
        ngroup, dup = 1, LANES // pairs
    seq_per = nseq // ngroup
    vr = hd // dup

    def to_scan_k(a):
        a = a.reshape(ngroup, seq_per, t_all, heads_b, hd)[:, :, t0:]
        a = a.transpose(0, 2, 4, 1, 3).reshape(ngroup, t_real, hd, seq_per * heads_b)
        return jnp.tile(a, (1, 1, 1, dup))

    def to_scan_v(a):
        a = a.reshape(ngroup, seq_per, t_all, heads_b, dup, vr)[:, :, t0:]
        return a.transpose(0, 2, 5, 4, 1, 3).reshape(ngroup, t_real, vr, LANES)

    s0 = wkv0.reshape(ngroup, seq_per, heads_b, dup, vr, hd).transpose(0, 5, 4, 3, 1, 2)
    s0 = s0.reshape(ngroup, hd, vr, LANES)
    if scan_kernels:
        ks, vs = _to_scan(scan_in[0], heads=heads_b, dup=dup, tb=tb)
        ysc, s_new = _rwkv_scan_kmajor(ks, vs, s0, tb=tb, t0=t0)
    else:
        ysc, s_new = _rwkv_scan(*[to_scan_k(a) for a in scan_in[:5]], to_scan_v(scan_in[5]), s0, tb=tb)
    wkv_new = s_new.reshape(ngroup, hd, vr, dup, seq_per, heads_b).transpose(0, 4, 5, 3, 2, 1)
    wkv_new = wkv_new.reshape(nseq, heads_b, hd, hd)
    skip_t = t_real - nblk * lb if lb == CHUNK else 0
    ys = ysc[:, skip_t:].reshape(ngroup, t_real - skip_t, vr, dup, seq_per, heads_b)
    ys = ys.transpose(0, 4, 1, 5, 3, 2).reshape(nseq * (t_real - skip_t), width_b)

    if lb == CHUNK:
        h1 = _mix_out(ys, bonus, g, ya, x_rows, w["mix_out"], nseq=nseq, nblk=nblk, skip=nchunk - nblk)
    else:
        h1 = _mix_out(ys, bonus, g, ya, x_rows, w["mix_out"], nseq=1, nblk=x_rows.shape[0] // CHUNK, skip=0)

    u, idx, gate = _peer_route(h1, w["route"], w["cand_pairs"], SUBLANES // WORDS)
    n = u.shape[0]
    act = _peer_u(idx, u.reshape(n, SUBLANES, LANES), gate, w["tab_u"], w["eexp"], tm=w["peer_tm"])
    y = _peer_v(idx, act, h1.reshape(n, SUBLANES, LANES), w["norm_final"], w["tab_v"], w["eexp"], tm=w["peer_tm"])
    return y.reshape(n, d), conv_new, ssd_new, shift_new[:, 0, :], wkv_new


def kernel(x_prompt, x_sample, state_conv, state_ssd, state_shift, state_wkv, meta_tokens, norm_mix, w_in, conv_w, conv_b, dt_bias, a_log, d_skip, ssm_norm, shift_mu, decay_w0, decay_w2, iclr_a0, iclr_a2, gate_g2, k_k, k_a, r_k, lnx_w, lnx_b, w_out, norm_ffn, w_query, sub_keys, expert_u, expert_v, norm_final):
    bp, seq_p, d = x_prompt.shape
    bs, seq_s, _ = x_sample.shape
    depth = w_in.shape[0]
    assert depth == 1 and seq_p % CHUNK == 0 and (bs * seq_s) % CHUNK == 0 and seq_s % SUBLANES == 0
    assert d == SUBLANES * LANES
    heads_a = state_ssd.shape[2]
    width_a = heads_a * SSD_HEAD_DIM
    conv_dim = state_conv.shape[3]
    rw_in = state_shift.shape[2]
    heads_b = state_wkv.shape[2]
    width_b = heads_b * RWKV_HEAD_DIM
    assert heads_a <= LANES and rw_in == 3 * width_b + DECAY_LORA + AAA_LORA + GATE_LORA

    wi = w_in[0]
    ssd_in = width_a + conv_dim + heads_a
    w_cat = jnp.concatenate([
        wi[:, width_a:width_a + conv_dim], wi[:, :width_a],
        jnp.pad(wi[:, width_a + conv_dim:ssd_in], ((0, 0), (0, LANES - heads_a))),
        wi[:, ssd_in:]], axis=1).astype(BF16)
    widths = (conv_dim, width_a, LANES, rw_in)
    pad_h = lambda a: jnp.pad(a.reshape(1, -1), ((0, 0), (0, LANES - heads_a)))
    i_l = np.arange(CHUNK)
    tri = jnp.asarray(i_l[:, None] >= i_l[None, :], dtype=BF16)
    ehead = jnp.asarray(np.arange(LANES)[:, None] == np.arange(width_a)[None, :] // SSD_HEAD_DIM, dtype=BF16)
    ecol = jnp.asarray(np.arange(LANES)[:, None] == np.arange(heads_a * CHUNK)[None, :] // CHUNK, dtype=BF16)
    ssd_consts = (conv_w[0], conv_b[0].reshape(1, -1), pad_h(dt_bias[0]), pad_h(a_log[0]),
                  jnp.repeat(d_skip[0], SSD_HEAD_DIM).reshape(1, -1), ssm_norm[0].reshape(1, -1), tri, ehead, ecol)
    ones_b = _block_ones(width_b, RWKV_HEAD_DIM)
    zero = jnp.zeros((DECAY_LORA, width_b), F32)
    w_wa = jnp.concatenate([jnp.concatenate([decay_w2[0], zero], axis=1),
                            jnp.concatenate([jnp.zeros((AAA_LORA, width_b), F32), iclr_a2[0]], axis=1)], axis=0)
    row = lambda a: a.reshape(1, -1)
    pre_consts = (row(shift_mu[0]), row(decay_w0[0]), row(iclr_a0[0]), *_hi_lo(w_wa), *_hi_lo(gate_g2[0]),
                  row(k_k[0]), row(k_a[0]), row(r_k[0]), ones_b)
    wo = w_out[0].astype(BF16)
    mix_consts = (row(lnx_w[0]), row(lnx_b[0]), ones_b, wo[:width_a], wo[width_a:])
    cand_pairs, cand_ids = _cand_rows()
    keys = sub_keys[0].transpose(1, 0, 2, 3).reshape(2 * PEER_HEADS, N_KEYS, -1)
    route_consts = (row(norm_ffn[0]), w_query[0].astype(BF16), *_hi_lo(keys),
                    jnp.asarray(np.broadcast_to(cand_ids[:, None], (cand_ids.shape[0], CHUNK))))
    nsel = PEER_HEADS * PEER_TOPK
    lane8 = np.arange(nsel * SUBLANES)
    eexp = jnp.asarray(np.arange(nsel)[:, None] == lane8[None, :] // SUBLANES, dtype=BF16)
    w = dict(norm_mix=row(norm_mix[0]), w_in=w_cat, widths=widths, ssm_norm=row(ssm_norm[0]), ones_b=ones_b,
             ssd=ssd_consts, rwkv_pre=pre_consts, mix_out=mix_consts, route=route_consts,
             tab_u=_pack_table(expert_u[0]), tab_v=_pack_table(expert_v[0]), eexp=eexp, cand_pairs=cand_pairs,
             norm_final=norm_final.reshape(SUBLANES, LANES), peer_tm=CHUNK)

    npad = CHUNK - N_META
    lead = jnp.concatenate([jnp.zeros((npad, d), F32), meta_tokens.astype(F32)], axis=0)
    zeros = lambda *s: jnp.zeros(s, F32)
    yp, cp, sp, shp, wp = _stream(
        x_prompt.reshape(bp * seq_p, d), lead,
        zeros(bp, CONV_W - 1, conv_dim), zeros(bp, heads_a, SSD_HEAD_DIM, SSD_STATE), zeros(bp, rw_in),
        zeros(bp, heads_b, RWKV_HEAD_DIM, RWKV_HEAD_DIM), w,
        nseq=bp, nblk=seq_p // CHUNK, lb=CHUNK, has_lead=True, npad=npad, tb=N_META)
    ys, cs, ss, shs, ws = _stream(
        x_sample.reshape(bs * seq_s, d), lead, state_conv[0], state_ssd[0], state_shift[0], state_wkv[0], w,
        nseq=bs, nblk=1, lb=seq_s, has_lead=False, npad=CHUNK - seq_s, tb=seq_s)
    return (yp.reshape(bp, seq_p, d), ys.reshape(bs, seq_s, d), cp[None], sp[None], shp[None], wp[None],
            cs[None], ss[None], shs[None], ws[None])
```
